```python
import math
import jax
import jax.numpy as jnp
from jax import lax
import numpy as np

D_MODEL = 1024
BATCH = 2
SEQ = 8192
DEPTH = 1

GRID_W = 64
CTX_LEN = 256
EPS = 1e-6
DA_HEADS = 4
DA_D = 64
DA_DV = 2 * DA_D
ROPE_AXIS_DIM = DA_D // 2
ROPE_THETA = 10000.0
Q_BLOCK = 128
GDN_HEADS = 4
GDN_DK = 128
GDN_DV = 128
CONV_K = 3
CHUNK = 64
N_EXPERTS = 16
EXPERT_FF = 1024
CAP_FACTOR = 2
DA_QK = DA_HEADS * 2 * DA_D
DA_V = DA_HEADS * DA_DV
GDN_QK = GDN_HEADS * GDN_DK
GDN_V = GDN_HEADS * GDN_DV
GDN_QKV = 2 * GDN_QK + GDN_V
GDN_AB = 2 * 2 * GDN_HEADS
D_MIX = DA_V + GDN_V
IN_SPLIT_POINTS = (DA_QK, 2 * DA_QK, 2 * DA_QK + DA_V, 2 * DA_QK + DA_V + GDN_QKV, 2 * DA_QK + DA_V + GDN_QKV + GDN_V)
IN_COLS = 2 * DA_QK + DA_V + GDN_QKV + GDN_V + GDN_AB

kernel_name = 'hybrid_diffattn_gdn_ecmoe_dit_layer'


def rms_norm(x, g):
    xf = x.astype(jnp.float32)
    y = xf * lax.rsqrt(jnp.mean(xf * xf, axis=-1, keepdims=True) + EPS)
    return (y * g.astype(jnp.float32)).astype(x.dtype)


def modulate(x, g, shift, scale):
    return rms_norm(x, g) * (1.0 + scale) + shift


def l2_normalize(t):
    tf = t.astype(jnp.float32)
    return (tf * lax.rsqrt(jnp.sum(tf * tf, axis=-1, keepdims=True) + EPS)).astype(t.dtype)


def rope_axis(x, ang):
    cos = jnp.cos(ang).astype(x.dtype)[:, None, None, :]
    sin = jnp.sin(ang).astype(x.dtype)[:, None, None, :]
    x1, x2 = jnp.split(x, 2, axis=-1)
    return jnp.concatenate([x1 * cos - x2 * sin, x2 * cos + x1 * sin], axis=-1)


def rope_2d(x, ang_row, ang_col):
    xr, xc = jnp.split(x, 2, axis=-1)
    return jnp.concatenate([rope_axis(xr, ang_row), rope_axis(xc, ang_col)], axis=-1)


def diff_attn(q, k, v, lam):
    B, L = q.shape[:2]
    nblk = L // Q_BLOCK
    qb = jnp.moveaxis((q * DA_D ** -0.5).reshape(B, nblk, Q_BLOCK, DA_HEADS, 2, DA_D), 1, 0)

    def block(qi):
        s = jnp.einsum('bqhid,bkhid->bhiqk', qi, k).astype(jnp.float32)
        p = jax.nn.softmax(s, axis=-1)
        a = (p[:, :, 0] - lam * p[:, :, 1]).astype(v.dtype)
        return jnp.einsum('bhqk,bkhe->bqhe', a, v)

    o = lax.map(block, qb)
    return jnp.moveaxis(o, 0, 1).reshape(B, L, DA_HEADS, DA_DV)


def short_conv(x, w):
    L = x.shape[1]
    p = CONV_K // 2
    xp = jnp.pad(x, ((0, 0), (p, p), (0, 0)))
    return jax.nn.silu(sum(xp[:, i:i + L] * w[i] for i in range(CONV_K)))


def gdn_inputs(qkv, ab, conv_w, a_log, dt_bias):
    B, L = qkv.shape[:2]
    q, k, v = jnp.split(short_conv(qkv, conv_w), [GDN_QK, 2 * GDN_QK], axis=-1)
    q = l2_normalize(q.reshape(B, L, GDN_HEADS, GDN_DK)) * GDN_DK ** -0.5
    k = l2_normalize(k.reshape(B, L, GDN_HEADS, GDN_DK))
    v = v.reshape(B, L, GDN_HEADS, GDN_DV)
    ab = ab.reshape(B, L, 2, 2, GDN_HEADS).astype(jnp.float32)
    g = -jnp.exp(a_log.astype(jnp.float32)) * jax.nn.softplus(ab[:, :, 0] + dt_bias.astype(jnp.float32))
    beta = jax.nn.sigmoid(ab[:, :, 1])
    return q, k, v, g, beta


def gdn_chunked(q, k, v, g, beta, s0, need_out):
    B, L, H, dk = k.shape
    dv = v.shape[-1]
    n = L // CHUNK
    f32 = jnp.float32

    def chunks(t):
        t = t.astype(f32).reshape(B, n, CHUNK, *t.shape[2:])
        return jnp.moveaxis(t, (1, 3), (0, 2))

    kc, vc, gc, bc = chunks(k), chunks(v), chunks(g), chunks(beta)
    Gc = jnp.cumsum(gc, axis=-1)
    diff = Gc[..., :, None] - Gc[..., None, :]
    idx = jnp.arange(CHUNK)
    strict = idx[:, None] > idx[None, :]
    dec_strict = jnp.where(strict, jnp.exp(jnp.where(strict, diff, 0.0)), 0.0)
    a_mat = jnp.eye(CHUNK, dtype=f32) + bc[..., None] * jnp.einsum('nbhid,nbhjd->nbhij', kc, kc) * dec_strict
    rhs = jnp.concatenate([(bc * jnp.exp(Gc))[..., None] * kc, bc[..., None] * vc], axis=-1)
    sol = lax.linalg.triangular_solve(a_mat, rhs, left_side=True, lower=True, unit_diagonal=True)
    w, ub = sol[..., :dk], sol[..., dk:]
    kd = kc * jnp.exp(Gc[..., -1:] - Gc)[..., None]
    g_last = jnp.exp(Gc[..., -1])[..., None, None]

    if need_out:
        qc = chunks(q)
        incl = idx[:, None] >= idx[None, :]
        dec_incl = jnp.where(incl, jnp.exp(jnp.where(incl, diff, 0.0)), 0.0)
        qd = qc * jnp.exp(Gc)[..., None]
        aqk = jnp.einsum('nbhid,nbhjd->nbhij', qc, kc) * dec_incl

        def step(s, xs):
            w_n, ub_n, kd_n, gl_n, qd_n, aqk_n = xs
            u = ub_n - jnp.einsum('bhcd,bhde->bhce', w_n, s)
            o = jnp.einsum('bhcd,bhde->bhce', qd_n, s) + jnp.einsum('bhij,bhje->bhie', aqk_n, u)
            return gl_n * s + jnp.einsum('bhcd,bhce->bhde', kd_n, u), o

        s_fin, o = lax.scan(step, s0, (w, ub, kd, g_last, qd, aqk))
        o = jnp.moveaxis(o, (0, 2), (1, 3)).reshape(B, L, H, dv).astype(v.dtype)
        return o, s_fin

    def step_state(s, xs):
        w_n, ub_n, kd_n, gl_n = xs
        u = ub_n - jnp.einsum('bhcd,bhde->bhce', w_n, s)
        return gl_n * s + jnp.einsum('bhcd,bhce->bhde', kd_n, u), None

    s_fin, _ = lax.scan(step_state, s0, (w, ub, kd, g_last))
    return None, s_fin


def bidirectional_gdn(lat, ctx_in, need_ctx_out):
    ql, kl, vl, gl, bl = lat
    qc, kc, vc, gc, bc = ctx_in
    B = ql.shape[0]
    o_lat, o_ctx = None, None
    for d in range(2):
        rev = (lambda t: jnp.flip(t, axis=1)) if d == 1 else (lambda t: t)
        s0 = jnp.zeros((B, GDN_HEADS, GDN_DK, GDN_DV), jnp.float32)
        oc, s_ctx = gdn_chunked(rev(qc), rev(kc), rev(vc), rev(gc[:, :, d]), rev(bc[:, :, d]), s0, need_ctx_out)
        ol, _ = gdn_chunked(rev(ql), rev(kl), rev(vl), rev(gl[:, :, d]), rev(bl[:, :, d]), s_ctx, True)
        o_lat = rev(ol) if o_lat is None else o_lat + rev(ol)
        if need_ctx_out:
            o_ctx = rev(oc) if o_ctx is None else o_ctx + rev(oc)
    return o_lat, o_ctx


def gdn_output(o, gate, gdn_norm_g):
    B, L = o.shape[:2]
    y = rms_norm(o, gdn_norm_g) * jax.nn.silu(gate.reshape(B, L, GDN_HEADS, GDN_DV))
    return y.reshape(B, L, GDN_V)


def expert_choice_ffn(h, w_router, w_gate, w_up, w_down):
    def route_set(t):
        n = t.shape[0]
        cap = CAP_FACTOR * n // N_EXPERTS
        aff = jax.nn.softmax((t @ w_router).astype(jnp.float32), axis=-1)
        gate, idx = lax.top_k(aff.T, cap)
        xe = t[idx]
        hid = jax.nn.silu(jnp.einsum('ecd,edf->ecf', xe, w_gate)) * jnp.einsum('ecd,edf->ecf', xe, w_up)
        ye = jnp.einsum('ecf,efd->ecd', hid, w_down) * gate[..., None].astype(t.dtype)
        return jnp.zeros_like(t).at[idx.reshape(-1)].add(ye.reshape(-1, t.shape[-1]))

    return jax.vmap(route_set)(h)


def hybrid_layer(x, xc, c, c_ctx, ang_row, ang_col, lam_init, last,
                 w_mod, b_mod, norm1_g, w_in, conv_w, a_log, dt_bias, gdn_norm_g,
                 lam_q1, lam_k1, lam_q2, lam_k2, da_subln_g, w_out, norm2_g,
                 w_router, w_gate, w_up, w_down):
    B, L, _ = x.shape
    Lc = xc.shape[1]
    sh1, sc1, gt1, sh2, sc2, gt2 = jnp.split((jax.nn.silu(c) @ w_mod + b_mod)[:, None, :], 6, axis=-1)
    sh1c, sc1c, gt1c, sh2c, sc2c, gt2c = jnp.split(jax.nn.silu(c_ctx) @ w_mod + b_mod, 6, axis=-1)

    p = modulate(x, norm1_g, sh1, sc1) @ w_in
    pc = modulate(xc, norm1_g, sh1c, sc1c) @ w_in
    q, k, v, qkv, gate, ab = jnp.split(p, IN_SPLIT_POINTS, axis=-1)
    qc, kc, vc, qkvc, gatec, abc = jnp.split(pc, IN_SPLIT_POINTS, axis=-1)

    lam = (jnp.exp(jnp.sum(lam_q1 * lam_k1, dtype=jnp.float32))
           - jnp.exp(jnp.sum(lam_q2 * lam_k2, dtype=jnp.float32)) + lam_init)
    k_c = kc.reshape(B, Lc, DA_HEADS, 2, DA_D)
    v_c = vc.reshape(B, Lc, DA_HEADS, DA_DV)
    q_l = rope_2d(q.reshape(B, L, DA_HEADS, 2, DA_D), ang_row, ang_col)
    k_l = rope_2d(k.reshape(B, L, DA_HEADS, 2, DA_D), ang_row, ang_col)
    o_da = diff_attn(q_l, jnp.concatenate([k_c, k_l], axis=1),
                     jnp.concatenate([v_c, v.reshape(B, L, DA_HEADS, DA_DV)], axis=1), lam)
    o_da = (rms_norm(o_da, da_subln_g) * (1.0 - lam_init)).reshape(B, L, DA_V)

    o_gdn_l, o_gdn_c = bidirectional_gdn(gdn_inputs(qkv, ab, conv_w, a_log, dt_bias),
                                         gdn_inputs(qkvc, abc, conv_w, a_log, dt_bias), not last)
    o_gdn = gdn_output(o_gdn_l, gate, gdn_norm_g)

    x_new = x + gt1 * (jnp.concatenate([o_da, o_gdn], axis=-1) @ w_out)
    x_new = x_new + gt2 * expert_choice_ffn(modulate(x_new, norm2_g, sh2, sc2), w_router, w_gate, w_up, w_down)

    if not last:
        o_da_c = diff_attn(qc.reshape(B, Lc, DA_HEADS, 2, DA_D), k_c, v_c, lam)
        o_da_c = (rms_norm(o_da_c, da_subln_g) * (1.0 - lam_init)).reshape(B, Lc, DA_V)
        xc = xc + gt1c * (jnp.concatenate([o_da_c, gdn_output(o_gdn_c, gatec, gdn_norm_g)], axis=-1) @ w_out)
        xc = xc + gt2c * expert_choice_ffn(modulate(xc, norm2_g, sh2c, sc2c), w_router, w_gate, w_up, w_down)
    return x_new, xc


def setup_inputs(seed: int = 0) -> dict:
    key = jax.random.key(seed)
    ks = jax.random.split(key, 24)
    D = D_MODEL
    f32 = jnp.float32

    def nrm(k, shape, s):
        return jax.random.normal(k, shape, f32) * s

    dt = jnp.exp(jax.random.uniform(ks[10], (DEPTH, 2, GDN_HEADS), f32, math.log(1e-3), math.log(1e-1)))
    return {
        'x': nrm(ks[0], (BATCH, SEQ, D), 1.0),
        'c': nrm(ks[1], (BATCH, D), 1.0),
        'ctx': nrm(ks[2], (BATCH, CTX_LEN, D), 1.0),
        'c_ctx': nrm(ks[3], (D,), 1.0),
        'w_mod': nrm(ks[4], (DEPTH, D, 6 * D), 0.5 * D ** -0.5),
        'b_mod': nrm(ks[5], (DEPTH, 6 * D), 0.01),
        'norm1_g': 1.0 + nrm(ks[6], (DEPTH, D), 0.05),
        'w_in': nrm(ks[7], (DEPTH, D, IN_COLS), D ** -0.5),
        'conv_w': nrm(ks[8], (DEPTH, CONV_K, GDN_QKV), CONV_K ** -0.5),
        'a_log': jnp.log(jax.random.uniform(ks[9], (DEPTH, 2, GDN_HEADS), f32, 1.0, 16.0)),
        'dt_bias': dt + jnp.log(-jnp.expm1(-dt)),
        'gdn_norm_g': 1.0 + nrm(ks[11], (DEPTH, GDN_DV), 0.05),
        'lam_q1': nrm(ks[12], (DEPTH, DA_D), 0.1),
        'lam_k1': nrm(ks[13], (DEPTH, DA_D), 0.1),
        'lam_q2': nrm(ks[14], (DEPTH, DA_D), 0.1),
        'lam_k2': nrm(ks[15], (DEPTH, DA_D), 0.1),
        'da_subln_g': 1.0 + nrm(ks[16], (DEPTH, DA_DV), 0.05),
        'w_out': nrm(ks[17], (DEPTH, D_MIX, D), D_MIX ** -0.5),
        'norm2_g': 1.0 + nrm(ks[18], (DEPTH, D), 0.05),
        'w_router': nrm(ks[19], (DEPTH, D, N_EXPERTS), D ** -0.5),
        'w_gate': nrm(ks[20], (DEPTH, N_EXPERTS, D, EXPERT_FF), D ** -0.5),
        'w_up': nrm(ks[21], (DEPTH, N_EXPERTS, D, EXPERT_FF), D ** -0.5),
        'w_down': nrm(ks[22], (DEPTH, N_EXPERTS, EXPERT_FF, D), EXPERT_FF ** -0.5),
        'final_g': 1.0 + nrm(ks[23], (D,), 0.05),
    }


def reference(x, c, ctx, c_ctx, w_mod, b_mod, norm1_g, w_in, conv_w, a_log, dt_bias, gdn_norm_g,
              lam_q1, lam_k1, lam_q2, lam_k2, da_subln_g, w_out, norm2_g,
              w_router, w_gate, w_up, w_down, final_g):
    f32 = jnp.float32
    L = x.shape[1]
    ROWS = L // GRID_W
    rows = jnp.repeat(jnp.arange(ROWS, dtype=f32), GRID_W)
    cols = jnp.tile(jnp.arange(GRID_W, dtype=f32), ROWS)
    inv_freq = jnp.power(ROPE_THETA, -jnp.arange(0, ROPE_AXIS_DIM, 2, dtype=f32) / ROPE_AXIS_DIM)
    ang_row = rows[:, None] * inv_freq[None, :]
    ang_col = cols[:, None] * inv_freq[None, :]

    xc = ctx
    for layer in range(DEPTH):
        lam_init = 0.8 - 0.6 * math.exp(-0.3 * layer)
        x, xc = hybrid_layer(x, xc, c, c_ctx, ang_row, ang_col, lam_init, layer == DEPTH - 1,
                             w_mod[layer], b_mod[layer], norm1_g[layer], w_in[layer], conv_w[layer],
                             a_log[layer], dt_bias[layer], gdn_norm_g[layer],
                             lam_q1[layer], lam_k1[layer], lam_q2[layer], lam_k2[layer], da_subln_g[layer],
                             w_out[layer], norm2_g[layer], w_router[layer], w_gate[layer], w_up[layer], w_down[layer])
    return rms_norm(x, final_g)
```

```python
import functools
import math

import jax
import jax.numpy as jnp
from jax import lax
from jax.experimental import pallas as pl
from jax.experimental.pallas import tpu as pltpu

F32 = jnp.float32
BF16 = jnp.bfloat16

D_MODEL = 1024
GRID_W = 64
EPS = 1e-6
DA_HEADS = 4
DA_D = 64
DA_DV = 2 * DA_D
ROPE_AXIS_DIM = DA_D // 2
ROPE_THETA = 10000.0
GDN_HEADS = 4
GDN_DK = 128
GDN_DV = 128
CHUNK = 64
N_EXPERTS = 16
CAP_FACTOR = 2
LAM_INIT = 0.8 - 0.6 * math.exp(-0.3 * 0)

DA_QK = DA_HEADS * 2 * DA_D
DA_V = DA_HEADS * DA_DV
GDN_QK = GDN_HEADS * GDN_DK
GDN_V = GDN_HEADS * GDN_DV
GDN_QKV = 2 * GDN_QK + GDN_V
GDN_AB = 2 * 2 * GDN_HEADS
COL_Q, COL_K, COL_V = 0, DA_QK, 2 * DA_QK
COL_QKV = 2 * DA_QK + DA_V
COL_GATE = COL_QKV + GDN_QKV
COL_AB = COL_GATE + GDN_V
LANES = 128
IN_COLS_PAD = COL_AB + LANES
MXU_WIDTH = 256
ATTN_COL_TILE = MXU_WIDTH
VT_ROWS = DA_DV + 8

VMEM_LIMIT = 56 * 1024 * 1024
NEG_BIG = -1e30
LOG2_E = math.log2(math.e)


def _cparams(sem):
    return pltpu.CompilerParams(dimension_semantics=sem, vmem_limit_bytes=VMEM_LIMIT)


def _sigmoid(x):
    return 1.0 / (1.0 + jnp.exp(-x))


def _silu(x):
    return x * _sigmoid(x)


def _split3(a):
    a1 = a.astype(BF16)
    r1 = a - a1.astype(F32)
    a2 = r1.astype(BF16)
    a3 = (r1 - a2.astype(F32)).astype(BF16)
    return a1, a2, a3


def _dot(a, b):
    return jnp.dot(a, b, preferred_element_type=F32)


def _dot_nt(a, b):
    return lax.dot_general(a, b, (((1,), (1,)), ((), ())), preferred_element_type=F32)


def _dot_tn(a, b):
    return lax.dot_general(a, b, (((0,), (0,)), ((), ())), preferred_element_type=F32)


def _mod_kernel(c_ref, w_ref, b_ref, o_ref):
    s = _silu(c_ref[...])
    o_ref[...] = jnp.dot(s, w_ref[...], precision=lax.Precision.HIGHEST,
                         preferred_element_type=F32) + b_ref[...]


def _mod_call(cvec, w_mod, b_mod):
    d, n = w_mod.shape
    tn = 1024
    return pl.pallas_call(
        _mod_kernel,
        grid=(n // tn,),
        in_specs=[pl.BlockSpec((8, d), lambda j: (0, 0)),
                  pl.BlockSpec((d, tn), lambda j: (0, j)),
                  pl.BlockSpec((1, tn), lambda j: (0, j))],
        out_specs=pl.BlockSpec((8, tn), lambda j: (0, j)),
        out_shape=jax.ShapeDtypeStruct((8, n), F32),
        compiler_params=_cparams(("arbitrary",)),
        name="mod",
    )(cvec, w_mod, b_mod.reshape(1, n))


def _inproj_kernel(x_ref, g_ref, sh_ref, sc_ref, cos_ref, sin_ref, w_ref,
                   q_ref, k_ref, vt_ref, qkv_ref, gate_ref, ab_ref):
    xb = x_ref[0]
    tm = xb.shape[0]
    ms = jnp.mean(xb * xb, axis=-1, keepdims=True)
    hm = (xb * lax.rsqrt(ms + EPS) * g_ref[...]) * (1.0 + sc_ref[0]) + sh_ref[0]
    hb = hm.astype(BF16)

    def proj(lo, hi):
        return _dot(hb, w_ref[:, lo:hi])

    cos = cos_ref[...]
    sin = sin_ref[...]
    lane = lax.broadcasted_iota(jnp.int32, (tm, LANES), 1)
    first_half = (lane % ROPE_AXIS_DIM) < (ROPE_AXIS_DIM // 2)

    def rope(t):
        outs = []
        for j in range(t.shape[1] // LANES):
            s = t[:, LANES * j:LANES * (j + 1)]
            partner = jnp.where(first_half,
                                pltpu.roll(s, LANES - ROPE_AXIS_DIM // 2, 1),
                                pltpu.roll(s, ROPE_AXIS_DIM // 2, 1))
            outs.append(s * cos + partner * sin)
        return jnp.concatenate(outs, axis=1)

    q_ref[0] = (rope(proj(COL_Q, COL_K)) * (DA_D ** -0.5 * LOG2_E)).astype(BF16)
    k_ref[0] = rope(proj(COL_K, COL_V)).astype(BF16)
    vt = proj(COL_V, COL_QKV).T.astype(BF16)
    vt_ref[0, :, 0, 0:DA_DV, :] = vt.reshape(DA_HEADS, DA_DV, tm)
    extra = lax.broadcasted_iota(jnp.int32, (DA_HEADS, VT_ROWS - DA_DV, tm), 1)
    vt_ref[0, :, 0, DA_DV:VT_ROWS, :] = jnp.where(extra == 0, 1.0, 0.0).astype(BF16)
    qkv_ref[0] = proj(COL_QKV, COL_GATE)
    gate_ref[0] = proj(COL_GATE, COL_AB)
    ab_ref[0] = proj(COL_AB, IN_COLS_PAD)


def _inproj_call(x, norm_g, shift, scale, cos, sin, w_bf, tm):
    b, n, d = x.shape
    row = lambda bi, i: (bi, i, 0)
    vec = lambda bi, i: (bi, 0, 0)
    outs = [(DA_QK, BF16), (DA_QK, BF16), None, (GDN_QKV, F32), (GDN_V, F32), (LANES, F32)]
    out_specs = [pl.BlockSpec((1, tm, o[0]), row) if o else
                 pl.BlockSpec((1, DA_HEADS, 1, VT_ROWS, tm), lambda bi, i: (bi, 0, i, 0, 0)) for o in outs]
    out_shape = [jax.ShapeDtypeStruct((b, n, o[0]), o[1]) if o else
                 jax.ShapeDtypeStruct((b, DA_HEADS, n // tm, VT_ROWS, tm), BF16) for o in outs]
    return pl.pallas_call(
        _inproj_kernel,
        grid=(b, n // tm),
        in_specs=[pl.BlockSpec((1, tm, d), row),
                  pl.BlockSpec((1, d), lambda bi, i: (0, 0)),
                  pl.BlockSpec((1, 1, d), vec),
                  pl.BlockSpec((1, 1, d), vec),
                  pl.BlockSpec((tm, LANES), lambda bi, i: (i, 0)),
                  pl.BlockSpec((tm, LANES), lambda bi, i: (i, 0)),
                  pl.BlockSpec((d, IN_COLS_PAD), lambda bi, i: (0, 0))],
        out_specs=out_specs,
        out_shape=out_shape,
        compiler_params=_cparams(("parallel", "parallel")),
        name="inproj",
    )(x, norm_g, shift, scale, cos, sin, w_bf)


def _attn_kernel(q_ref, k_ref, vt_ref, lam_ref, g_ref, o_ref,
                 q2_ref, acc_ref, m_ref, s_ref, p_ref, a_ref, *, tk):
    q = q_ref[0]
    tq = q.shape[0]
    lane = lax.broadcasted_iota(jnp.int32, (tq, LANES), 1)
    zero = jnp.zeros_like(q)
    q2_ref[0:tq, :] = jnp.where(lane < DA_D, q, zero)
    q2_ref[tq:2 * tq, :] = jnp.where(lane >= DA_D, q, zero)
    m_ref[...] = jnp.full(m_ref.shape, NEG_BIG, F32)
    acc_ref[...] = jnp.zeros(acc_ref.shape, F32)

    nct = 2 * tq // ATTN_COL_TILE
    col = lambda ct: slice(ct * ATTN_COL_TILE, (ct + 1) * ATTN_COL_TILE)

    def scores(kk, ct):
        return _dot_nt(kk, q2_ref[col(ct), :])

    def softmax_update(ct, s):
        cols = col(ct)
        m_old = m_ref[:, cols]
        m_new = jnp.maximum(m_old, jnp.max(s, axis=0, keepdims=True))
        alpha = jnp.exp2(m_old - m_new)
        p = jnp.exp2(s - m_new)
        m_ref[:, cols] = m_new
        return p.astype(BF16), alpha

    def value_update(ct, vt, p, alpha):
        cols = col(ct)
        acc_ref[:, cols] = alpha * acc_ref[:, cols] + _dot(vt, p)

    nchunk = k_ref.shape[1] // tk
    last = nct - 1
    s_ref[...] = scores(k_ref[0, 0:tk, :], 0)
    p_ref[...] = jnp.zeros(p_ref.shape, BF16)
    a_ref[...] = jnp.ones(a_ref.shape, F32)

    def body(j, carry):
        kk = k_ref[0, pl.ds(pl.multiple_of(j * tk, tk), tk), :]
        vt = vt_ref[0, 0, j]
        s_cur = s_ref[...]
        pending = (last, vt_ref[0, 0, jnp.maximum(j - 1, 0)], p_ref[...], a_ref[...])
        for ct in range(nct):
            if ct < last:
                s_ahead = scores(kk, ct + 1)
            else:
                jn = jnp.minimum(j + 1, nchunk - 1)
                s_ahead = scores(k_ref[0, pl.ds(pl.multiple_of(jn * tk, tk), tk), :], 0)
            p, alpha = softmax_update(ct, s_cur)
            value_update(*pending)
            pending = (ct, vt, p, alpha)
            s_cur = s_ahead
        s_ref[...] = s_cur
        p_ref[...] = pending[2]
        a_ref[...] = pending[3]
        return carry

    lax.fori_loop(0, nchunk, body, 0)
    value_update(last, vt_ref[0, 0, nchunk - 1], p_ref[...], a_ref[...])

    lv = lam_ref[...]
    lam = (jnp.exp(jnp.sum(lv[0:1] * lv[1:2], axis=1, keepdims=True))
           - jnp.exp(jnp.sum(lv[2:3] * lv[3:4], axis=1, keepdims=True)) + LAM_INIT)
    den = acc_ref[DA_DV:DA_DV + 1, :]
    ot = (acc_ref[0:DA_DV, 0:tq] / den[:, 0:tq]
          - lam * (acc_ref[0:DA_DV, tq:2 * tq] / den[:, tq:2 * tq]))
    o = ot.T
    ms = jnp.mean(o * o, axis=-1, keepdims=True)
    o_ref[0] = ((o * lax.rsqrt(ms + EPS) * g_ref[...]) * (1.0 - LAM_INIT)).astype(o_ref.dtype)


def _attn_call(q, k_all, vt_all, lam_vecs, subln_g, tq):
    b, n, _ = q.shape
    nk = k_all.shape[1]
    _, _, nchunk, _, tk = vt_all.shape
    return pl.pallas_call(
        functools.partial(_attn_kernel, tk=tk),
        grid=(b, DA_HEADS, n // tq),
        in_specs=[pl.BlockSpec((1, tq, LANES), lambda bi, h, i: (bi, i, h)),
                  pl.BlockSpec((1, nk, LANES), lambda bi, h, i: (bi, 0, h)),
                  pl.BlockSpec((1, 1, nchunk, VT_ROWS, tk), lambda bi, h, i: (bi, h, 0, 0, 0)),
                  pl.BlockSpec((4, DA_D), lambda bi, h, i: (0, 0)),
                  pl.BlockSpec((1, DA_DV), lambda bi, h, i: (0, 0))],
        out_specs=pl.BlockSpec((1, tq, LANES), lambda bi, h, i: (bi, i, h)),
        out_shape=jax.ShapeDtypeStruct((b, n, DA_V), BF16),
        scratch_shapes=[pltpu.VMEM((2 * tq, LANES), BF16), pltpu.VMEM((VT_ROWS, 2 * tq), F32),
                        pltpu.VMEM((1, 2 * tq), F32),
                        pltpu.VMEM((tk, ATTN_COL_TILE), F32), pltpu.VMEM((tk, ATTN_COL_TILE), BF16),
                        pltpu.VMEM((1, ATTN_COL_TILE), F32)],
        compiler_params=_cparams(("parallel", "parallel", "parallel")),
        name="attn",
    )(q, k_all, vt_all, lam_vecs, subln_g)


def _gdnpre_kernel(x_ref, xp_ref, xn_ref, cw_ref, ab_ref, al_ref, dtb_ref,
                   q_ref, k_ref, v_ref, gb_ref, gbt_ref):
    i = pl.program_id(1)
    nblk = pl.num_programs(1)
    x = x_ref[0]
    tm = x.shape[0]
    prev = jnp.where(i > 0, xp_ref[0][7:8, :], 0.0)
    nxt = jnp.where(i < nblk - 1, xn_ref[0][0:1, :], 0.0)
    row = lax.broadcasted_iota(jnp.int32, (tm, 1), 0)
    xm1 = jnp.where(row == 0, prev, pltpu.roll(x, 1, 0))
    xp1 = jnp.where(row == tm - 1, nxt, pltpu.roll(x, tm - 1, 0))
    w = cw_ref[...]
    s = _silu(xm1 * w[0:1] + x * w[1:2] + xp1 * w[2:3])

    def l2n(t):
        return t * lax.rsqrt(jnp.sum(t * t, axis=-1, keepdims=True) + EPS)

    for h in range(GDN_HEADS):
        lo, hi = h * GDN_DK, (h + 1) * GDN_DK
        q_ref[0, :, lo:hi] = l2n(s[:, lo:hi]) * GDN_DK ** -0.5
        k_ref[0, :, lo:hi] = l2n(s[:, GDN_QK + lo:GDN_QK + hi])
    v_ref[0] = s[:, 2 * GDN_QK:]

    ab = ab_ref[0]
    lane = lax.broadcasted_iota(jnp.int32, ab.shape, 1)
    z = ab + dtb_ref[...]
    softplus = jnp.maximum(z, 0.0) + jnp.log1p(jnp.exp(-jnp.abs(z)))
    gval = -jnp.exp(al_ref[...]) * softplus
    gb = jnp.where(lane < GDN_AB // 2, gval, jnp.where(lane < GDN_AB, _sigmoid(ab), 0.0))
    gb_ref[0] = gb
    gbt_ref[0] = gb.T[:GDN_AB, :]


def _gdnpre_call(qkv, ab, conv_w8, alog_row, dtb_row, tm):
    b, n, c = qkv.shape
    nb8 = n // 8
    step8 = tm // 8
    row = lambda bi, i: (bi, i, 0)
    return pl.pallas_call(
        _gdnpre_kernel,
        grid=(b, n // tm),
        in_specs=[pl.BlockSpec((1, tm, c), row),
                  pl.BlockSpec((1, 8, c), lambda bi, i: (bi, jnp.maximum(i * step8 - 1, 0), 0)),
                  pl.BlockSpec((1, 8, c), lambda bi, i: (bi, jnp.minimum((i + 1) * step8, nb8 - 1), 0)),
                  pl.BlockSpec((8, c), lambda bi, i: (0, 0)),
                  pl.BlockSpec((1, tm, LANES), row),
                  pl.BlockSpec((1, LANES), lambda bi, i: (0, 0)),
                  pl.BlockSpec((1, LANES), lambda bi, i: (0, 0))],
        out_specs=[pl.BlockSpec((1, tm, GDN_QK), row), pl.BlockSpec((1, tm, GDN_QK), row),
                   pl.BlockSpec((1, tm, GDN_V), row), pl.BlockSpec((1, tm, LANES), row),
                   pl.BlockSpec((1, GDN_AB, tm), lambda bi, i: (bi, 0, i))],
        out_shape=[jax.ShapeDtypeStruct((b, n, GDN_QK), F32), jax.ShapeDtypeStruct((b, n, GDN_QK), F32),
                   jax.ShapeDtypeStruct((b, n, GDN_V), F32), jax.ShapeDtypeStruct((b, n, LANES), F32),
                   jax.ShapeDtypeStruct((b, GDN_AB, n), F32)],
        compiler_params=_cparams(("parallel", "parallel")),
        name="gdn_pre",
    )(qkv, qkv, qkv, conv_w8, ab, alog_row, dtb_row)


def _solve_unit_triangular(a, rhs, bd_mask):
    bf = lambda t: t.astype(BF16)
    d = jnp.where(bd_mask, a, 0.0)
    n = a - d
    d2 = _dot(bf(d), bf(d))
    p = (-d + d2) - _dot(bf(d), bf(d2))
    d4 = _dot(bf(d2), bf(d2))
    p = p + d4 + _dot(bf(p), bf(d4))
    d8 = _dot(bf(d4), bf(d4))
    p = p + d8 + _dot(bf(p), bf(d8))
    m = n + _dot(bf(p), bf(n))
    z = rhs + _dot(bf(p), bf(rhs))
    m2 = _dot(bf(m), bf(m))
    z = z + _dot(bf(m2), bf(z))
    return z - _dot(bf(m), bf(z))


def _gdn_scan_kernel(q_ref, k_ref, v_ref, gb_ref, gbt_ref, s0_ref, o_ref, sfin_ref, s_ref,
                     *, rev, direction):
    i = pl.program_id(1)

    @pl.when(i == 0)
    def _():
        s_ref[...] = s0_ref[0]

    c64 = CHUNK
    nchunk = q_ref.shape[1] // c64
    ri = lax.broadcasted_iota(jnp.int32, (c64, c64), 0)
    ci = lax.broadcasted_iota(jnp.int32, (c64, c64), 1)
    if rev:
        strict, incl = ri < ci, ri <= ci
    else:
        strict, incl = ri > ci, ri >= ci
    tri_c = jnp.where(incl, 1.0, 0.0).astype(BF16)
    tri_r = jnp.where((ri >= ci) if rev else (ri <= ci), 1.0, 0.0).astype(BF16)
    bd_mask = (ri // 16) == (ci // 16)
    last = 0 if rev else c64 - 1

    order = range(nchunk - 1, -1, -1) if rev else range(nchunk)
    for c in order:
        rows = slice(c * c64, (c + 1) * c64)
        gall = gb_ref[0, rows, :]
        gt = gbt_ref[0, :, rows]
        g1, g2, g3 = _split3(gall)
        gc_col = _dot(tri_c, g1) + _dot(tri_c, g2) + _dot(tri_c, g3)
        t1, t2, t3 = _split3(gt)
        gc_row = _dot(t1, tri_r) + _dot(t2, tri_r) + _dot(t3, tri_r)
        for h in range(GDN_HEADS):
            ln = direction * GDN_HEADS + h
            cols = slice(h * GDN_DK, (h + 1) * GDN_DK)
            k = k_ref[0, rows, cols]
            q = q_ref[0, rows, cols]
            v = v_ref[0, rows, cols]
            gcol = gc_col[:, ln:ln + 1]
            grow = gc_row[ln:ln + 1, :]
            bcol = gall[:, GDN_AB // 2 + ln:GDN_AB // 2 + ln + 1]
            diff = gcol - grow
            e_strict = jnp.where(strict, jnp.exp(jnp.where(strict, diff, 0.0)), 0.0)
            e_incl = jnp.where(incl, jnp.exp(jnp.where(incl, diff, 0.0)), 0.0)
            kb = k.astype(BF16)
            kk = _dot_nt(kb, kb)
            qk = _dot_nt(q.astype(BF16), kb)
            a = bcol * kk * e_strict
            eg = jnp.exp(gcol)
            rhs = jnp.concatenate([(bcol * eg) * k, bcol * v], axis=1)
            sol = _solve_unit_triangular(a, rhs, bd_mask)
            w = sol[:, :GDN_DK]
            ub = sol[:, GDN_DK:]
            glast = gcol[last:last + 1, :]
            kd = k * jnp.exp(glast - gcol)
            qd = q * eg
            aqk = qk * e_incl

            s = s_ref[h]
            sb = s.astype(BF16)
            u = ub - _dot(w.astype(BF16), sb)
            ubf = u.astype(BF16)
            o = _dot(qd.astype(BF16), sb) + _dot(aqk.astype(BF16), ubf)
            s_ref[h] = jnp.exp(glast) * s + _dot_tn(kd.astype(BF16), ubf)
            o_ref[0, rows, cols] = o

    @pl.when(i == pl.num_programs(1) - 1)
    def _():
        sfin_ref[0] = s_ref[...]


def _gdn_scan_call(q, k, v, gb, gbt, s0, direction, sc):
    b, n, _ = q.shape
    nsup = n // sc
    rev = direction == 1
    blk = (lambda i: nsup - 1 - i) if rev else (lambda i: i)
    row = lambda bi, i: (bi, blk(i), 0)
    st = lambda bi, i: (bi, 0, 0, 0)
    return pl.pallas_call(
        functools.partial(_gdn_scan_kernel, rev=rev, direction=direction),
        grid=(b, nsup),
        in_specs=[pl.BlockSpec((1, sc, GDN_QK), row), pl.BlockSpec((1, sc, GDN_QK), row),
                  pl.BlockSpec((1, sc, GDN_V), row), pl.BlockSpec((1, sc, LANES), row),
                  pl.BlockSpec((1, GDN_AB, sc), lambda bi, i: (bi, 0, blk(i))),
                  pl.BlockSpec((1, GDN_HEADS, GDN_DK, GDN_DV), st)],
        out_specs=[pl.BlockSpec((1, sc, GDN_V), row),
                   pl.BlockSpec((1, GDN_HEADS, GDN_DK, GDN_DV), st)],
        out_shape=[jax.ShapeDtypeStruct((b, n, GDN_V), F32),
                   jax.ShapeDtypeStruct((b, GDN_HEADS, GDN_DK, GDN_DV), F32)],
        scratch_shapes=[pltpu.VMEM((GDN_HEADS, GDN_DK, GDN_DV), F32)],
        compiler_params=_cparams(("parallel", "arbitrary")),
        name="gdn_scan_d%d" % direction,
    )(q, k, v, gb, gbt, s0)


def _outproj_kernel(oda_ref, of_ref, ob_ref, gate_ref, x_ref, gt1_ref, gng_ref, wout_ref,
                    n2g_ref, sh2_ref, sc2_ref, wr_ref, xnew_ref, h_ref, aff_ref, afft_ref):
    og = of_ref[0] + ob_ref[0]
    gate = gate_ref[0]
    ys = []
    for h in range(GDN_HEADS):
        cols = slice(h * GDN_DV, (h + 1) * GDN_DV)
        t = og[:, cols]
        y = t * lax.rsqrt(jnp.mean(t * t, axis=-1, keepdims=True) + EPS) * gng_ref[...]
        ys.append((y * _silu(gate[:, cols])).astype(BF16))
    y_gdn = jnp.concatenate(ys, axis=1)
    proj = _dot(oda_ref[0], wout_ref[0:DA_V, :]) + _dot(y_gdn, wout_ref[DA_V:, :])
    xn = x_ref[0] + gt1_ref[0] * proj
    xnew_ref[0] = xn
    ms = jnp.mean(xn * xn, axis=-1, keepdims=True)
    hm = (xn * lax.rsqrt(ms + EPS) * n2g_ref[...]) * (1.0 + sc2_ref[0]) + sh2_ref[0]
    h_ref[0] = hm.astype(BF16)
    logits = jnp.dot(hm, wr_ref[...], precision=lax.Precision.HIGHEST, preferred_element_type=F32)
    lane = lax.broadcasted_iota(jnp.int32, logits.shape, 1)
    logits = jnp.where(lane < N_EXPERTS, logits, NEG_BIG)
    e = jnp.exp(logits - jnp.max(logits, axis=-1, keepdims=True))
    aff = e / jnp.sum(e, axis=-1, keepdims=True)
    aff_ref[0] = aff
    afft_ref[0] = aff.T[:N_EXPERTS, :]


def _outproj_call(o_da, o_f, o_b, gate, x, gt1, gdn_norm_g, w_out_bf, norm2_g, sh2, sc2, wr_pad, tm):
    b, n, d = x.shape
    row = lambda bi, i: (bi, i, 0)
    vec = lambda bi, i: (bi, 0, 0)
    const2 = lambda bi, i: (0, 0)
    return pl.pallas_call(
        _outproj_kernel,
        grid=(b, n // tm),
        in_specs=[pl.BlockSpec((1, tm, DA_V), row), pl.BlockSpec((1, tm, GDN_V), row),
                  pl.BlockSpec((1, tm, GDN_V), row), pl.BlockSpec((1, tm, GDN_V), row),
                  pl.BlockSpec((1, tm, d), row), pl.BlockSpec((1, 1, d), vec),
                  pl.BlockSpec((1, GDN_DV), const2), pl.BlockSpec((DA_V + GDN_V, d), const2),
                  pl.BlockSpec((1, d), const2), pl.BlockSpec((1, 1, d), vec), pl.BlockSpec((1, 1, d), vec),
                  pl.BlockSpec((d, LANES), const2)],
        out_specs=[pl.BlockSpec((1, tm, d), row), pl.BlockSpec((1, tm, d), row),
                   pl.BlockSpec((1, tm, LANES), row),
                   pl.BlockSpec((1, N_EXPERTS, tm), lambda bi, i: (bi, 0, i))],
        out_shape=[jax.ShapeDtypeStruct((b, n, d), F32), jax.ShapeDtypeStruct((b, n, d), BF16),
                   jax.ShapeDtypeStruct((b, n, LANES), F32),
                   jax.ShapeDtypeStruct((b, N_EXPERTS, n), F32)],
        compiler_params=_cparams(("parallel", "parallel")),
        name="outproj",
    )(o_da, o_f, o_b, gate, x, gt1, gdn_norm_g, w_out_bf, norm2_g, sh2, sc2, wr_pad)


PREFIX_BLK = 256
BISECT_STEPS = 64


def _select_kernel(afft_ref, aff_ref, post_ref, pos_ref, st_ref, *, cap, tb):
    at = afft_ref[0]
    n = at.shape[1]

    def count_ge(t):
        return jnp.sum(jnp.where(at >= t, 1.0, 0.0), axis=1, keepdims=True)

    def bisect(_, bounds):
        lo, hi = bounds
        mid = 0.5 * (lo + hi)
        ok = count_ge(mid) >= cap
        return jnp.where(ok, mid, lo), jnp.where(ok, hi, mid)

    _, hi = lax.fori_loop(0, BISECT_STEPS, bisect,
                          (jnp.zeros((N_EXPERTS, 1), F32), jnp.full((N_EXPERTS, 1), 2.0, F32)))

    def below(h):
        return jnp.max(jnp.where(at < h, at, -1.0), axis=1, keepdims=True)

    def short(state):
        return jnp.sum(jnp.where(count_ge(state[0]) < cap, 1.0, 0.0)) > 0.0

    def step_down(state):
        t, h = state
        lacking = count_ge(t) < cap
        h = jnp.where(lacking, t, h)
        return jnp.where(lacking, below(h), t), h

    thr, _ = lax.while_loop(short, step_down, (below(hi), hi))
    need = cap - jnp.sum(jnp.where(at > thr, 1.0, 0.0), axis=1, keepdims=True)

    pi = lax.broadcasted_iota(jnp.int32, (PREFIX_BLK, PREFIX_BLK), 0)
    pj = lax.broadcasted_iota(jnp.int32, (PREFIX_BLK, PREFIX_BLK), 1)
    upper = jnp.where(pi <= pj, 1.0, 0.0).astype(BF16)
    lower = jnp.where(pi >= pj, 1.0, 0.0).astype(BF16)

    def prefix_lanes(m):
        carry = jnp.zeros((m.shape[0], 1), F32)
        outs = []
        for c in range(m.shape[1] // PREFIX_BLK):
            mc = m[:, c * PREFIX_BLK:(c + 1) * PREFIX_BLK]
            loc = _dot(mc.astype(BF16), upper)
            outs.append(loc - mc + carry)
            carry = carry + loc[:, PREFIX_BLK - 1:PREFIX_BLK]
        return jnp.concatenate(outs, axis=1)

    def prefix_rows(m):
        carry = jnp.zeros((1, m.shape[1]), F32)
        outs = []
        for c in range(m.shape[0] // PREFIX_BLK):
            mc = m[c * PREFIX_BLK:(c + 1) * PREFIX_BLK, :]
            loc = _dot(lower, mc.astype(BF16))
            outs.append(loc - mc + carry)
            carry = carry + loc[PREFIX_BLK - 1:PREFIX_BLK, :]
        return jnp.concatenate(outs, axis=0)

    eq_t = jnp.where(at == thr, 1.0, 0.0)
    sel_t = jnp.where(at > thr, 1.0, jnp.where(prefix_lanes(eq_t) < need, eq_t, 0.0))
    post_ref[0] = jnp.where(sel_t > 0.0, prefix_lanes(sel_t), -1.0)

    ti = lax.broadcasted_iota(jnp.int32, (n, LANES), 0)
    bi = lax.broadcasted_iota(jnp.int32, (n, LANES), 1)
    before = jnp.where(ti < bi * tb, 1.0, 0.0).astype(BF16)
    st_ref[0] = _dot(sel_t.astype(BF16), before).astype(jnp.int32)

    a = aff_ref[0]
    er = lax.broadcasted_iota(jnp.int32, (N_EXPERTS, LANES), 0)
    ec = lax.broadcasted_iota(jnp.int32, (N_EXPERTS, LANES), 1)
    diag = er == ec
    thr_row = jnp.sum(jnp.where(diag, thr, 0.0), axis=0, keepdims=True)
    need_row = jnp.sum(jnp.where(diag, need, 0.0), axis=0, keepdims=True)
    valid = lax.broadcasted_iota(jnp.int32, a.shape, 1) < N_EXPERTS
    eq = jnp.where(valid & (a == thr_row), 1.0, 0.0)
    sel = jnp.where(valid & (a > thr_row), 1.0, jnp.where(prefix_rows(eq) < need_row, eq, 0.0))
    pos_ref[0] = jnp.where(sel > 0.0, prefix_rows(sel), -1.0)


def _select_call(afft, aff, cap, tb):
    b, e, n = afft.shape
    return pl.pallas_call(
        functools.partial(_select_kernel, cap=cap, tb=tb),
        grid=(b,),
        in_specs=[pl.BlockSpec((1, e, n), lambda bi: (bi, 0, 0)),
                  pl.BlockSpec((1, n, LANES), lambda bi: (bi, 0, 0))],
        out_specs=[pl.BlockSpec((1, e, n), lambda bi: (bi, 0, 0)),
                   pl.BlockSpec((1, n, LANES), lambda bi: (bi, 0, 0)),
                   pl.BlockSpec((1, e, LANES), lambda bi: (bi, 0, 0))],
        out_shape=[jax.ShapeDtypeStruct((b, e, n), F32), jax.ShapeDtypeStruct((b, n, LANES), F32),
                   jax.ShapeDtypeStruct((b, e, LANES), jnp.int32)],
        compiler_params=_cparams(("parallel",)),
        name="select",
    )(afft, aff)


def _moe_ffn_kernel(st_ref, h_ref, post_ref, afft_ref, wg_ref, wu_ref, wd_ref, ye_ref,
                    acc_ref, gacc_ref, *, tj, tb):
    b = pl.program_id(0)
    e = pl.program_id(1)
    cap = ye_ref.shape[2]
    nblk = h_ref.shape[1] // tb
    slot = lax.broadcasted_iota(jnp.int32, (tj, tb), 0).astype(F32)

    for j in range(cap // tj):
        lo = j * tj
        acc_ref[...] = jnp.zeros(acc_ref.shape, F32)
        gacc_ref[...] = jnp.zeros(gacc_ref.shape, F32)

        def blk_body(bi, carry):
            s0 = st_ref[b, e, bi]
            s1 = st_ref[b, e, bi + 1]

            @pl.when((s1 > lo) & (s0 < lo + tj))
            def _():
                prow = post_ref[0, 0, pl.ds(bi, 1), :]
                hit = prow == (slot + float(lo))
                onehot = jnp.where(hit, 1.0, 0.0).astype(BF16)
                off = pl.multiple_of(bi * tb, tb)
                acc_ref[...] += _dot(onehot, h_ref[0, pl.ds(off, tb), :])
                arow = afft_ref[0, 0, pl.ds(bi, 1), :]
                gacc_ref[...] += jnp.sum(jnp.where(hit, arow, 0.0), axis=1, keepdims=True)

            return carry

        lax.fori_loop(0, nblk, blk_body, 0)
        xe = acc_ref[...].astype(BF16)
        hid = _silu(_dot(xe, wg_ref[0])) * _dot(xe, wu_ref[0])
        ye = _dot(hid.astype(BF16), wd_ref[0]) * gacc_ref[...]
        ye_ref[0, 0, lo:lo + tj, :] = ye.astype(ye_ref.dtype)


def _moe_ffn_call(starts, h, post4, afft4, wg, wu, wd, cap, tj, tb):
    b, n, d = h.shape
    e, _, f = wg.shape
    nblk = n // tb
    wspec = lambda shp: pl.BlockSpec((1,) + shp, lambda bi, ei: (ei, 0, 0))
    return pl.pallas_call(
        functools.partial(_moe_ffn_kernel, tj=tj, tb=tb),
        grid=(b, e),
        in_specs=[pl.BlockSpec(memory_space=pltpu.SMEM),
                  pl.BlockSpec((1, n, d), lambda bi, ei: (bi, 0, 0), pipeline_mode=pl.Buffered(1)),
                  pl.BlockSpec((1, 1, nblk, tb), lambda bi, ei: (bi, ei, 0, 0)),
                  pl.BlockSpec((1, 1, nblk, tb), lambda bi, ei: (bi, ei, 0, 0)),
                  wspec((d, f)), wspec((d, f)), wspec((f, d))],
        out_specs=pl.BlockSpec((1, 1, cap, d), lambda bi, ei: (bi, ei, 0, 0)),
        out_shape=jax.ShapeDtypeStruct((b, e, cap, d), BF16),
        scratch_shapes=[pltpu.VMEM((tj, d), F32), pltpu.VMEM((tj, 1), F32)],
        compiler_params=_cparams(("parallel", "arbitrary")),
        name="moe_ffn",
    )(starts, h, post4, afft4, wg, wu, wd)


def _combine_kernel(st_ref, pos_ref, ye_ref, o_ref, acc_ref, *, tj, tb):
    b = pl.program_id(0)
    blk = pl.program_id(2)
    cap = ye_ref.shape[2]
    acc_ref[...] = jnp.zeros(acc_ref.shape, F32)
    slot = lax.broadcasted_iota(jnp.int32, (tb, tj), 1).astype(F32)
    pos = pos_ref[0]
    for e in range(N_EXPERTS):
        s0 = st_ref[b, e, blk]
        s1 = st_ref[b, e, blk + 1]
        pcol = pos[:, e:e + 1]
        for j in range(cap // tj):
            lo = j * tj

            @pl.when((s1 > lo) & (s0 < lo + tj))
            def _():
                onehot = jnp.where(pcol == (slot + float(lo)), 1.0, 0.0).astype(BF16)
                acc_ref[...] += _dot(onehot, ye_ref[0, e, lo:lo + tj, :])

    o_ref[0] = acc_ref[...]


def _combine_call(starts, pos, ye, tj, tb, dh):
    b, e, cap, d = ye.shape
    n = pos.shape[1]
    return pl.pallas_call(
        functools.partial(_combine_kernel, tj=tj, tb=tb),
        grid=(b, d // dh, n // tb),
        in_specs=[pl.BlockSpec(memory_space=pltpu.SMEM),
                  pl.BlockSpec((1, tb, LANES), lambda bi, c, i: (bi, i, 0)),
                  pl.BlockSpec((1, e, cap, dh), lambda bi, c, i: (bi, 0, 0, c))],
        out_specs=pl.BlockSpec((1, tb, dh), lambda bi, c, i: (bi, i, c)),
        out_shape=jax.ShapeDtypeStruct((b, n, d), F32),
        scratch_shapes=[pltpu.VMEM((tb, dh), F32)],
        compiler_params=_cparams(("parallel", "parallel", "arbitrary")),
        name="combine",
    )(starts, pos, ye)


def _final_kernel(x_ref, moe_ref, gt2_ref, g_ref, o_ref):
    y = x_ref[0] + gt2_ref[0] * moe_ref[0]
    ms = jnp.mean(y * y, axis=-1, keepdims=True)
    o_ref[0] = y * lax.rsqrt(ms + EPS) * g_ref[...]


def _final_call(x_new, moe, gt2, final_g, tm):
    b, n, d = x_new.shape
    row = lambda bi, i: (bi, i, 0)
    return pl.pallas_call(
        _final_kernel,
        grid=(b, n // tm),
        in_specs=[pl.BlockSpec((1, tm, d), row), pl.BlockSpec((1, tm, d), row),
                  pl.BlockSpec((1, 1, d), lambda bi, i: (bi, 0, 0)),
                  pl.BlockSpec((1, d), lambda bi, i: (0, 0))],
        out_specs=pl.BlockSpec((1, tm, d), row),
        out_shape=jax.ShapeDtypeStruct((b, n, d), F32),
        compiler_params=_cparams(("parallel", "parallel")),
        name="final",
    )(x_new, moe, gt2, final_g)


def _rope_tables(n):
    t = jnp.arange(n, dtype=jnp.int32)
    rows = (t // GRID_W).astype(F32)
    cols = (t % GRID_W).astype(F32)
    inv_freq = jnp.power(ROPE_THETA, -jnp.arange(0, ROPE_AXIS_DIM, 2, dtype=F32) / ROPE_AXIS_DIM)
    ang_row = rows[:, None] * inv_freq[None, :]
    ang_col = cols[:, None] * inv_freq[None, :]
    half = ROPE_AXIS_DIM // 2

    def axis_tables(ang):
        c, s = jnp.cos(ang), jnp.sin(ang)
        return jnp.concatenate([c, c], axis=1), jnp.concatenate([-s, s], axis=1)

    cr, sr = axis_tables(ang_row)
    cc, sc = axis_tables(ang_col)
    cos64 = jnp.concatenate([cr, cc], axis=1)
    sin64 = jnp.concatenate([sr, sc], axis=1)
    del half
    return jnp.concatenate([cos64, cos64], axis=1), jnp.concatenate([sin64, sin64], axis=1)


def _pad_lanes(v):
    return jnp.pad(v.reshape(1, -1), ((0, 0), (0, LANES - v.size)))


def kernel(x, c, ctx, c_ctx, w_mod, b_mod, norm1_g, w_in, conv_w, a_log, dt_bias, gdn_norm_g,
           lam_q1, lam_k1, lam_q2, lam_k2, da_subln_g, w_out, norm2_g,
           w_router, w_gate, w_up, w_down, final_g):
    b, n, d = x.shape
    nc = ctx.shape[1]
    layer = 0

    cvec = jnp.concatenate([c, c_ctx[None, :], jnp.zeros((8 - b - 1, d), F32)], axis=0)
    mod = _mod_call(cvec, w_mod[layer], b_mod[layer])
    sh1, sc1, gt1, sh2, sc2, gt2 = [mod[:b, i * d:(i + 1) * d].reshape(b, 1, d) for i in range(6)]
    sh1c = jnp.broadcast_to(mod[b:b + 1, 0:d].reshape(1, 1, d), (b, 1, d))
    sc1c = jnp.broadcast_to(mod[b:b + 1, d:2 * d].reshape(1, 1, d), (b, 1, d))

    w_in_bf = jnp.pad(w_in[layer].astype(BF16), ((0, 0), (0, IN_COLS_PAD - w_in.shape[2])))
    g1 = norm1_g[layer].reshape(1, d)
    cos_l, sin_l = _rope_tables(n)
    cos_c, sin_c = jnp.ones((nc, LANES), F32), jnp.zeros((nc, LANES), F32)
    q, k, vt, qkv, gate, ab = _inproj_call(x, g1, sh1, sc1, cos_l, sin_l, w_in_bf, tm=512)
    _, kc, vct, qkvc, _, abc = _inproj_call(ctx, g1, sh1c, sc1c, cos_c, sin_c, w_in_bf, tm=nc)

    lam_vecs = jnp.stack([lam_q1[layer], lam_k1[layer], lam_q2[layer], lam_k2[layer]], axis=0)
    tk = vt.shape[-1]
    reps = tk // nc
    k_all = jnp.concatenate([kc] * reps + [k], axis=1)
    vct_pad = jnp.pad(vct, ((0, 0), (0, 0), (0, 0), (0, 0), (0, tk - nc)))
    vt_all = jnp.concatenate([vct_pad, vt], axis=2)
    o_da = _attn_call(q, k_all, vt_all, lam_vecs, da_subln_g[layer].reshape(1, DA_DV), tq=1024)

    conv_w8 = jnp.pad(conv_w[layer], ((0, 8 - conv_w.shape[1]), (0, 0)))
    alog_row = _pad_lanes(a_log[layer])
    dtb_row = _pad_lanes(dt_bias[layer])
    ql, kl, vl, gbl, gbtl = _gdnpre_call(qkv, ab, conv_w8, alog_row, dtb_row, tm=512)
    qc, kcg, vcg, gbc, gbtc = _gdnpre_call(qkvc, abc, conv_w8, alog_row, dtb_row, tm=nc)
    zeros_state = jnp.zeros((b, GDN_HEADS, GDN_DK, GDN_DV), F32)
    o_dirs = []
    for direction in range(2):
        _, s_ctx = _gdn_scan_call(qc, kcg, vcg, gbc, gbtc, zeros_state, direction, sc=nc)
        o_dir, _ = _gdn_scan_call(ql, kl, vl, gbl, gbtl, s_ctx, direction, sc=256)
        o_dirs.append(o_dir)

    wr_pad = jnp.pad(w_router[layer], ((0, 0), (0, LANES - N_EXPERTS)))
    x_new, h, aff, afft = _outproj_call(
        o_da, o_dirs[0], o_dirs[1], gate, x, gt1, gdn_norm_g[layer].reshape(1, GDN_DV),
        w_out[layer].astype(BF16), norm2_g[layer].reshape(1, d), sh2, sc2, wr_pad, tm=512)

    cap = CAP_FACTOR * n // N_EXPERTS
    tb = 512
    tj = 256
    post, pos, starts = _select_call(afft, aff, cap, tb)
    nblk = n // tb
    ye = _moe_ffn_call(starts, h, post.reshape(b, N_EXPERTS, nblk, tb), afft.reshape(b, N_EXPERTS, nblk, tb),
                       w_gate[layer].astype(BF16), w_up[layer].astype(BF16), w_down[layer].astype(BF16),
                       cap, tj, tb)
    moe = _combine_call(starts, pos, ye, tj, tb, dh=512)
    return _final_call(x_new, moe, gt2, final_g.reshape(1, d), tm=512)
```

```python
import functools
import math

import jax
import jax.numpy as jnp
import numpy as np
from jax import lax
from jax.experimental import pallas as pl
from jax.experimental.pallas import tpu as pltpu

F32 = jnp.float32
BF16 = jnp.bfloat16

D_MODEL = 1024
GRID_W = 64
EPS = 1e-6
DA_HEADS = 4
DA_D = 64
DA_DV = 2 * DA_D
ROPE_AXIS_DIM = DA_D // 2
ROPE_THETA = 10000.0
GDN_HEADS = 4
GDN_DK = 128
GDN_DV = 128
CHUNK = 64
N_EXPERTS = 16
CAP_FACTOR = 2
LAM_INIT = 0.8 - 0.6 * math.exp(-0.3 * 0)

DA_QK = DA_HEADS * 2 * DA_D
DA_V = DA_HEADS * DA_DV
GDN_QK = GDN_HEADS * GDN_DK
GDN_V = GDN_HEADS * GDN_DV
GDN_QKV = 2 * GDN_QK + GDN_V
GDN_AB = 2 * 2 * GDN_HEADS
COL_Q, COL_K, COL_V = 0, DA_QK, 2 * DA_QK
COL_QKV = 2 * DA_QK + DA_V
COL_GATE = COL_QKV + GDN_QKV
COL_AB = COL_GATE + GDN_V
LANES = 128
IN_COLS_PAD = COL_AB + LANES
MXU_WIDTH = 256
ATTN_COL_TILE = MXU_WIDTH
VT_ROWS = DA_DV + 8

VMEM_LIMIT = 56 * 1024 * 1024
NEG_BIG = -1e30
LOG2_E = math.log2(math.e)


def _cparams(sem):
    return pltpu.CompilerParams(dimension_semantics=sem, vmem_limit_bytes=VMEM_LIMIT)


def _sigmoid(x):
    return 1.0 / (1.0 + jnp.exp(-x))


def _silu(x):
    return x * _sigmoid(x)


def _split3(a):
    a1 = a.astype(BF16)
    r1 = a - a1.astype(F32)
    a2 = r1.astype(BF16)
    a3 = (r1 - a2.astype(F32)).astype(BF16)
    return a1, a2, a3


def _dot(a, b):
    return jnp.dot(a, b, preferred_element_type=F32)


def _dot_x3(a, b):
    a1, a2, _ = _split3(a)
    b1, b2, _ = _split3(b)
    return _dot(a1, b1) + (_dot(a1, b2) + _dot(a2, b1))


def _dot_nt(a, b):
    return lax.dot_general(a, b, (((1,), (1,)), ((), ())), preferred_element_type=F32)


def _dot_tn(a, b):
    return lax.dot_general(a, b, (((0,), (0,)), ((), ())), preferred_element_type=F32)


def _mod_kernel(c_ref, w_ref, b_ref, o_ref):
    s = _silu(c_ref[...])
    o_ref[...] = jnp.dot(s, w_ref[...], precision=lax.Precision.HIGHEST,
                         preferred_element_type=F32) + b_ref[...]


def _mod_call(cvec, w_mod, b_mod):
    d, n = w_mod.shape
    tn = 1024
    return pl.pallas_call(
        _mod_kernel,
        grid=(n // tn,),
        in_specs=[pl.BlockSpec((8, d), lambda j: (0, 0)),
                  pl.BlockSpec((d, tn), lambda j: (0, j)),
                  pl.BlockSpec((1, tn), lambda j: (0, j))],
        out_specs=pl.BlockSpec((8, tn), lambda j: (0, j)),
        out_shape=jax.ShapeDtypeStruct((8, n), F32),
        compiler_params=_cparams(("arbitrary",)),
        name="mod",
    )(cvec, w_mod, b_mod.reshape(1, n))


def _inproj_kernel(x_ref, g_ref, sh_ref, sc_ref, cos_ref, sin_ref, w_ref,
                   q_ref, k_ref, vt_ref, qkv_ref, gate_ref, ab_ref):
    xb = x_ref[0]
    tm = xb.shape[0]
    ms = jnp.mean(xb * xb, axis=-1, keepdims=True)
    hm = (xb * lax.rsqrt(ms + EPS) * g_ref[...]) * (1.0 + sc_ref[0]) + sh_ref[0]
    hb = hm.astype(BF16)

    def proj(lo, hi):
        return _dot(hb, w_ref[:, lo:hi])

    cos = cos_ref[...]
    sin = sin_ref[...]
    lane = lax.broadcasted_iota(jnp.int32, (tm, LANES), 1)
    first_half = (lane % ROPE_AXIS_DIM) < (ROPE_AXIS_DIM // 2)

    def rope(t):
        outs = []
        for j in range(t.shape[1] // LANES):
            s = t[:, LANES * j:LANES * (j + 1)]
            partner = jnp.where(first_half,
                                pltpu.roll(s, LANES - ROPE_AXIS_DIM // 2, 1),
                                pltpu.roll(s, ROPE_AXIS_DIM // 2, 1))
            outs.append(s * cos + partner * sin)
        return jnp.concatenate(outs, axis=1)

    q_ref[0] = (rope(proj(COL_Q, COL_K)) * (DA_D ** -0.5 * LOG2_E)).astype(BF16)
    k_ref[0] = rope(proj(COL_K, COL_V)).astype(BF16)
    vt = proj(COL_V, COL_QKV).T.astype(BF16)
    vt_ref[0, :, 0, 0:DA_DV, :] = vt.reshape(DA_HEADS, DA_DV, tm)
    extra = lax.broadcasted_iota(jnp.int32, (DA_HEADS, VT_ROWS - DA_DV, tm), 1)
    vt_ref[0, :, 0, DA_DV:VT_ROWS, :] = jnp.where(extra == 0, 1.0, 0.0).astype(BF16)
    qkv_ref[0] = proj(COL_QKV, COL_GATE)
    gate_ref[0] = proj(COL_GATE, COL_AB)
    ab_ref[0] = proj(COL_AB, IN_COLS_PAD)


def _inproj_call(x, norm_g, shift, scale, cos, sin, w_bf, tm):
    b, n, d = x.shape
    row = lambda bi, i: (bi, i, 0)
    vec = lambda bi, i: (bi, 0, 0)
    outs = [(DA_QK, BF16), (DA_QK, BF16), None, (GDN_QKV, F32), (GDN_V, F32), (LANES, F32)]
    out_specs = [pl.BlockSpec((1, tm, o[0]), row) if o else
                 pl.BlockSpec((1, DA_HEADS, 1, VT_ROWS, tm), lambda bi, i: (bi, 0, i, 0, 0)) for o in outs]
    out_shape = [jax.ShapeDtypeStruct((b, n, o[0]), o[1]) if o else
                 jax.ShapeDtypeStruct((b, DA_HEADS, n // tm, VT_ROWS, tm), BF16) for o in outs]
    return pl.pallas_call(
        _inproj_kernel,
        grid=(b, n // tm),
        in_specs=[pl.BlockSpec((1, tm, d), row),
                  pl.BlockSpec((1, d), lambda bi, i: (0, 0)),
                  pl.BlockSpec((1, 1, d), vec),
                  pl.BlockSpec((1, 1, d), vec),
                  pl.BlockSpec((tm, LANES), lambda bi, i: (i, 0)),
                  pl.BlockSpec((tm, LANES), lambda bi, i: (i, 0)),
                  pl.BlockSpec((d, IN_COLS_PAD), lambda bi, i: (0, 0))],
        out_specs=out_specs,
        out_shape=out_shape,
        compiler_params=_cparams(("parallel", "parallel")),
        name="inproj",
    )(x, norm_g, shift, scale, cos, sin, w_bf)


def _attn_kernel(q_ref, k_ref, vt_ref, lam_ref, g_ref, o_ref,
                 q2_ref, acc_ref, m_ref, s_ref, p_ref, a_ref, *, tk):
    q = q_ref[0]
    tq = q.shape[0]
    lane = lax.broadcasted_iota(jnp.int32, (tq, LANES), 1)
    zero = jnp.zeros_like(q)
    q2_ref[0:tq, :] = jnp.where(lane < DA_D, q, zero)
    q2_ref[tq:2 * tq, :] = jnp.where(lane >= DA_D, q, zero)
    m_ref[...] = jnp.full(m_ref.shape, NEG_BIG, F32)
    acc_ref[...] = jnp.zeros(acc_ref.shape, F32)

    nct = 2 * tq // ATTN_COL_TILE
    col = lambda ct: slice(ct * ATTN_COL_TILE, (ct + 1) * ATTN_COL_TILE)

    def scores(kk, ct):
        return _dot_nt(kk, q2_ref[col(ct), :])

    def softmax_update(ct, s):
        cols = col(ct)
        m_old = m_ref[:, cols]
        m_new = jnp.maximum(m_old, jnp.max(s, axis=0, keepdims=True))
        alpha = jnp.exp2(m_old - m_new)
        p = jnp.exp2(s - m_new)
        m_ref[:, cols] = m_new
        return p.astype(BF16), alpha

    def value_update(ct, vt, p, alpha):
        cols = col(ct)
        acc_ref[:, cols] = alpha * acc_ref[:, cols] + _dot(vt, p)

    nchunk = k_ref.shape[1] // tk
    last = nct - 1
    s_ref[...] = scores(k_ref[0, 0:tk, :], 0)
    p_ref[...] = jnp.zeros(p_ref.shape, BF16)
    a_ref[...] = jnp.ones(a_ref.shape, F32)

    def body(j, carry):
        kk = k_ref[0, pl.ds(pl.multiple_of(j * tk, tk), tk), :]
        vt = vt_ref[0, 0, j]
        s_cur = s_ref[...]
        pending = (last, vt_ref[0, 0, jnp.maximum(j - 1, 0)], p_ref[...], a_ref[...])
        for ct in range(nct):
            if ct < last:
                s_ahead = scores(kk, ct + 1)
            else:
                jn = jnp.minimum(j + 1, nchunk - 1)
                s_ahead = scores(k_ref[0, pl.ds(pl.multiple_of(jn * tk, tk), tk), :], 0)
            p, alpha = softmax_update(ct, s_cur)
            value_update(*pending)
            pending = (ct, vt, p, alpha)
            s_cur = s_ahead
        s_ref[...] = s_cur
        p_ref[...] = pending[2]
        a_ref[...] = pending[3]
        return carry

    lax.fori_loop(0, nchunk, body, 0)
    value_update(last, vt_ref[0, 0, nchunk - 1], p_ref[...], a_ref[...])

    lv = lam_ref[...]
    lam = (jnp.exp(jnp.sum(lv[0:1] * lv[1:2], axis=1, keepdims=True))
           - jnp.exp(jnp.sum(lv[2:3] * lv[3:4], axis=1, keepdims=True)) + LAM_INIT)
    den = acc_ref[DA_DV:DA_DV + 1, :]
    ot = (acc_ref[0:DA_DV, 0:tq] / den[:, 0:tq]
          - lam * (acc_ref[0:DA_DV, tq:2 * tq] / den[:, tq:2 * tq]))
    o = ot.T
    ms = jnp.mean(o * o, axis=-1, keepdims=True)
    o_ref[0] = ((o * lax.rsqrt(ms + EPS) * g_ref[...]) * (1.0 - LAM_INIT)).astype(o_ref.dtype)


def _attn_call(q, k_all, vt_all, lam_vecs, subln_g, tq):
    b, n, _ = q.shape
    nk = k_all.shape[1]
    _, _, nchunk, _, tk = vt_all.shape
    return pl.pallas_call(
        functools.partial(_attn_kernel, tk=tk),
        grid=(b, DA_HEADS, n // tq),
        in_specs=[pl.BlockSpec((1, tq, LANES), lambda bi, h, i: (bi, i, h)),
                  pl.BlockSpec((1, nk, LANES), lambda bi, h, i: (bi, 0, h)),
                  pl.BlockSpec((1, 1, nchunk, VT_ROWS, tk), lambda bi, h, i: (bi, h, 0, 0, 0)),
                  pl.BlockSpec((4, DA_D), lambda bi, h, i: (0, 0)),
                  pl.BlockSpec((1, DA_DV), lambda bi, h, i: (0, 0))],
        out_specs=pl.BlockSpec((1, tq, LANES), lambda bi, h, i: (bi, i, h)),
        out_shape=jax.ShapeDtypeStruct((b, n, DA_V), BF16),
        scratch_shapes=[pltpu.VMEM((2 * tq, LANES), BF16), pltpu.VMEM((VT_ROWS, 2 * tq), F32),
                        pltpu.VMEM((1, 2 * tq), F32),
                        pltpu.VMEM((tk, ATTN_COL_TILE), F32), pltpu.VMEM((tk, ATTN_COL_TILE), BF16),
                        pltpu.VMEM((1, ATTN_COL_TILE), F32)],
        compiler_params=_cparams(("parallel", "parallel", "parallel")),
        name="attn",
    )(q, k_all, vt_all, lam_vecs, subln_g)


def _gdnpre_kernel(x_ref, xp_ref, xn_ref, cw_ref, ab_ref, al_ref, dtb_ref,
                   q_ref, k_ref, v_ref, gb_ref, gbt_ref):
    i = pl.program_id(1)
    nblk = pl.num_programs(1)
    x = x_ref[0]
    tm = x.shape[0]
    prev = jnp.where(i > 0, xp_ref[0][7:8, :], 0.0)
    nxt = jnp.where(i < nblk - 1, xn_ref[0][0:1, :], 0.0)
    row = lax.broadcasted_iota(jnp.int32, (tm, 1), 0)
    xm1 = jnp.where(row == 0, prev, pltpu.roll(x, 1, 0))
    xp1 = jnp.where(row == tm - 1, nxt, pltpu.roll(x, tm - 1, 0))
    w = cw_ref[...]
    s = _silu(xm1 * w[0:1] + x * w[1:2] + xp1 * w[2:3])

    def l2n(t):
        return t * lax.rsqrt(jnp.sum(t * t, axis=-1, keepdims=True) + EPS)

    for h in range(GDN_HEADS):
        lo, hi = h * GDN_DK, (h + 1) * GDN_DK
        q_ref[0, :, lo:hi] = l2n(s[:, lo:hi]) * GDN_DK ** -0.5
        k_ref[0, :, lo:hi] = l2n(s[:, GDN_QK + lo:GDN_QK + hi])
    v_ref[0] = s[:, 2 * GDN_QK:]

    ab = ab_ref[0]
    lane = lax.broadcasted_iota(jnp.int32, ab.shape, 1)
    z = ab + dtb_ref[...]
    softplus = jnp.maximum(z, 0.0) + jnp.log1p(jnp.exp(-jnp.abs(z)))
    gval = -jnp.exp(al_ref[...]) * softplus
    gb = jnp.where(lane < GDN_AB // 2, gval, jnp.where(lane < GDN_AB, _sigmoid(ab), 0.0))
    gb_ref[0] = gb
    gbt_ref[0] = gb.T[:GDN_AB, :]


def _gdnpre_call(qkv, ab, conv_w8, alog_row, dtb_row, tm):
    b, n, c = qkv.shape
    nb8 = n // 8
    step8 = tm // 8
    row = lambda bi, i: (bi, i, 0)
    return pl.pallas_call(
        _gdnpre_kernel,
        grid=(b, n // tm),
        in_specs=[pl.BlockSpec((1, tm, c), row),
                  pl.BlockSpec((1, 8, c), lambda bi, i: (bi, jnp.maximum(i * step8 - 1, 0), 0)),
                  pl.BlockSpec((1, 8, c), lambda bi, i: (bi, jnp.minimum((i + 1) * step8, nb8 - 1), 0)),
                  pl.BlockSpec((8, c), lambda bi, i: (0, 0)),
                  pl.BlockSpec((1, tm, LANES), row),
                  pl.BlockSpec((1, LANES), lambda bi, i: (0, 0)),
                  pl.BlockSpec((1, LANES), lambda bi, i: (0, 0))],
        out_specs=[pl.BlockSpec((1, tm, GDN_QK), row), pl.BlockSpec((1, tm, GDN_QK), row),
                   pl.BlockSpec((1, tm, GDN_V), row), pl.BlockSpec((1, tm, LANES), row),
                   pl.BlockSpec((1, GDN_AB, tm), lambda bi, i: (bi, 0, i))],
        out_shape=[jax.ShapeDtypeStruct((b, n, GDN_QK), F32), jax.ShapeDtypeStruct((b, n, GDN_QK), F32),
                   jax.ShapeDtypeStruct((b, n, GDN_V), F32), jax.ShapeDtypeStruct((b, n, LANES), F32),
                   jax.ShapeDtypeStruct((b, GDN_AB, n), F32)],
        compiler_params=_cparams(("parallel", "parallel")),
        name="gdn_pre",
    )(qkv, qkv, qkv, conv_w8, ab, alog_row, dtb_row)


def _solve_unit_triangular(a_list, rhs_list, bd_mask):
    def mm(xs, ys):
        return [_dot(x.astype(BF16), y.astype(BF16)) for x, y in zip(xs, ys)]

    d = [jnp.where(bd_mask, a, 0.0) for a in a_list]
    n = [a - di for a, di in zip(a_list, d)]
    d2 = mm(d, d)
    dd2 = mm(d, d2)
    p = [(-di + d2i) - t for di, d2i, t in zip(d, d2, dd2)]
    d4 = mm(d2, d2)
    pd4 = mm(p, d4)
    p = [pi + d4i + t for pi, d4i, t in zip(p, d4, pd4)]
    d8 = mm(d4, d4)
    pd8 = mm(p, d8)
    p = [pi + d8i + t for pi, d8i, t in zip(p, d8, pd8)]
    m = [ni + t for ni, t in zip(n, mm(p, n))]
    z = [ri + t for ri, t in zip(rhs_list, mm(p, rhs_list))]
    m2 = mm(m, m)
    z = [zi + t for zi, t in zip(z, mm(m2, z))]
    return [zi - t for zi, t in zip(z, mm(m, z))]


def _gdn_scan_kernel(qf_ref, kf_ref, vf_ref, gbf_ref, gbtf_ref, qb_ref, kb_ref, vb_ref, gbb_ref, gbtb_ref,
                     s0_ref, of_ref, ob_ref, sfin_ref, s_ref):
    i = pl.program_id(1)

    @pl.when(i == 0)
    def _():
        s_ref[...] = s0_ref[0]

    c64 = CHUNK
    nchunk = qf_ref.shape[1] // c64
    ri = lax.broadcasted_iota(jnp.int32, (c64, c64), 0)
    ci = lax.broadcasted_iota(jnp.int32, (c64, c64), 1)
    bd_mask = (ri // 16) == (ci // 16)
    lower_incl, upper_incl = ri >= ci, ri <= ci
    heads = range(GDN_HEADS)
    rows = [slice(c * c64, (c + 1) * c64) for c in range(nchunk)]
    cols = [slice(h * GDN_DK, (h + 1) * GDN_DK) for h in heads]

    dirs = []
    for d, refs in enumerate(((qf_ref, kf_ref, vf_ref, gbf_ref, gbtf_ref, of_ref),
                              (qb_ref, kb_ref, vb_ref, gbb_ref, gbtb_ref, ob_ref))):
        rev = d == 1
        incl = upper_incl if rev else lower_incl
        dirs.append(dict(
            refs=refs, incl=incl, strict=(ri < ci) if rev else (ri > ci),
            tri_c=jnp.where(incl, 1.0, 0.0).astype(BF16),
            tri_r=jnp.where(lower_incl if rev else upper_incl, 1.0, 0.0).astype(BF16),
            last=0 if rev else c64 - 1,
            order=list(range(nchunk - 1, -1, -1) if rev else range(nchunk))))

    gall, gc_col, gc_row = {}, {}, {}
    for d, dr in enumerate(dirs):
        gb_ref, gbt_ref = dr["refs"][3], dr["refs"][4]
        for c in range(nchunk):
            gall[d, c] = gb_ref[0, rows[c], :]
            g1, g2, g3 = _split3(gall[d, c])
            tc, tr = dr["tri_c"], dr["tri_r"]
            gc_col[d, c] = _dot(tc, g1) + _dot(tc, g2) + _dot(tc, g3)
            t1, t2, t3 = _split3(gbt_ref[0, :, rows[c]])
            gc_row[d, c] = _dot(t1, tr) + _dot(t2, tr) + _dot(t3, tr)

    probs = [(d, c, h) for d in range(2) for c in range(nchunk) for h in heads]
    ln = lambda d, h: d * GDN_HEADS + h
    k = {(d, c, h): dirs[d]["refs"][1][0, rows[c], cols[h]] for d, c, h in probs}
    q = {(d, c, h): dirs[d]["refs"][0][0, rows[c], cols[h]] for d, c, h in probs}
    kbf = {p: k[p].astype(BF16) for p in probs}
    kk = {p: _dot_nt(kbf[p], kbf[p]) for p in probs}
    qk = {p: _dot_nt(q[p].astype(BF16), kbf[p]) for p in probs}
    a_list, rhs_list = [], []
    kd, qd, aqk, glast = {}, {}, {}, {}
    for p in probs:
        d, c, h = p
        dr = dirs[d]
        lane = ln(d, h)
        gcol = gc_col[d, c][:, lane:lane + 1]
        bcol = gall[d, c][:, GDN_AB // 2 + lane:GDN_AB // 2 + lane + 1]
        diff = gcol - gc_row[d, c][lane:lane + 1, :]
        e_strict = jnp.where(dr["strict"], jnp.exp(jnp.where(dr["strict"], diff, 0.0)), 0.0)
        e_incl = jnp.where(dr["incl"], jnp.exp(jnp.where(dr["incl"], diff, 0.0)), 0.0)
        eg = jnp.exp(gcol)
        glast[p] = gcol[dr["last"]:dr["last"] + 1, :]
        a_list.append(bcol * kk[p] * e_strict)
        v = dr["refs"][2][0, rows[c], cols[h]]
        rhs_list.append(jnp.concatenate([(bcol * eg) * k[p], bcol * v], axis=1))
        kd[p] = (k[p] * jnp.exp(glast[p] - gcol)).astype(BF16)
        qd[p] = (q[p] * eg).astype(BF16)
        aqk[p] = (qk[p] * e_incl).astype(BF16)
    sol = dict(zip(probs, _solve_unit_triangular(a_list, rhs_list, bd_mask)))

    chains = [(d, h) for d in range(2) for h in heads]
    for t in range(nchunk):
        cur = {(d, h): (d, dirs[d]["order"][t], h) for d, h in chains}
        s = {dh: s_ref[dh[0], dh[1]] for dh in chains}
        sb = {dh: s[dh].astype(BF16) for dh in chains}
        ws = {dh: _dot(sol[cur[dh]][:, :GDN_DK].astype(BF16), sb[dh]) for dh in chains}
        qs = {dh: _dot(qd[cur[dh]], sb[dh]) for dh in chains}
        u = {dh: (sol[cur[dh]][:, GDN_DK:] - ws[dh]).astype(BF16) for dh in chains}
        au = {dh: _dot(aqk[cur[dh]], u[dh]) for dh in chains}
        ku = {dh: _dot_tn(kd[cur[dh]], u[dh]) for dh in chains}
        for dh in chains:
            d, c, h = cur[dh]
            s_ref[d, h] = jnp.exp(glast[cur[dh]]) * s[dh] + ku[dh]
            dirs[d]["refs"][5][0, rows[c], cols[h]] = qs[dh] + au[dh]

    @pl.when(i == pl.num_programs(1) - 1)
    def _():
        sfin_ref[0] = s_ref[...]


def _gdn_scan_call(q, k, v, gb, gbt, s0, sc):
    b, n, _ = q.shape
    nsup = n // sc
    fwd = lambda bi, i: (bi, i, 0)
    bwd = lambda bi, i: (bi, nsup - 1 - i, 0)
    st = lambda bi, i: (bi, 0, 0, 0, 0)
    state = (1, 2, GDN_HEADS, GDN_DK, GDN_DV)

    def in_specs(row, tr):
        return [pl.BlockSpec((1, sc, GDN_QK), row), pl.BlockSpec((1, sc, GDN_QK), row),
                pl.BlockSpec((1, sc, GDN_V), row), pl.BlockSpec((1, sc, LANES), row),
                pl.BlockSpec((1, GDN_AB, sc), tr)]

    return pl.pallas_call(
        _gdn_scan_kernel,
        grid=(b, nsup),
        in_specs=(in_specs(fwd, lambda bi, i: (bi, 0, i))
                  + in_specs(bwd, lambda bi, i: (bi, 0, nsup - 1 - i))
                  + [pl.BlockSpec(state, st)]),
        out_specs=[pl.BlockSpec((1, sc, GDN_V), fwd), pl.BlockSpec((1, sc, GDN_V), bwd),
                   pl.BlockSpec(state, st)],
        out_shape=[jax.ShapeDtypeStruct((b, n, GDN_V), F32), jax.ShapeDtypeStruct((b, n, GDN_V), F32),
                   jax.ShapeDtypeStruct((b,) + state[1:], F32)],
        scratch_shapes=[pltpu.VMEM(state[1:], F32)],
        compiler_params=_cparams(("parallel", "arbitrary")),
        name="gdn_scan",
    )(q, k, v, gb, gbt, q, k, v, gb, gbt, s0)


def _outproj_kernel(oda_ref, of_ref, ob_ref, gate_ref, x_ref, gt1_ref, gng_ref, wout_ref,
                    n2g_ref, sh2_ref, sc2_ref, wr_ref, xnew_ref, h_ref, aff_ref, afft_ref):
    og = of_ref[0] + ob_ref[0]
    gate = gate_ref[0]
    ys = []
    for h in range(GDN_HEADS):
        cols = slice(h * GDN_DV, (h + 1) * GDN_DV)
        t = og[:, cols]
        y = t * lax.rsqrt(jnp.mean(t * t, axis=-1, keepdims=True) + EPS) * gng_ref[...]
        ys.append((y * _silu(gate[:, cols])).astype(BF16))
    y_gdn = jnp.concatenate(ys, axis=1)
    proj = _dot(oda_ref[0], wout_ref[0:DA_V, :]) + _dot(y_gdn, wout_ref[DA_V:, :])
    xn = x_ref[0] + gt1_ref[0] * proj
    xnew_ref[0] = xn
    ms = jnp.mean(xn * xn, axis=-1, keepdims=True)
    hm = (xn * lax.rsqrt(ms + EPS) * n2g_ref[...]) * (1.0 + sc2_ref[0]) + sh2_ref[0]
    h_ref[0] = hm.astype(BF16)
    logits = _dot_x3(hm, wr_ref[...])
    lane = lax.broadcasted_iota(jnp.int32, logits.shape, 1)
    logits = jnp.where(lane < N_EXPERTS, logits, NEG_BIG)
    e = jnp.exp(logits - jnp.max(logits, axis=-1, keepdims=True))
    aff = e / jnp.sum(e, axis=-1, keepdims=True)
    aff_ref[0] = aff
    afft_ref[0] = aff.T[:N_EXPERTS, :]


def _outproj_call(o_da, o_f, o_b, gate, x, gt1, gdn_norm_g, w_out_bf, norm2_g, sh2, sc2, wr_pad, tm):
    b, n, d = x.shape
    row = lambda bi, i: (bi, i, 0)
    vec = lambda bi, i: (bi, 0, 0)
    const2 = lambda bi, i: (0, 0)
    return pl.pallas_call(
        _outproj_kernel,
        grid=(b, n // tm),
        in_specs=[pl.BlockSpec((1, tm, DA_V), row), pl.BlockSpec((1, tm, GDN_V), row),
                  pl.BlockSpec((1, tm, GDN_V), row), pl.BlockSpec((1, tm, GDN_V), row),
                  pl.BlockSpec((1, tm, d), row), pl.BlockSpec((1, 1, d), vec),
                  pl.BlockSpec((1, GDN_DV), const2), pl.BlockSpec((DA_V + GDN_V, d), const2),
                  pl.BlockSpec((1, d), const2), pl.BlockSpec((1, 1, d), vec), pl.BlockSpec((1, 1, d), vec),
                  pl.BlockSpec((d, LANES), const2)],
        out_specs=[pl.BlockSpec((1, tm, d), row), pl.BlockSpec((1, tm, d), row),
                   pl.BlockSpec((1, tm, LANES), row),
                   pl.BlockSpec((1, N_EXPERTS, tm), lambda bi, i: (bi, 0, i))],
        out_shape=[jax.ShapeDtypeStruct((b, n, d), F32), jax.ShapeDtypeStruct((b, n, d), BF16),
                   jax.ShapeDtypeStruct((b, n, LANES), F32),
                   jax.ShapeDtypeStruct((b, N_EXPERTS, n), F32)],
        compiler_params=_cparams(("parallel", "parallel")),
        name="outproj",
    )(o_da, o_f, o_b, gate, x, gt1, gdn_norm_g, w_out_bf, norm2_g, sh2, sc2, wr_pad)


PREFIX_BLK = 256
BISECT_STEPS = 64


def _select_kernel(afft_ref, aff_ref, post_ref, pos_ref, st_ref, *, cap, tb):
    at = afft_ref[0]
    n = at.shape[1]

    def count_ge(t):
        return jnp.sum(jnp.where(at >= t, 1.0, 0.0), axis=1, keepdims=True)

    def bisect(_, bounds):
        lo, hi = bounds
        mid = 0.5 * (lo + hi)
        ok = count_ge(mid) >= cap
        return jnp.where(ok, mid, lo), jnp.where(ok, hi, mid)

    _, hi = lax.fori_loop(0, BISECT_STEPS, bisect,
                          (jnp.zeros((N_EXPERTS, 1), F32), jnp.full((N_EXPERTS, 1), 2.0, F32)))

    def below(h):
        return jnp.max(jnp.where(at < h, at, -1.0), axis=1, keepdims=True)

    def short(state):
        return jnp.sum(jnp.where(count_ge(state[0]) < cap, 1.0, 0.0)) > 0.0

    def step_down(state):
        t, h = state
        lacking = count_ge(t) < cap
        h = jnp.where(lacking, t, h)
        return jnp.where(lacking, below(h), t), h

    thr, _ = lax.while_loop(short, step_down, (below(hi), hi))
    need = cap - jnp.sum(jnp.where(at > thr, 1.0, 0.0), axis=1, keepdims=True)

    pi = lax.broadcasted_iota(jnp.int32, (PREFIX_BLK, PREFIX_BLK), 0)
    pj = lax.broadcasted_iota(jnp.int32, (PREFIX_BLK, PREFIX_BLK), 1)
    upper = jnp.where(pi <= pj, 1.0, 0.0).astype(BF16)
    lower = jnp.where(pi >= pj, 1.0, 0.0).astype(BF16)

    def prefix_lanes(m):
        carry = jnp.zeros((m.shape[0], 1), F32)
        outs = []
        for c in range(m.shape[1] // PREFIX_BLK):
            mc = m[:, c * PREFIX_BLK:(c + 1) * PREFIX_BLK]
            loc = _dot(mc.astype(BF16), upper)
            outs.append(loc - mc + carry)
            carry = carry + loc[:, PREFIX_BLK - 1:PREFIX_BLK]
        return jnp.concatenate(outs, axis=1)

    def prefix_rows(m):
        carry = jnp.zeros((1, m.shape[1]), F32)
        outs = []
        for c in range(m.shape[0] // PREFIX_BLK):
            mc = m[c * PREFIX_BLK:(c + 1) * PREFIX_BLK, :]
            loc = _dot(lower, mc.astype(BF16))
            outs.append(loc - mc + carry)
            carry = carry + loc[PREFIX_BLK - 1:PREFIX_BLK, :]
        return jnp.concatenate(outs, axis=0)

    eq_t = jnp.where(at == thr, 1.0, 0.0)
    sel_t = jnp.where(at > thr, 1.0, jnp.where(prefix_lanes(eq_t) < need, eq_t, 0.0))
    post_ref[0] = jnp.where(sel_t > 0.0, prefix_lanes(sel_t), -1.0)

    ti = lax.broadcasted_iota(jnp.int32, (n, LANES), 0)
    bi = lax.broadcasted_iota(jnp.int32, (n, LANES), 1)
    before = jnp.where(ti < bi * tb, 1.0, 0.0).astype(BF16)
    st_ref[0] = _dot(sel_t.astype(BF16), before).astype(jnp.int32)

    a = aff_ref[0]
    er = lax.broadcasted_iota(jnp.int32, (N_EXPERTS, LANES), 0)
    ec = lax.broadcasted_iota(jnp.int32, (N_EXPERTS, LANES), 1)
    diag = er == ec
    thr_row = jnp.sum(jnp.where(diag, thr, 0.0), axis=0, keepdims=True)
    need_row = jnp.sum(jnp.where(diag, need, 0.0), axis=0, keepdims=True)
    valid = lax.broadcasted_iota(jnp.int32, a.shape, 1) < N_EXPERTS
    eq = jnp.where(valid & (a == thr_row), 1.0, 0.0)
    sel = jnp.where(valid & (a > thr_row), 1.0, jnp.where(prefix_rows(eq) < need_row, eq, 0.0))
    pos_ref[0] = jnp.where(sel > 0.0, prefix_rows(sel), -1.0)


def _select_call(afft, aff, cap, tb):
    b, e, n = afft.shape
    return pl.pallas_call(
        functools.partial(_select_kernel, cap=cap, tb=tb),
        grid=(b,),
        in_specs=[pl.BlockSpec((1, e, n), lambda bi: (bi, 0, 0)),
                  pl.BlockSpec((1, n, LANES), lambda bi: (bi, 0, 0))],
        out_specs=[pl.BlockSpec((1, e, n), lambda bi: (bi, 0, 0)),
                   pl.BlockSpec((1, n, LANES), lambda bi: (bi, 0, 0)),
                   pl.BlockSpec((1, e, LANES), lambda bi: (bi, 0, 0))],
        out_shape=[jax.ShapeDtypeStruct((b, e, n), F32), jax.ShapeDtypeStruct((b, n, LANES), F32),
                   jax.ShapeDtypeStruct((b, e, LANES), jnp.int32)],
        compiler_params=_cparams(("parallel",)),
        name="select",
    )(afft, aff)


def _moe_ffn_kernel(st_ref, h_ref, post_ref, afft_ref, wg_ref, wu_ref, wd_ref, ye_ref,
                    acc_ref, gacc_ref, wgb_ref, wub_ref, wdb_ref, *, tj, tb):
    b = pl.program_id(0)
    e = pl.program_id(1)
    cap = ye_ref.shape[2]
    nblk = h_ref.shape[1] // tb
    slot = lax.broadcasted_iota(jnp.int32, (tj, tb), 0).astype(F32)
    wgb_ref[...] = wg_ref[0].astype(BF16)
    wub_ref[...] = wu_ref[0].astype(BF16)
    wdb_ref[...] = wd_ref[0].astype(BF16)

    for j in range(cap // tj):
        lo = j * tj
        acc_ref[...] = jnp.zeros(acc_ref.shape, F32)
        gacc_ref[...] = jnp.zeros(gacc_ref.shape, F32)

        def blk_body(bi, carry):
            s0 = st_ref[b, e, bi]
            s1 = st_ref[b, e, bi + 1]

            @pl.when((s1 > lo) & (s0 < lo + tj))
            def _():
                prow = post_ref[0, 0, pl.ds(bi, 1), :]
                hit = prow == (slot + float(lo))
                onehot = jnp.where(hit, 1.0, 0.0).astype(BF16)
                off = pl.multiple_of(bi * tb, tb)
                acc_ref[...] += _dot(onehot, h_ref[0, pl.ds(off, tb), :])
                arow = afft_ref[0, 0, pl.ds(bi, 1), :]
                gacc_ref[...] += jnp.sum(jnp.where(hit, arow, 0.0), axis=1, keepdims=True)

            return carry

        lax.fori_loop(0, nblk, blk_body, 0)
        xe = acc_ref[...].astype(BF16)
        hid = _silu(_dot(xe, wgb_ref[...])) * _dot(xe, wub_ref[...])
        ye = _dot(hid.astype(BF16), wdb_ref[...]) * gacc_ref[...]
        ye_ref[0, 0, lo:lo + tj, :] = ye.astype(ye_ref.dtype)


def _moe_ffn_call(starts, h, post4, afft4, wg, wu, wd, cap, tj, tb):
    b, n, d = h.shape
    e, _, f = wg.shape
    nblk = n // tb
    wspec = lambda shp: pl.BlockSpec((1,) + shp, lambda bi, ei: (ei, 0, 0))
    return pl.pallas_call(
        functools.partial(_moe_ffn_kernel, tj=tj, tb=tb),
        grid=(b, e),
        in_specs=[pl.BlockSpec(memory_space=pltpu.SMEM),
                  pl.BlockSpec((1, n, d), lambda bi, ei: (bi, 0, 0), pipeline_mode=pl.Buffered(1)),
                  pl.BlockSpec((1, 1, nblk, tb), lambda bi, ei: (bi, ei, 0, 0)),
                  pl.BlockSpec((1, 1, nblk, tb), lambda bi, ei: (bi, ei, 0, 0)),
                  wspec((d, f)), wspec((d, f)), wspec((f, d))],
        out_specs=pl.BlockSpec((1, 1, cap, d), lambda bi, ei: (bi, ei, 0, 0)),
        out_shape=jax.ShapeDtypeStruct((b, e, cap, d), BF16),
        scratch_shapes=[pltpu.VMEM((tj, d), F32), pltpu.VMEM((tj, 1), F32),
                        pltpu.VMEM((d, f), BF16), pltpu.VMEM((d, f), BF16), pltpu.VMEM((f, d), BF16)],
        compiler_params=_cparams(("parallel", "arbitrary")),
        name="moe_ffn",
    )(starts, h, post4, afft4, wg, wu, wd)


def _combine_kernel(st_ref, pos_ref, ye_ref, o_ref, acc_ref, *, tj, tb):
    b = pl.program_id(0)
    blk = pl.program_id(2)
    cap = ye_ref.shape[2]
    acc_ref[...] = jnp.zeros(acc_ref.shape, F32)
    slot = lax.broadcasted_iota(jnp.int32, (tb, tj), 1).astype(F32)
    pos = pos_ref[0]
    for e in range(N_EXPERTS):
        s0 = st_ref[b, e, blk]
        s1 = st_ref[b, e, blk + 1]
        pcol = pos[:, e:e + 1]
        for j in range(cap // tj):
            lo = j * tj

            @pl.when((s1 > lo) & (s0 < lo + tj))
            def _():
                onehot = jnp.where(pcol == (slot + float(lo)), 1.0, 0.0).astype(BF16)
                acc_ref[...] += _dot(onehot, ye_ref[0, e, lo:lo + tj, :])

    o_ref[0] = acc_ref[...]


def _combine_call(starts, pos, ye, tj, tb, dh):
    b, e, cap, d = ye.shape
    n = pos.shape[1]
    return pl.pallas_call(
        functools.partial(_combine_kernel, tj=tj, tb=tb),
        grid=(b, d // dh, n // tb),
        in_specs=[pl.BlockSpec(memory_space=pltpu.SMEM),
                  pl.BlockSpec((1, tb, LANES), lambda bi, c, i: (bi, i, 0)),
                  pl.BlockSpec((1, e, cap, dh), lambda bi, c, i: (bi, 0, 0, c))],
        out_specs=pl.BlockSpec((1, tb, dh), lambda bi, c, i: (bi, i, c)),
        out_shape=jax.ShapeDtypeStruct((b, n, d), F32),
        scratch_shapes=[pltpu.VMEM((tb, dh), F32)],
        compiler_params=_cparams(("parallel", "parallel", "arbitrary")),
        name="combine",
    )(starts, pos, ye)


def _final_kernel(x_ref, moe_ref, gt2_ref, g_ref, o_ref):
    y = x_ref[0] + gt2_ref[0] * moe_ref[0]
    ms = jnp.mean(y * y, axis=-1, keepdims=True)
    o_ref[0] = y * lax.rsqrt(ms + EPS) * g_ref[...]


def _final_call(x_new, moe, gt2, final_g, tm):
    b, n, d = x_new.shape
    row = lambda bi, i: (bi, i, 0)
    return pl.pallas_call(
        _final_kernel,
        grid=(b, n // tm),
        in_specs=[pl.BlockSpec((1, tm, d), row), pl.BlockSpec((1, tm, d), row),
                  pl.BlockSpec((1, 1, d), lambda bi, i: (bi, 0, 0)),
                  pl.BlockSpec((1, d), lambda bi, i: (0, 0))],
        out_specs=pl.BlockSpec((1, tm, d), row),
        out_shape=jax.ShapeDtypeStruct((b, n, d), F32),
        compiler_params=_cparams(("parallel", "parallel")),
        name="final",
    )(x_new, moe, gt2, final_g)


def _rope_tables(n):
    t = np.arange(n)
    rows = (t // GRID_W).astype(np.float64)
    cols = (t % GRID_W).astype(np.float64)
    inv_freq = np.power(ROPE_THETA, -np.arange(0, ROPE_AXIS_DIM, 2, dtype=np.float64) / ROPE_AXIS_DIM)
    ang_row = rows[:, None] * inv_freq[None, :]
    ang_col = cols[:, None] * inv_freq[None, :]

    def axis_tables(ang):
        c = np.cos(ang).astype(np.float32)
        s = np.sin(ang).astype(np.float32)
        return np.concatenate([c, c], axis=1), np.concatenate([-s, s], axis=1)

    cr, sr = axis_tables(ang_row)
    cc, sc = axis_tables(ang_col)
    cos64 = np.concatenate([cr, cc], axis=1)
    sin64 = np.concatenate([sr, sc], axis=1)
    return (jnp.asarray(np.concatenate([cos64, cos64], axis=1)),
            jnp.asarray(np.concatenate([sin64, sin64], axis=1)))


def _pad_lanes(v):
    return jnp.pad(v.reshape(1, -1), ((0, 0), (0, LANES - v.size)))


def kernel(x, c, ctx, c_ctx, w_mod, b_mod, norm1_g, w_in, conv_w, a_log, dt_bias, gdn_norm_g,
           lam_q1, lam_k1, lam_q2, lam_k2, da_subln_g, w_out, norm2_g,
           w_router, w_gate, w_up, w_down, final_g):
    b, n, d = x.shape
    nc = ctx.shape[1]
    layer = 0

    cvec = jnp.concatenate([c, c_ctx[None, :], jnp.zeros((8 - b - 1, d), F32)], axis=0)
    mod = _mod_call(cvec, w_mod[layer], b_mod[layer])
    sh1, sc1, gt1, sh2, sc2, gt2 = [mod[:b, i * d:(i + 1) * d].reshape(b, 1, d) for i in range(6)]
    sh1c = jnp.broadcast_to(mod[b:b + 1, 0:d].reshape(1, 1, d), (b, 1, d))
    sc1c = jnp.broadcast_to(mod[b:b + 1, d:2 * d].reshape(1, 1, d), (b, 1, d))

    w_in_bf = jnp.pad(w_in[layer].astype(BF16), ((0, 0), (0, IN_COLS_PAD - w_in.shape[2])))
    g1 = norm1_g[layer].reshape(1, d)
    cos_l, sin_l = _rope_tables(n)
    cos_c, sin_c = jnp.ones((nc, LANES), F32), jnp.zeros((nc, LANES), F32)
    q, k, vt, qkv, gate, ab = _inproj_call(x, g1, sh1, sc1, cos_l, sin_l, w_in_bf, tm=512)
    _, kc, vct, qkvc, _, abc = _inproj_call(ctx, g1, sh1c, sc1c, cos_c, sin_c, w_in_bf, tm=nc)

    lam_vecs = jnp.stack([lam_q1[layer], lam_k1[layer], lam_q2[layer], lam_k2[layer]], axis=0)
    tk = vt.shape[-1]
    reps = tk // nc
    k_all = jnp.concatenate([kc] * reps + [k], axis=1)
    vct_pad = jnp.pad(vct, ((0, 0), (0, 0), (0, 0), (0, 0), (0, tk - nc)))
    vt_all = jnp.concatenate([vct_pad, vt], axis=2)
    o_da = _attn_call(q, k_all, vt_all, lam_vecs, da_subln_g[layer].reshape(1, DA_DV), tq=1024)

    conv_w8 = jnp.pad(conv_w[layer], ((0, 8 - conv_w.shape[1]), (0, 0)))
    alog_row = _pad_lanes(a_log[layer])
    dtb_row = _pad_lanes(dt_bias[layer])
    ql, kl, vl, gbl, gbtl = _gdnpre_call(qkv, ab, conv_w8, alog_row, dtb_row, tm=512)
    qc, kcg, vcg, gbc, gbtc = _gdnpre_call(qkvc, abc, conv_w8, alog_row, dtb_row, tm=nc)
    zeros_state = jnp.zeros((b, 2, GDN_HEADS, GDN_DK, GDN_DV), F32)
    _, _, s_ctx = _gdn_scan_call(qc, kcg, vcg, gbc, gbtc, zeros_state, sc=nc)
    o_fwd, o_bwd, _ = _gdn_scan_call(ql, kl, vl, gbl, gbtl, s_ctx, sc=256)
    o_dirs = [o_fwd, o_bwd]

    wr_pad = jnp.pad(w_router[layer], ((0, 0), (0, LANES - N_EXPERTS)))
    x_new, h, aff, afft = _outproj_call(
        o_da, o_dirs[0], o_dirs[1], gate, x, gt1, gdn_norm_g[layer].reshape(1, GDN_DV),
        w_out[layer].astype(BF16), norm2_g[layer].reshape(1, d), sh2, sc2, wr_pad, tm=512)

    cap = CAP_FACTOR * n // N_EXPERTS
    tb = 512
    tj = 256
    post, pos, starts = _select_call(afft, aff, cap, tb)
    nblk = n // tb
    ye = _moe_ffn_call(starts, h, post.reshape(b, N_EXPERTS, nblk, tb), afft.reshape(b, N_EXPERTS, nblk, tb),
                       w_gate[layer], w_up[layer], w_down[layer], cap, tj, tb)
    moe = _combine_call(starts, pos, ye, tj, tb, dh=512)
    return _final_call(x_new, moe, gt2, final_g.reshape(1, d), tm=512)
```

```python
import functools
import math

import jax
import jax.numpy as jnp
import numpy as np
from jax import lax
from jax.experimental import pallas as pl
from jax.experimental.pallas import tpu as pltpu

F32 = jnp.float32
BF16 = jnp.bfloat16

D_MODEL = 1024
GRID_W = 64
EPS = 1e-6
DA_HEADS = 4
DA_D = 64
DA_DV = 2 * DA_D
ROPE_AXIS_DIM = DA_D // 2
ROPE_THETA = 10000.0
GDN_HEADS = 4
GDN_DK = 128
GDN_DV = 128
CHUNK = 64
N_EXPERTS = 16
CAP_FACTOR = 2
LAM_INIT = 0.8 - 0.6 * math.exp(-0.3 * 0)

DA_QK = DA_HEADS * 2 * DA_D
DA_V = DA_HEADS * DA_DV
GDN_QK = GDN_HEADS * GDN_DK
GDN_V = GDN_HEADS * GDN_DV
GDN_QKV = 2 * GDN_QK + GDN_V
GDN_AB = 2 * 2 * GDN_HEADS
COL_Q, COL_K, COL_V = 0, DA_QK, 2 * DA_QK
COL_QKV = 2 * DA_QK + DA_V
COL_GATE = COL_QKV + GDN_QKV
COL_AB = COL_GATE + GDN_V
LANES = 128
IN_COLS_PAD = COL_AB + LANES
MXU_WIDTH = 256
ATTN_COL_TILE = MXU_WIDTH
VT_ROWS = DA_DV + 8

VMEM_LIMIT = 56 * 1024 * 1024
NEG_BIG = -1e30
LOG2_E = math.log2(math.e)


def _cparams(sem):
    return pltpu.CompilerParams(dimension_semantics=sem, vmem_limit_bytes=VMEM_LIMIT)


def _sigmoid(x):
    return 1.0 / (1.0 + jnp.exp(-x))


def _silu(x):
    return x * _sigmoid(x)


def _split3(a):
    a1 = a.astype(BF16)
    r1 = a - a1.astype(F32)
    a2 = r1.astype(BF16)
    a3 = (r1 - a2.astype(F32)).astype(BF16)
    return a1, a2, a3


def _dot(a, b):
    return jnp.dot(a, b, preferred_element_type=F32)


def _dot_x3(a, b):
    a1, a2, _ = _split3(a)
    b1, b2, _ = _split3(b)
    return _dot(a1, b1) + (_dot(a1, b2) + _dot(a2, b1))


def _dot_nt(a, b):
    return lax.dot_general(a, b, (((1,), (1,)), ((), ())), preferred_element_type=F32)


def _dot_tn(a, b):
    return lax.dot_general(a, b, (((0,), (0,)), ((), ())), preferred_element_type=F32)


def _mod_kernel(c_ref, w_ref, b_ref, o_ref):
    s = _silu(c_ref[...])
    o_ref[...] = jnp.dot(s, w_ref[...], precision=lax.Precision.HIGHEST,
                         preferred_element_type=F32) + b_ref[...]


def _mod_call(cvec, w_mod, b_mod):
    d, n = w_mod.shape
    tn = 1024
    return pl.pallas_call(
        _mod_kernel,
        grid=(n // tn,),
        in_specs=[pl.BlockSpec((8, d), lambda j: (0, 0)),
                  pl.BlockSpec((d, tn), lambda j: (0, j)),
                  pl.BlockSpec((1, tn), lambda j: (0, j))],
        out_specs=pl.BlockSpec((8, tn), lambda j: (0, j)),
        out_shape=jax.ShapeDtypeStruct((8, n), F32),
        compiler_params=_cparams(("arbitrary",)),
        name="mod",
    )(cvec, w_mod, b_mod.reshape(1, n))


def _inproj_kernel(x_ref, g_ref, sh_ref, sc_ref, cos_ref, sin_ref, w_ref,
                   q_ref, k_ref, vt_ref, qkv_ref, gate_ref, ab_ref):
    xb = x_ref[0]
    tm = xb.shape[0]
    ms = jnp.mean(xb * xb, axis=-1, keepdims=True)
    hm = (xb * lax.rsqrt(ms + EPS) * g_ref[...]) * (1.0 + sc_ref[0]) + sh_ref[0]
    hb = hm.astype(BF16)

    def proj(lo, hi):
        return _dot(hb, w_ref[:, lo:hi])

    cos = cos_ref[...]
    sin = sin_ref[...]
    lane = lax.broadcasted_iota(jnp.int32, (tm, LANES), 1)
    first_half = (lane % ROPE_AXIS_DIM) < (ROPE_AXIS_DIM // 2)

    def rope(t):
        outs = []
        for j in range(t.shape[1] // LANES):
            s = t[:, LANES * j:LANES * (j + 1)]
            partner = jnp.where(first_half,
                                pltpu.roll(s, LANES - ROPE_AXIS_DIM // 2, 1),
                                pltpu.roll(s, ROPE_AXIS_DIM // 2, 1))
            outs.append(s * cos + partner * sin)
        return jnp.concatenate(outs, axis=1)

    q_ref[0] = (rope(proj(COL_Q, COL_K)) * (DA_D ** -0.5 * LOG2_E)).astype(BF16)
    k_ref[0] = rope(proj(COL_K, COL_V)).astype(BF16)
    vt = proj(COL_V, COL_QKV).T.astype(BF16)
    vt_ref[0, :, 0, 0:DA_DV, :] = vt.reshape(DA_HEADS, DA_DV, tm)
    extra = lax.broadcasted_iota(jnp.int32, (DA_HEADS, VT_ROWS - DA_DV, tm), 1)
    vt_ref[0, :, 0, DA_DV:VT_ROWS, :] = jnp.where(extra == 0, 1.0, 0.0).astype(BF16)
    qkv_ref[0] = proj(COL_QKV, COL_GATE)
    gate_ref[0] = proj(COL_GATE, COL_AB)
    ab_ref[0] = proj(COL_AB, IN_COLS_PAD)


def _inproj_call(x, norm_g, shift, scale, cos, sin, w_bf, tm):
    b, n, d = x.shape
    row = lambda bi, i: (bi, i, 0)
    vec = lambda bi, i: (bi, 0, 0)
    outs = [(DA_QK, BF16), (DA_QK, BF16), None, (GDN_QKV, F32), (GDN_V, F32), (LANES, F32)]
    out_specs = [pl.BlockSpec((1, tm, o[0]), row) if o else
                 pl.BlockSpec((1, DA_HEADS, 1, VT_ROWS, tm), lambda bi, i: (bi, 0, i, 0, 0)) for o in outs]
    out_shape = [jax.ShapeDtypeStruct((b, n, o[0]), o[1]) if o else
                 jax.ShapeDtypeStruct((b, DA_HEADS, n // tm, VT_ROWS, tm), BF16) for o in outs]
    return pl.pallas_call(
        _inproj_kernel,
        grid=(b, n // tm),
        in_specs=[pl.BlockSpec((1, tm, d), row),
                  pl.BlockSpec((1, d), lambda bi, i: (0, 0)),
                  pl.BlockSpec((1, 1, d), vec),
                  pl.BlockSpec((1, 1, d), vec),
                  pl.BlockSpec((tm, LANES), lambda bi, i: (i, 0)),
                  pl.BlockSpec((tm, LANES), lambda bi, i: (i, 0)),
                  pl.BlockSpec((d, IN_COLS_PAD), lambda bi, i: (0, 0))],
        out_specs=out_specs,
        out_shape=out_shape,
        compiler_params=_cparams(("parallel", "parallel")),
        name="inproj",
    )(x, norm_g, shift, scale, cos, sin, w_bf)


def _attn_kernel(q_ref, k_ref, vt_ref, lam_ref, g_ref, o_ref,
                 q2_ref, acc_ref, m_ref, s_ref, p_ref, a_ref, *, tk):
    q = q_ref[0]
    tq = q.shape[0]
    lane = lax.broadcasted_iota(jnp.int32, (tq, LANES), 1)
    zero = jnp.zeros_like(q)
    q2_ref[0:tq, :] = jnp.where(lane < DA_D, q, zero)
    q2_ref[tq:2 * tq, :] = jnp.where(lane >= DA_D, q, zero)
    m_ref[...] = jnp.full(m_ref.shape, NEG_BIG, F32)
    acc_ref[...] = jnp.zeros(acc_ref.shape, F32)

    nct = 2 * tq // ATTN_COL_TILE
    col = lambda ct: slice(ct * ATTN_COL_TILE, (ct + 1) * ATTN_COL_TILE)

    def scores(kk, ct):
        return _dot_nt(kk, q2_ref[col(ct), :])

    def softmax_update(ct, s):
        cols = col(ct)
        m_old = m_ref[:, cols]
        m_new = jnp.maximum(m_old, jnp.max(s, axis=0, keepdims=True))
        alpha = jnp.exp2(m_old - m_new)
        p = jnp.exp2(s - m_new)
        m_ref[:, cols] = m_new
        return p.astype(BF16), alpha

    def value_update(ct, vt, p, alpha):
        cols = col(ct)
        acc_ref[:, cols] = alpha * acc_ref[:, cols] + _dot(vt, p)

    nchunk = k_ref.shape[1] // tk
    last = nct - 1
    s_ref[...] = scores(k_ref[0, 0:tk, :], 0)
    p_ref[...] = jnp.zeros(p_ref.shape, BF16)
    a_ref[...] = jnp.ones(a_ref.shape, F32)

    def body(j, carry):
        kk = k_ref[0, pl.ds(pl.multiple_of(j * tk, tk), tk), :]
        vt = vt_ref[0, 0, j]
        s_cur = s_ref[...]
        pending = (last, vt_ref[0, 0, jnp.maximum(j - 1, 0)], p_ref[...], a_ref[...])
        for ct in range(nct):
            if ct < last:
                s_ahead = scores(kk, ct + 1)
            else:
                jn = jnp.minimum(j + 1, nchunk - 1)
                s_ahead = scores(k_ref[0, pl.ds(pl.multiple_of(jn * tk, tk), tk), :], 0)
            p, alpha = softmax_update(ct, s_cur)
            value_update(*pending)
            pending = (ct, vt, p, alpha)
            s_cur = s_ahead
        s_ref[...] = s_cur
        p_ref[...] = pending[2]
        a_ref[...] = pending[3]
        return carry

    lax.fori_loop(0, nchunk, body, 0)
    value_update(last, vt_ref[0, 0, nchunk - 1], p_ref[...], a_ref[...])

    lv = lam_ref[...]
    lam = (jnp.exp(jnp.sum(lv[0:1] * lv[1:2], axis=1, keepdims=True))
           - jnp.exp(jnp.sum(lv[2:3] * lv[3:4], axis=1, keepdims=True)) + LAM_INIT)
    den = acc_ref[DA_DV:DA_DV + 1, :]
    ot = (acc_ref[0:DA_DV, 0:tq] / den[:, 0:tq]
          - lam * (acc_ref[0:DA_DV, tq:2 * tq] / den[:, tq:2 * tq]))
    o = ot.T
    ms = jnp.mean(o * o, axis=-1, keepdims=True)
    o_ref[0] = ((o * lax.rsqrt(ms + EPS) * g_ref[...]) * (1.0 - LAM_INIT)).astype(o_ref.dtype)


def _attn_call(q, k_all, vt_all, lam_vecs, subln_g, tq):
    b, n, _ = q.shape
    nk = k_all.shape[1]
    _, _, nchunk, _, tk = vt_all.shape
    return pl.pallas_call(
        functools.partial(_attn_kernel, tk=tk),
        grid=(b, DA_HEADS, n // tq),
        in_specs=[pl.BlockSpec((1, tq, LANES), lambda bi, h, i: (bi, i, h)),
                  pl.BlockSpec((1, nk, LANES), lambda bi, h, i: (bi, 0, h)),
                  pl.BlockSpec((1, 1, nchunk, VT_ROWS, tk), lambda bi, h, i: (bi, h, 0, 0, 0)),
                  pl.BlockSpec((4, DA_D), lambda bi, h, i: (0, 0)),
                  pl.BlockSpec((1, DA_DV), lambda bi, h, i: (0, 0))],
        out_specs=pl.BlockSpec((1, tq, LANES), lambda bi, h, i: (bi, i, h)),
        out_shape=jax.ShapeDtypeStruct((b, n, DA_V), BF16),
        scratch_shapes=[pltpu.VMEM((2 * tq, LANES), BF16), pltpu.VMEM((VT_ROWS, 2 * tq), F32),
                        pltpu.VMEM((1, 2 * tq), F32),
                        pltpu.VMEM((tk, ATTN_COL_TILE), F32), pltpu.VMEM((tk, ATTN_COL_TILE), BF16),
                        pltpu.VMEM((1, ATTN_COL_TILE), F32)],
        compiler_params=_cparams(("parallel", "parallel", "parallel")),
        name="attn",
    )(q, k_all, vt_all, lam_vecs, subln_g)


def _gdnpre_kernel(x_ref, xp_ref, xn_ref, cw_ref, ab_ref, al_ref, dtb_ref,
                   q_ref, k_ref, v_ref, gb_ref, gbt_ref):
    i = pl.program_id(1)
    nblk = pl.num_programs(1)
    x = x_ref[0]
    tm = x.shape[0]
    prev = jnp.where(i > 0, xp_ref[0][7:8, :], 0.0)
    nxt = jnp.where(i < nblk - 1, xn_ref[0][0:1, :], 0.0)
    row = lax.broadcasted_iota(jnp.int32, (tm, 1), 0)
    xm1 = jnp.where(row == 0, prev, pltpu.roll(x, 1, 0))
    xp1 = jnp.where(row == tm - 1, nxt, pltpu.roll(x, tm - 1, 0))
    w = cw_ref[...]
    s = _silu(xm1 * w[0:1] + x * w[1:2] + xp1 * w[2:3])

    def l2n(t):
        return t * lax.rsqrt(jnp.sum(t * t, axis=-1, keepdims=True) + EPS)

    for h in range(GDN_HEADS):
        lo, hi = h * GDN_DK, (h + 1) * GDN_DK
        q_ref[0, :, lo:hi] = l2n(s[:, lo:hi]) * GDN_DK ** -0.5
        k_ref[0, :, lo:hi] = l2n(s[:, GDN_QK + lo:GDN_QK + hi])
    v_ref[0] = s[:, 2 * GDN_QK:]

    ab = ab_ref[0]
    lane = lax.broadcasted_iota(jnp.int32, ab.shape, 1)
    z = ab + dtb_ref[...]
    softplus = jnp.maximum(z, 0.0) + jnp.log1p(jnp.exp(-jnp.abs(z)))
    gval = -jnp.exp(al_ref[...]) * softplus
    gb = jnp.where(lane < GDN_AB // 2, gval, jnp.where(lane < GDN_AB, _sigmoid(ab), 0.0))
    gb_ref[0] = gb
    gbt_ref[0] = gb.T[:GDN_AB, :]


def _gdnpre_call(qkv, ab, conv_w8, alog_row, dtb_row, tm):
    b, n, c = qkv.shape
    nb8 = n // 8
    step8 = tm // 8
    row = lambda bi, i: (bi, i, 0)
    return pl.pallas_call(
        _gdnpre_kernel,
        grid=(b, n // tm),
        in_specs=[pl.BlockSpec((1, tm, c), row),
                  pl.BlockSpec((1, 8, c), lambda bi, i: (bi, jnp.maximum(i * step8 - 1, 0), 0)),
                  pl.BlockSpec((1, 8, c), lambda bi, i: (bi, jnp.minimum((i + 1) * step8, nb8 - 1), 0)),
                  pl.BlockSpec((8, c), lambda bi, i: (0, 0)),
                  pl.BlockSpec((1, tm, LANES), row),
                  pl.BlockSpec((1, LANES), lambda bi, i: (0, 0)),
                  pl.BlockSpec((1, LANES), lambda bi, i: (0, 0))],
        out_specs=[pl.BlockSpec((1, tm, GDN_QK), row), pl.BlockSpec((1, tm, GDN_QK), row),
                   pl.BlockSpec((1, tm, GDN_V), row), pl.BlockSpec((1, tm, LANES), row),
                   pl.BlockSpec((1, GDN_AB, tm), lambda bi, i: (bi, 0, i))],
        out_shape=[jax.ShapeDtypeStruct((b, n, GDN_QK), F32), jax.ShapeDtypeStruct((b, n, GDN_QK), F32),
                   jax.ShapeDtypeStruct((b, n, GDN_V), F32), jax.ShapeDtypeStruct((b, n, LANES), F32),
                   jax.ShapeDtypeStruct((b, GDN_AB, n), F32)],
        compiler_params=_cparams(("parallel", "parallel")),
        name="gdn_pre",
    )(qkv, qkv, qkv, conv_w8, ab, alog_row, dtb_row)


def _solve_unit_triangular(a_list, rhs_list, bd_mask):
    def mm(xs, ys):
        return [_dot(x.astype(BF16), y.astype(BF16)) for x, y in zip(xs, ys)]

    d = [jnp.where(bd_mask, a, 0.0) for a in a_list]
    n = [a - di for a, di in zip(a_list, d)]
    d2 = mm(d, d)
    dd2 = mm(d, d2)
    p = [(-di + d2i) - t for di, d2i, t in zip(d, d2, dd2)]
    d4 = mm(d2, d2)
    pd4 = mm(p, d4)
    p = [pi + d4i + t for pi, d4i, t in zip(p, d4, pd4)]
    d8 = mm(d4, d4)
    pd8 = mm(p, d8)
    p = [pi + d8i + t for pi, d8i, t in zip(p, d8, pd8)]
    m = [ni + t for ni, t in zip(n, mm(p, n))]
    z = [ri + t for ri, t in zip(rhs_list, mm(p, rhs_list))]
    m2 = mm(m, m)
    z = [zi + t for zi, t in zip(z, mm(m2, z))]
    return [zi - t for zi, t in zip(z, mm(m, z))]


def _gdn_scan_kernel(qf_ref, kf_ref, vf_ref, gbf_ref, gbtf_ref, qb_ref, kb_ref, vb_ref, gbb_ref, gbtb_ref,
                     s0_ref, of_ref, ob_ref, sfin_ref, s_ref):
    i = pl.program_id(1)

    @pl.when(i == 0)
    def _():
        s_ref[...] = s0_ref[0]

    c64 = CHUNK
    nchunk = qf_ref.shape[1] // c64
    ri = lax.broadcasted_iota(jnp.int32, (c64, c64), 0)
    ci = lax.broadcasted_iota(jnp.int32, (c64, c64), 1)
    bd_mask = (ri // 16) == (ci // 16)
    lower_incl, upper_incl = ri >= ci, ri <= ci
    heads = range(GDN_HEADS)
    rows = [slice(c * c64, (c + 1) * c64) for c in range(nchunk)]
    cols = [slice(h * GDN_DK, (h + 1) * GDN_DK) for h in heads]

    dirs = []
    for d, refs in enumerate(((qf_ref, kf_ref, vf_ref, gbf_ref, gbtf_ref, of_ref),
                              (qb_ref, kb_ref, vb_ref, gbb_ref, gbtb_ref, ob_ref))):
        rev = d == 1
        incl = upper_incl if rev else lower_incl
        dirs.append(dict(
            refs=refs, incl=incl, strict=(ri < ci) if rev else (ri > ci),
            tri_c=jnp.where(incl, 1.0, 0.0).astype(BF16),
            tri_r=jnp.where(lower_incl if rev else upper_incl, 1.0, 0.0).astype(BF16),
            last=0 if rev else c64 - 1,
            order=list(range(nchunk - 1, -1, -1) if rev else range(nchunk))))

    gall, gc_col, gc_row = {}, {}, {}
    for d, dr in enumerate(dirs):
        gb_ref, gbt_ref = dr["refs"][3], dr["refs"][4]
        for c in range(nchunk):
            gall[d, c] = gb_ref[0, rows[c], :]
            g1, g2, g3 = _split3(gall[d, c])
            tc, tr = dr["tri_c"], dr["tri_r"]
            gc_col[d, c] = _dot(tc, g1) + _dot(tc, g2) + _dot(tc, g3)
            t1, t2, t3 = _split3(gbt_ref[0, :, rows[c]])
            gc_row[d, c] = _dot(t1, tr) + _dot(t2, tr) + _dot(t3, tr)

    probs = [(d, c, h) for d in range(2) for c in range(nchunk) for h in heads]
    ln = lambda d, h: d * GDN_HEADS + h
    k = {(d, c, h): dirs[d]["refs"][1][0, rows[c], cols[h]] for d, c, h in probs}
    q = {(d, c, h): dirs[d]["refs"][0][0, rows[c], cols[h]] for d, c, h in probs}
    kbf = {p: k[p].astype(BF16) for p in probs}
    kk = {p: _dot_nt(kbf[p], kbf[p]) for p in probs}
    qk = {p: _dot_nt(q[p].astype(BF16), kbf[p]) for p in probs}
    a_list, rhs_list = [], []
    kd, qd, aqk, glast = {}, {}, {}, {}
    for p in probs:
        d, c, h = p
        dr = dirs[d]
        lane = ln(d, h)
        gcol = gc_col[d, c][:, lane:lane + 1]
        bcol = gall[d, c][:, GDN_AB // 2 + lane:GDN_AB // 2 + lane + 1]
        diff = gcol - gc_row[d, c][lane:lane + 1, :]
        e_strict = jnp.where(dr["strict"], jnp.exp(jnp.where(dr["strict"], diff, 0.0)), 0.0)
        e_incl = jnp.where(dr["incl"], jnp.exp(jnp.where(dr["incl"], diff, 0.0)), 0.0)
        eg = jnp.exp(gcol)
        glast[p] = gcol[dr["last"]:dr["last"] + 1, :]
        a_list.append(bcol * kk[p] * e_strict)
        v = dr["refs"][2][0, rows[c], cols[h]]
        rhs_list.append(jnp.concatenate([(bcol * eg) * k[p], bcol * v], axis=1))
        kd[p] = (k[p] * jnp.exp(glast[p] - gcol)).astype(BF16)
        qd[p] = (q[p] * eg).astype(BF16)
        aqk[p] = (qk[p] * e_incl).astype(BF16)
    sol = dict(zip(probs, _solve_unit_triangular(a_list, rhs_list, bd_mask)))

    chains = [(d, h) for d in range(2) for h in heads]
    for t in range(nchunk):
        cur = {(d, h): (d, dirs[d]["order"][t], h) for d, h in chains}
        s = {dh: s_ref[dh[0], dh[1]] for dh in chains}
        sb = {dh: s[dh].astype(BF16) for dh in chains}
        ws = {dh: _dot(sol[cur[dh]][:, :GDN_DK].astype(BF16), sb[dh]) for dh in chains}
        qs = {dh: _dot(qd[cur[dh]], sb[dh]) for dh in chains}
        u = {dh: (sol[cur[dh]][:, GDN_DK:] - ws[dh]).astype(BF16) for dh in chains}
        au = {dh: _dot(aqk[cur[dh]], u[dh]) for dh in chains}
        ku = {dh: _dot_tn(kd[cur[dh]], u[dh]) for dh in chains}
        for dh in chains:
            d, c, h = cur[dh]
            s_ref[d, h] = jnp.exp(glast[cur[dh]]) * s[dh] + ku[dh]
            dirs[d]["refs"][5][0, rows[c], cols[h]] = qs[dh] + au[dh]

    @pl.when(i == pl.num_programs(1) - 1)
    def _():
        sfin_ref[0] = s_ref[...]


def _gdn_scan_call(q, k, v, gb, gbt, s0, sc):
    b, n, _ = q.shape
    nsup = n // sc
    fwd = lambda bi, i: (bi, i, 0)
    bwd = lambda bi, i: (bi, nsup - 1 - i, 0)
    st = lambda bi, i: (bi, 0, 0, 0, 0)
    state = (1, 2, GDN_HEADS, GDN_DK, GDN_DV)

    def in_specs(row, tr):
        return [pl.BlockSpec((1, sc, GDN_QK), row), pl.BlockSpec((1, sc, GDN_QK), row),
                pl.BlockSpec((1, sc, GDN_V), row), pl.BlockSpec((1, sc, LANES), row),
                pl.BlockSpec((1, GDN_AB, sc), tr)]

    return pl.pallas_call(
        _gdn_scan_kernel,
        grid=(b, nsup),
        in_specs=(in_specs(fwd, lambda bi, i: (bi, 0, i))
                  + in_specs(bwd, lambda bi, i: (bi, 0, nsup - 1 - i))
                  + [pl.BlockSpec(state, st)]),
        out_specs=[pl.BlockSpec((1, sc, GDN_V), fwd), pl.BlockSpec((1, sc, GDN_V), bwd),
                   pl.BlockSpec(state, st)],
        out_shape=[jax.ShapeDtypeStruct((b, n, GDN_V), F32), jax.ShapeDtypeStruct((b, n, GDN_V), F32),
                   jax.ShapeDtypeStruct((b,) + state[1:], F32)],
        scratch_shapes=[pltpu.VMEM(state[1:], F32)],
        compiler_params=_cparams(("parallel", "arbitrary")),
        name="gdn_scan",
    )(q, k, v, gb, gbt, q, k, v, gb, gbt, s0)


def _outproj_kernel(oda_ref, of_ref, ob_ref, gate_ref, x_ref, gt1_ref, gng_ref, wout_ref,
                    n2g_ref, sh2_ref, sc2_ref, wr_ref, xnew_ref, h_ref, aff_ref, afft_ref):
    og = of_ref[0] + ob_ref[0]
    gate = gate_ref[0]
    ys = []
    for h in range(GDN_HEADS):
        cols = slice(h * GDN_DV, (h + 1) * GDN_DV)
        t = og[:, cols]
        y = t * lax.rsqrt(jnp.mean(t * t, axis=-1, keepdims=True) + EPS) * gng_ref[...]
        ys.append((y * _silu(gate[:, cols])).astype(BF16))
    y_gdn = jnp.concatenate(ys, axis=1)
    proj = _dot(oda_ref[0], wout_ref[0:DA_V, :]) + _dot(y_gdn, wout_ref[DA_V:, :])
    xn = x_ref[0] + gt1_ref[0] * proj
    xnew_ref[0] = xn
    ms = jnp.mean(xn * xn, axis=-1, keepdims=True)
    hm = (xn * lax.rsqrt(ms + EPS) * n2g_ref[...]) * (1.0 + sc2_ref[0]) + sh2_ref[0]
    h_ref[0] = hm.astype(BF16)
    logits = _dot_x3(hm, wr_ref[...])
    lane = lax.broadcasted_iota(jnp.int32, logits.shape, 1)
    logits = jnp.where(lane < N_EXPERTS, logits, NEG_BIG)
    e = jnp.exp(logits - jnp.max(logits, axis=-1, keepdims=True))
    aff = e / jnp.sum(e, axis=-1, keepdims=True)
    aff_ref[0] = aff
    afft_ref[0] = aff.T[:N_EXPERTS, :]


def _outproj_call(o_da, o_f, o_b, gate, x, gt1, gdn_norm_g, w_out_bf, norm2_g, sh2, sc2, wr_pad, tm):
    b, n, d = x.shape
    row = lambda bi, i: (bi, i, 0)
    vec = lambda bi, i: (bi, 0, 0)
    const2 = lambda bi, i: (0, 0)
    return pl.pallas_call(
        _outproj_kernel,
        grid=(b, n // tm),
        in_specs=[pl.BlockSpec((1, tm, DA_V), row), pl.BlockSpec((1, tm, GDN_V), row),
                  pl.BlockSpec((1, tm, GDN_V), row), pl.BlockSpec((1, tm, GDN_V), row),
                  pl.BlockSpec((1, tm, d), row), pl.BlockSpec((1, 1, d), vec),
                  pl.BlockSpec((1, GDN_DV), const2), pl.BlockSpec((DA_V + GDN_V, d), const2),
                  pl.BlockSpec((1, d), const2), pl.BlockSpec((1, 1, d), vec), pl.BlockSpec((1, 1, d), vec),
                  pl.BlockSpec((d, LANES), const2)],
        out_specs=[pl.BlockSpec((1, tm, d), row), pl.BlockSpec((1, tm, d), row),
                   pl.BlockSpec((1, tm, LANES), row),
                   pl.BlockSpec((1, N_EXPERTS, tm), lambda bi, i: (bi, 0, i))],
        out_shape=[jax.ShapeDtypeStruct((b, n, d), F32), jax.ShapeDtypeStruct((b, n, d), BF16),
                   jax.ShapeDtypeStruct((b, n, LANES), F32),
                   jax.ShapeDtypeStruct((b, N_EXPERTS, n), F32)],
        compiler_params=_cparams(("parallel", "parallel")),
        name="outproj",
    )(o_da, o_f, o_b, gate, x, gt1, gdn_norm_g, w_out_bf, norm2_g, sh2, sc2, wr_pad)


PREFIX_BLK = 256
BISECT_STEPS = 64


def _select_kernel(afft_ref, aff_ref, post_ref, pos_ref, st_ref, rng_ref, *, cap, tb, tj):
    at = afft_ref[0]
    n = at.shape[1]

    def count_ge(t):
        return jnp.sum(jnp.where(at >= t, 1.0, 0.0), axis=1, keepdims=True)

    def bisect(_, bounds):
        lo, hi = bounds
        mid = 0.5 * (lo + hi)
        ok = count_ge(mid) >= cap
        return jnp.where(ok, mid, lo), jnp.where(ok, hi, mid)

    _, hi = lax.fori_loop(0, BISECT_STEPS, bisect,
                          (jnp.zeros((N_EXPERTS, 1), F32), jnp.full((N_EXPERTS, 1), 2.0, F32)))

    def below(h):
        return jnp.max(jnp.where(at < h, at, -1.0), axis=1, keepdims=True)

    def short(state):
        return jnp.sum(jnp.where(count_ge(state[0]) < cap, 1.0, 0.0)) > 0.0

    def step_down(state):
        t, h = state
        lacking = count_ge(t) < cap
        h = jnp.where(lacking, t, h)
        return jnp.where(lacking, below(h), t), h

    thr, _ = lax.while_loop(short, step_down, (below(hi), hi))
    need = cap - jnp.sum(jnp.where(at > thr, 1.0, 0.0), axis=1, keepdims=True)

    pi = lax.broadcasted_iota(jnp.int32, (PREFIX_BLK, PREFIX_BLK), 0)
    pj = lax.broadcasted_iota(jnp.int32, (PREFIX_BLK, PREFIX_BLK), 1)
    upper = jnp.where(pi <= pj, 1.0, 0.0).astype(BF16)
    lower = jnp.where(pi >= pj, 1.0, 0.0).astype(BF16)

    def prefix_lanes(m):
        carry = jnp.zeros((m.shape[0], 1), F32)
        outs = []
        for c in range(m.shape[1] // PREFIX_BLK):
            mc = m[:, c * PREFIX_BLK:(c + 1) * PREFIX_BLK]
            loc = _dot(mc.astype(BF16), upper)
            outs.append(loc - mc + carry)
            carry = carry + loc[:, PREFIX_BLK - 1:PREFIX_BLK]
        return jnp.concatenate(outs, axis=1)

    def prefix_rows(m):
        carry = jnp.zeros((1, m.shape[1]), F32)
        outs = []
        for c in range(m.shape[0] // PREFIX_BLK):
            mc = m[c * PREFIX_BLK:(c + 1) * PREFIX_BLK, :]
            loc = _dot(lower, mc.astype(BF16))
            outs.append(loc - mc + carry)
            carry = carry + loc[PREFIX_BLK - 1:PREFIX_BLK, :]
        return jnp.concatenate(outs, axis=0)

    eq_t = jnp.where(at == thr, 1.0, 0.0)
    sel_t = jnp.where(at > thr, 1.0, jnp.where(prefix_lanes(eq_t) < need, eq_t, 0.0))
    post_ref[0] = jnp.where(sel_t > 0.0, prefix_lanes(sel_t), -1.0)

    ti = lax.broadcasted_iota(jnp.int32, (n, LANES), 0)
    bi = lax.broadcasted_iota(jnp.int32, (n, LANES), 1)
    before = jnp.where(ti < bi * tb, 1.0, 0.0).astype(BF16)
    st = _dot(sel_t.astype(BF16), before)
    st_ref[0] = st.astype(jnp.int32)

    through = jnp.where(ti < (bi + 1) * tb, 1.0, 0.0).astype(BF16)
    st_end = _dot(sel_t.astype(BF16), through)
    blk_lane = lax.broadcasted_iota(jnp.int32, (N_EXPERTS, LANES), 1)
    is_blk = blk_lane < n // tb
    rng = jnp.zeros((N_EXPERTS, LANES), F32)
    ntile = cap // tj
    for j in range(ntile):
        first = jnp.sum(jnp.where(is_blk & (st_end <= j * tj), 1.0, 0.0), axis=1, keepdims=True)
        last = jnp.sum(jnp.where(is_blk & (st < (j + 1) * tj), 1.0, 0.0), axis=1, keepdims=True) - 1.0
        rng = rng + jnp.where(blk_lane == j, first, 0.0) + jnp.where(blk_lane == ntile + j, last, 0.0)
    rng_ref[0] = rng.astype(jnp.int32)

    a = aff_ref[0]
    er = lax.broadcasted_iota(jnp.int32, (N_EXPERTS, LANES), 0)
    ec = lax.broadcasted_iota(jnp.int32, (N_EXPERTS, LANES), 1)
    diag = er == ec
    thr_row = jnp.sum(jnp.where(diag, thr, 0.0), axis=0, keepdims=True)
    need_row = jnp.sum(jnp.where(diag, need, 0.0), axis=0, keepdims=True)
    valid = lax.broadcasted_iota(jnp.int32, a.shape, 1) < N_EXPERTS
    eq = jnp.where(valid & (a == thr_row), 1.0, 0.0)
    sel = jnp.where(valid & (a > thr_row), 1.0, jnp.where(prefix_rows(eq) < need_row, eq, 0.0))
    pos_ref[0] = jnp.where(sel > 0.0, prefix_rows(sel), -1.0)


def _select_call(afft, aff, cap, tb, tj):
    b, e, n = afft.shape
    assert n // tb + 1 <= LANES and 2 * (cap // tj) <= LANES
    return pl.pallas_call(
        functools.partial(_select_kernel, cap=cap, tb=tb, tj=tj),
        grid=(b,),
        in_specs=[pl.BlockSpec((1, e, n), lambda bi: (bi, 0, 0)),
                  pl.BlockSpec((1, n, LANES), lambda bi: (bi, 0, 0))],
        out_specs=[pl.BlockSpec((1, e, n), lambda bi: (bi, 0, 0)),
                   pl.BlockSpec((1, n, LANES), lambda bi: (bi, 0, 0)),
                   pl.BlockSpec((1, e, LANES), lambda bi: (bi, 0, 0)),
                   pl.BlockSpec((1, e, LANES), lambda bi: (bi, 0, 0))],
        out_shape=[jax.ShapeDtypeStruct((b, e, n), F32), jax.ShapeDtypeStruct((b, n, LANES), F32),
                   jax.ShapeDtypeStruct((b, e, LANES), jnp.int32),
                   jax.ShapeDtypeStruct((b, e, LANES), jnp.int32)],
        compiler_params=_cparams(("parallel",)),
        name="select",
    )(afft, aff)


def _moe_ffn_kernel(rng_ref, h_ref, post_ref, afft_ref, wg_ref, wu_ref, wd_ref, ye_ref,
                    acc_ref, gacc_ref, wgb_ref, wub_ref, wdb_ref, *, tj, tb):
    b = pl.program_id(0)
    e = pl.program_id(1)
    cap = ye_ref.shape[2]
    ntile = cap // tj
    slot = lax.broadcasted_iota(jnp.int32, (tj, tb), 0).astype(F32)
    wgb_ref[...] = wg_ref[0].astype(BF16)
    wub_ref[...] = wu_ref[0].astype(BF16)
    wdb_ref[...] = wd_ref[0].astype(BF16)

    for j in range(cap // tj):
        lo = j * tj
        acc_ref[...] = jnp.zeros(acc_ref.shape, F32)
        gacc_ref[...] = jnp.zeros(gacc_ref.shape, F32)

        def blk_body(bi, carry):
            prow = post_ref[0, 0, pl.ds(bi, 1), :]
            hit = prow == (slot + float(lo))
            onehot = jnp.where(hit, 1.0, 0.0).astype(BF16)
            off = pl.multiple_of(bi * tb, tb)
            acc_ref[...] += _dot(onehot, h_ref[0, pl.ds(off, tb), :])
            arow = afft_ref[0, 0, pl.ds(bi, 1), :]
            gacc_ref[...] += jnp.sum(jnp.where(hit, arow, 0.0), axis=1, keepdims=True)
            return carry

        lax.fori_loop(rng_ref[b, e, j], rng_ref[b, e, ntile + j] + 1, blk_body, 0)
        xe = acc_ref[...].astype(BF16)
        hid = _silu(_dot(xe, wgb_ref[...])) * _dot(xe, wub_ref[...])
        ye = _dot(hid.astype(BF16), wdb_ref[...]) * gacc_ref[...]
        ye_ref[0, 0, lo:lo + tj, :] = ye.astype(ye_ref.dtype)


def _moe_ffn_call(starts, h, post4, afft4, wg, wu, wd, cap, tj, tb):
    b, n, d = h.shape
    e, _, f = wg.shape
    nblk = n // tb
    wspec = lambda shp: pl.BlockSpec((1,) + shp, lambda bi, ei: (ei, 0, 0))
    return pl.pallas_call(
        functools.partial(_moe_ffn_kernel, tj=tj, tb=tb),
        grid=(b, e),
        in_specs=[pl.BlockSpec(memory_space=pltpu.SMEM),
                  pl.BlockSpec((1, n, d), lambda bi, ei: (bi, 0, 0), pipeline_mode=pl.Buffered(1)),
                  pl.BlockSpec((1, 1, nblk, tb), lambda bi, ei: (bi, ei, 0, 0)),
                  pl.BlockSpec((1, 1, nblk, tb), lambda bi, ei: (bi, ei, 0, 0)),
                  wspec((d, f)), wspec((d, f)), wspec((f, d))],
        out_specs=pl.BlockSpec((1, 1, cap, d), lambda bi, ei: (bi, ei, 0, 0)),
        out_shape=jax.ShapeDtypeStruct((b, e, cap, d), BF16),
        scratch_shapes=[pltpu.VMEM((tj, d), F32), pltpu.VMEM((tj, 1), F32),
                        pltpu.VMEM((d, f), BF16), pltpu.VMEM((d, f), BF16), pltpu.VMEM((f, d), BF16)],
        compiler_params=_cparams(("parallel", "arbitrary")),
        name="moe_ffn",
    )(starts, h, post4, afft4, wg, wu, wd)


def _combine_kernel(st_ref, pos_ref, ye_ref, x_ref, gt2_ref, g_ref, o_ref, *, tb):
    b = pl.program_id(0)
    blk = pl.program_id(1)
    cap = ye_ref.shape[2]
    win = 2 * tb
    slot = lax.broadcasted_iota(jnp.int32, (tb, win), 1).astype(F32)
    pos = pos_ref[0]
    moe = jnp.zeros((tb, ye_ref.shape[3]), F32)
    for e in range(N_EXPERTS):
        base = jnp.minimum(st_ref[b, e, blk] // tb, cap // tb - 2) * tb
        onehot = jnp.where(pos[:, e:e + 1] == slot + base.astype(F32), 1.0, 0.0).astype(BF16)
        moe = moe + _dot(onehot, ye_ref[0, e, pl.ds(pl.multiple_of(base, tb), win), :])
    y = x_ref[0] + gt2_ref[0] * moe
    ms = jnp.mean(y * y, axis=-1, keepdims=True)
    o_ref[0] = y * lax.rsqrt(ms + EPS) * g_ref[...]


def _combine_call(starts, pos, ye, x_new, gt2, final_g, tb):
    b, e, cap, d = ye.shape
    n = pos.shape[1]
    assert cap % tb == 0 and cap // tb >= 2
    row = lambda bi, i: (bi, i, 0)
    return pl.pallas_call(
        functools.partial(_combine_kernel, tb=tb),
        grid=(b, n // tb),
        in_specs=[pl.BlockSpec(memory_space=pltpu.SMEM),
                  pl.BlockSpec((1, tb, LANES), row),
                  pl.BlockSpec((1, e, cap, d), lambda bi, i: (bi, 0, 0, 0), pipeline_mode=pl.Buffered(1)),
                  pl.BlockSpec((1, tb, d), row),
                  pl.BlockSpec((1, 1, d), lambda bi, i: (bi, 0, 0)),
                  pl.BlockSpec((1, d), lambda bi, i: (0, 0))],
        out_specs=pl.BlockSpec((1, tb, d), row),
        out_shape=jax.ShapeDtypeStruct((b, n, d), F32),
        compiler_params=_cparams(("parallel", "arbitrary")),
        name="combine",
    )(starts, pos, ye, x_new, gt2, final_g)


def _rope_tables(n):
    t = np.arange(n)
    rows = (t // GRID_W).astype(np.float64)
    cols = (t % GRID_W).astype(np.float64)
    inv_freq = np.power(ROPE_THETA, -np.arange(0, ROPE_AXIS_DIM, 2, dtype=np.float64) / ROPE_AXIS_DIM)
    ang_row = rows[:, None] * inv_freq[None, :]
    ang_col = cols[:, None] * inv_freq[None, :]

    def axis_tables(ang):
        c = np.cos(ang).astype(np.float32)
        s = np.sin(ang).astype(np.float32)
        return np.concatenate([c, c], axis=1), np.concatenate([-s, s], axis=1)

    cr, sr = axis_tables(ang_row)
    cc, sc = axis_tables(ang_col)
    cos64 = np.concatenate([cr, cc], axis=1)
    sin64 = np.concatenate([sr, sc], axis=1)
    return (jnp.asarray(np.concatenate([cos64, cos64], axis=1)),
            jnp.asarray(np.concatenate([sin64, sin64], axis=1)))


def _pad_lanes(v):
    return jnp.pad(v.reshape(1, -1), ((0, 0), (0, LANES - v.size)))


def kernel(x, c, ctx, c_ctx, w_mod, b_mod, norm1_g, w_in, conv_w, a_log, dt_bias, gdn_norm_g,
           lam_q1, lam_k1, lam_q2, lam_k2, da_subln_g, w_out, norm2_g,
           w_router, w_gate, w_up, w_down, final_g):
    b, n, d = x.shape
    nc = ctx.shape[1]
    layer = 0

    cvec = jnp.concatenate([c, c_ctx[None, :], jnp.zeros((8 - b - 1, d), F32)], axis=0)
    mod = _mod_call(cvec, w_mod[layer], b_mod[layer])
    sh1, sc1, gt1, sh2, sc2, gt2 = [mod[:b, i * d:(i + 1) * d].reshape(b, 1, d) for i in range(6)]
    sh1c = jnp.broadcast_to(mod[b:b + 1, 0:d].reshape(1, 1, d), (b, 1, d))
    sc1c = jnp.broadcast_to(mod[b:b + 1, d:2 * d].reshape(1, 1, d), (b, 1, d))

    w_in_bf = jnp.pad(w_in[layer].astype(BF16), ((0, 0), (0, IN_COLS_PAD - w_in.shape[2])))
    g1 = norm1_g[layer].reshape(1, d)
    cos_l, sin_l = _rope_tables(n)
    cos_c, sin_c = jnp.ones((nc, LANES), F32), jnp.zeros((nc, LANES), F32)
    q, k, vt, qkv, gate, ab = _inproj_call(x, g1, sh1, sc1, cos_l, sin_l, w_in_bf, tm=512)
    _, kc, vct, qkvc, _, abc = _inproj_call(ctx, g1, sh1c, sc1c, cos_c, sin_c, w_in_bf, tm=nc)

    lam_vecs = jnp.stack([lam_q1[layer], lam_k1[layer], lam_q2[layer], lam_k2[layer]], axis=0)
    tk = vt.shape[-1]
    reps = tk // nc
    k_all = jnp.concatenate([kc] * reps + [k], axis=1)
    vct_pad = jnp.pad(vct, ((0, 0), (0, 0), (0, 0), (0, 0), (0, tk - nc)))
    vt_all = jnp.concatenate([vct_pad, vt], axis=2)
    o_da = _attn_call(q, k_all, vt_all, lam_vecs, da_subln_g[layer].reshape(1, DA_DV), tq=1024)

    conv_w8 = jnp.pad(conv_w[layer], ((0, 8 - conv_w.shape[1]), (0, 0)))
    alog_row = _pad_lanes(a_log[layer])
    dtb_row = _pad_lanes(dt_bias[layer])
    ql, kl, vl, gbl, gbtl = _gdnpre_call(qkv, ab, conv_w8, alog_row, dtb_row, tm=512)
    qc, kcg, vcg, gbc, gbtc = _gdnpre_call(qkvc, abc, conv_w8, alog_row, dtb_row, tm=nc)
    zeros_state = jnp.zeros((b, 2, GDN_HEADS, GDN_DK, GDN_DV), F32)
    _, _, s_ctx = _gdn_scan_call(qc, kcg, vcg, gbc, gbtc, zeros_state, sc=nc)
    o_fwd, o_bwd, _ = _gdn_scan_call(ql, kl, vl, gbl, gbtl, s_ctx, sc=256)
    o_dirs = [o_fwd, o_bwd]

    wr_pad = jnp.pad(w_router[layer], ((0, 0), (0, LANES - N_EXPERTS)))
    x_new, h, aff, afft = _outproj_call(
        o_da, o_dirs[0], o_dirs[1], gate, x, gt1, gdn_norm_g[layer].reshape(1, GDN_DV),
        w_out[layer].astype(BF16), norm2_g[layer].reshape(1, d), sh2, sc2, wr_pad, tm=512)

    cap = CAP_FACTOR * n // N_EXPERTS
    tb = 256
    tj = 256
    post, pos, starts, tile_rng = _select_call(afft, aff, cap, tb, tj)
    nblk = n // tb
    ye = _moe_ffn_call(tile_rng, h, post.reshape(b, N_EXPERTS, nblk, tb),
                       afft.reshape(b, N_EXPERTS, nblk, tb),
                       w_gate[layer], w_up[layer], w_down[layer], cap, tj, tb)
    return _combine_call(starts, pos, ye, x_new, gt2, final_g.reshape(1, d), tb)
```

```python
import functools
import math

import jax
import jax.numpy as jnp
import numpy as np
from jax import lax
from jax.experimental import pallas as pl
from jax.experimental.pallas import tpu as pltpu

F32 = jnp.float32
BF16 = jnp.bfloat16

D_MODEL = 1024
GRID_W = 64
EPS = 1e-6
DA_HEADS = 4
DA_D = 64
DA_DV = 2 * DA_D
ROPE_AXIS_DIM = DA_D // 2
ROPE_THETA = 10000.0
GDN_HEADS = 4
GDN_DK = 128
GDN_DV = 128
CHUNK = 64
N_EXPERTS = 16
CAP_FACTOR = 2
LAM_INIT = 0.8 - 0.6 * math.exp(-0.3 * 0)

DA_QK = DA_HEADS * 2 * DA_D
DA_V = DA_HEADS * DA_DV
GDN_QK = GDN_HEADS * GDN_DK
GDN_V = GDN_HEADS * GDN_DV
GDN_QKV = 2 * GDN_QK + GDN_V
GDN_AB = 2 * 2 * GDN_HEADS
COL_Q, COL_K, COL_V = 0, DA_QK, 2 * DA_QK
COL_QKV = 2 * DA_QK + DA_V
COL_GATE = COL_QKV + GDN_QKV
COL_AB = COL_GATE + GDN_V
LANES = 128
IN_COLS_PAD = COL_AB + LANES
MXU_WIDTH = 256
ATTN_COL_TILE = MXU_WIDTH
VT_ROWS = DA_DV + 8

VMEM_LIMIT = 56 * 1024 * 1024
NEG_BIG = -1e30
LOG2_E = math.log2(math.e)


def _cparams(sem):
    return pltpu.CompilerParams(dimension_semantics=sem, vmem_limit_bytes=VMEM_LIMIT)


def _sigmoid(x):
    return 1.0 / (1.0 + jnp.exp(-x))


def _silu(x):
    return x * _sigmoid(x)


def _split3(a):
    a1 = a.astype(BF16)
    r1 = a - a1.astype(F32)
    a2 = r1.astype(BF16)
    a3 = (r1 - a2.astype(F32)).astype(BF16)
    return a1, a2, a3


def _dot(a, b):
    return jnp.dot(a, b, preferred_element_type=F32)


def _dot_x3(a, b):
    a1, a2, _ = _split3(a)
    b1, b2, _ = _split3(b)
    return _dot(a1, b1) + (_dot(a1, b2) + _dot(a2, b1))


def _dot_nt(a, b):
    return lax.dot_general(a, b, (((1,), (1,)), ((), ())), preferred_element_type=F32)


def _dot_tn(a, b):
    return lax.dot_general(a, b, (((0,), (0,)), ((), ())), preferred_element_type=F32)


def _mod_kernel(c_ref, w_ref, b_ref, o_ref):
    s = _silu(c_ref[...])
    o_ref[...] = jnp.dot(s, w_ref[...], precision=lax.Precision.HIGHEST,
                         preferred_element_type=F32) + b_ref[...]


def _mod_call(cvec, w_mod, b_mod):
    d, n = w_mod.shape
    tn = 1024
    return pl.pallas_call(
        _mod_kernel,
        grid=(n // tn,),
        in_specs=[pl.BlockSpec((8, d), lambda j: (0, 0)),
                  pl.BlockSpec((d, tn), lambda j: (0, j)),
                  pl.BlockSpec((1, tn), lambda j: (0, j))],
        out_specs=pl.BlockSpec((8, tn), lambda j: (0, j)),
        out_shape=jax.ShapeDtypeStruct((8, n), F32),
        compiler_params=_cparams(("arbitrary",)),
        name="mod",
    )(cvec, w_mod, b_mod.reshape(1, n))


def _inproj_kernel(x_ref, xp_ref, xn_ref, g_ref, sh_ref, sc_ref, cos_ref, sin_ref, w_ref,
                   cw_ref, al_ref, dtb_ref,
                   q_ref, k_ref, vt_ref, gq_ref, gk_ref, gv_ref, gate_ref, gb_ref, gbt_ref):
    i = pl.program_id(1)
    nblk = pl.num_programs(1)
    tm = x_ref.shape[1]

    def modulated(xb):
        ms = jnp.mean(xb * xb, axis=-1, keepdims=True)
        hm = (xb * lax.rsqrt(ms + EPS) * g_ref[...]) * (1.0 + sc_ref[0]) + sh_ref[0]
        return hm.astype(BF16)

    hb = modulated(x_ref[0])

    def proj(lo, hi):
        return _dot(hb, w_ref[:, lo:hi])

    cos = cos_ref[...]
    sin = sin_ref[...]
    lane = lax.broadcasted_iota(jnp.int32, (tm, LANES), 1)
    first_half = (lane % ROPE_AXIS_DIM) < (ROPE_AXIS_DIM // 2)

    def rope(t):
        outs = []
        for j in range(t.shape[1] // LANES):
            s = t[:, LANES * j:LANES * (j + 1)]
            partner = jnp.where(first_half,
                                pltpu.roll(s, LANES - ROPE_AXIS_DIM // 2, 1),
                                pltpu.roll(s, ROPE_AXIS_DIM // 2, 1))
            outs.append(s * cos + partner * sin)
        return jnp.concatenate(outs, axis=1)

    q_ref[0] = (rope(proj(COL_Q, COL_K)) * (DA_D ** -0.5 * LOG2_E)).astype(BF16)
    k_ref[0] = rope(proj(COL_K, COL_V)).astype(BF16)
    vt = proj(COL_V, COL_QKV).T.astype(BF16)
    vt_ref[0, :, 0, 0:DA_DV, :] = vt.reshape(DA_HEADS, DA_DV, tm)
    extra = lax.broadcasted_iota(jnp.int32, (DA_HEADS, VT_ROWS - DA_DV, tm), 1)
    vt_ref[0, :, 0, DA_DV:VT_ROWS, :] = jnp.where(extra == 0, 1.0, 0.0).astype(BF16)
    gate_ref[0] = proj(COL_GATE, COL_AB)

    qkv = proj(COL_QKV, COL_GATE)
    halo = _dot(modulated(jnp.concatenate([xp_ref[0], xn_ref[0]], axis=0)),
                w_ref[:, COL_QKV:COL_GATE])
    prev = jnp.where(i > 0, halo[7:8, :], 0.0)
    nxt = jnp.where(i < nblk - 1, halo[8:9, :], 0.0)
    row = lax.broadcasted_iota(jnp.int32, (tm, 1), 0)
    qkv_m1 = jnp.where(row == 0, prev, pltpu.roll(qkv, 1, 0))
    qkv_p1 = jnp.where(row == tm - 1, nxt, pltpu.roll(qkv, tm - 1, 0))
    cw = cw_ref[...]
    s = _silu(qkv_m1 * cw[0:1] + qkv * cw[1:2] + qkv_p1 * cw[2:3])

    def l2n(t):
        return t * lax.rsqrt(jnp.sum(t * t, axis=-1, keepdims=True) + EPS)

    for h in range(GDN_HEADS):
        lo, hi = h * GDN_DK, (h + 1) * GDN_DK
        gq_ref[0, :, lo:hi] = l2n(s[:, lo:hi]) * GDN_DK ** -0.5
        gk_ref[0, :, lo:hi] = l2n(s[:, GDN_QK + lo:GDN_QK + hi])
    gv_ref[0] = s[:, 2 * GDN_QK:]

    ab = proj(COL_AB, IN_COLS_PAD)
    z = ab + dtb_ref[...]
    softplus = jnp.maximum(z, 0.0) + jnp.log1p(jnp.exp(-jnp.abs(z)))
    gval = -jnp.exp(al_ref[...]) * softplus
    gb = jnp.where(lane < GDN_AB // 2, gval, jnp.where(lane < GDN_AB, _sigmoid(ab), 0.0))
    gb_ref[0] = gb
    gbt_ref[0] = gb.T[:GDN_AB, :]


def _inproj_call(x, norm_g, shift, scale, cos, sin, w_bf, conv_w8, alog_row, dtb_row, tm):
    b, n, d = x.shape
    nb8 = n // 8
    step8 = tm // 8
    row = lambda bi, i: (bi, i, 0)
    vec = lambda bi, i: (bi, 0, 0)
    const = lambda bi, i: (0, 0)
    rows_out = [(DA_QK, BF16), (DA_QK, BF16), None, (GDN_QK, F32), (GDN_QK, F32), (GDN_V, F32),
                (GDN_V, F32), (LANES, F32)]
    out_specs = [pl.BlockSpec((1, tm, o[0]), row) if o else
                 pl.BlockSpec((1, DA_HEADS, 1, VT_ROWS, tm), lambda bi, i: (bi, 0, i, 0, 0))
                 for o in rows_out]
    out_shape = [jax.ShapeDtypeStruct((b, n, o[0]), o[1]) if o else
                 jax.ShapeDtypeStruct((b, DA_HEADS, n // tm, VT_ROWS, tm), BF16) for o in rows_out]
    out_specs.append(pl.BlockSpec((1, GDN_AB, tm), lambda bi, i: (bi, 0, i)))
    out_shape.append(jax.ShapeDtypeStruct((b, GDN_AB, n), F32))
    return pl.pallas_call(
        _inproj_kernel,
        grid=(b, n // tm),
        in_specs=[pl.BlockSpec((1, tm, d), row),
                  pl.BlockSpec((1, 8, d), lambda bi, i: (bi, jnp.maximum(i * step8 - 1, 0), 0)),
                  pl.BlockSpec((1, 8, d), lambda bi, i: (bi, jnp.minimum((i + 1) * step8, nb8 - 1), 0)),
                  pl.BlockSpec((1, d), const),
                  pl.BlockSpec((1, 1, d), vec),
                  pl.BlockSpec((1, 1, d), vec),
                  pl.BlockSpec((tm, LANES), lambda bi, i: (i, 0)),
                  pl.BlockSpec((tm, LANES), lambda bi, i: (i, 0)),
                  pl.BlockSpec((d, IN_COLS_PAD), const),
                  pl.BlockSpec((8, GDN_QKV), const),
                  pl.BlockSpec((1, LANES), const),
                  pl.BlockSpec((1, LANES), const)],
        out_specs=out_specs,
        out_shape=out_shape,
        compiler_params=_cparams(("parallel", "parallel")),
        name="inproj",
    )(x, x, x, norm_g, shift, scale, cos, sin, w_bf, conv_w8, alog_row, dtb_row)


def _attn_kernel(q_ref, k_ref, vt_ref, lam_ref, g_ref, o_ref,
                 q2_ref, acc_ref, m_ref, s_ref, p_ref, a_ref, *, tk):
    q = q_ref[0]
    tq = q.shape[0]
    lane = lax.broadcasted_iota(jnp.int32, (tq, LANES), 1)
    zero = jnp.zeros_like(q)
    q2_ref[0:tq, :] = jnp.where(lane < DA_D, q, zero)
    q2_ref[tq:2 * tq, :] = jnp.where(lane >= DA_D, q, zero)
    m_ref[...] = jnp.full(m_ref.shape, NEG_BIG, F32)
    acc_ref[...] = jnp.zeros(acc_ref.shape, F32)

    nct = 2 * tq // ATTN_COL_TILE
    col = lambda ct: slice(ct * ATTN_COL_TILE, (ct + 1) * ATTN_COL_TILE)

    def scores(kk, ct):
        return _dot_nt(kk, q2_ref[col(ct), :])

    def softmax_update(ct, s):
        cols = col(ct)
        m_old = m_ref[:, cols]
        m_new = jnp.maximum(m_old, jnp.max(s, axis=0, keepdims=True))
        alpha = jnp.exp2(m_old - m_new)
        p = jnp.exp2(s - m_new)
        m_ref[:, cols] = m_new
        return p.astype(BF16), alpha

    def value_update(ct, vt, p, alpha):
        cols = col(ct)
        acc_ref[:, cols] = alpha * acc_ref[:, cols] + _dot(vt, p)

    nchunk = k_ref.shape[1] // tk
    last = nct - 1
    s_ref[...] = scores(k_ref[0, 0:tk, :], 0)
    p_ref[...] = jnp.zeros(p_ref.shape, BF16)
    a_ref[...] = jnp.ones(a_ref.shape, F32)

    def body(j, carry):
        kk = k_ref[0, pl.ds(pl.multiple_of(j * tk, tk), tk), :]
        vt = vt_ref[0, 0, j]
        s_cur = s_ref[...]
        pending = (last, vt_ref[0, 0, jnp.maximum(j - 1, 0)], p_ref[...], a_ref[...])
        for ct in range(nct):
            if ct < last:
                s_ahead = scores(kk, ct + 1)
            else:
                jn = jnp.minimum(j + 1, nchunk - 1)
                s_ahead = scores(k_ref[0, pl.ds(pl.multiple_of(jn * tk, tk), tk), :], 0)
            p, alpha = softmax_update(ct, s_cur)
            value_update(*pending)
            pending = (ct, vt, p, alpha)
            s_cur = s_ahead
        s_ref[...] = s_cur
        p_ref[...] = pending[2]
        a_ref[...] = pending[3]
        return carry

    lax.fori_loop(0, nchunk, body, 0)
    value_update(last, vt_ref[0, 0, nchunk - 1], p_ref[...], a_ref[...])

    lv = lam_ref[...]
    lam = (jnp.exp(jnp.sum(lv[0:1] * lv[1:2], axis=1, keepdims=True))
           - jnp.exp(jnp.sum(lv[2:3] * lv[3:4], axis=1, keepdims=True)) + LAM_INIT)
    den = acc_ref[DA_DV:DA_DV + 1, :]
    ot = (acc_ref[0:DA_DV, 0:tq] / den[:, 0:tq]
          - lam * (acc_ref[0:DA_DV, tq:2 * tq] / den[:, tq:2 * tq]))
    o = ot.T
    ms = jnp.mean(o * o, axis=-1, keepdims=True)
    o_ref[0] = ((o * lax.rsqrt(ms + EPS) * g_ref[...]) * (1.0 - LAM_INIT)).astype(o_ref.dtype)


def _attn_call(q, k_all, vt_all, lam_vecs, subln_g, tq):
    b, n, _ = q.shape
    nk = k_all.shape[1]
    _, _, nchunk, _, tk = vt_all.shape
    return pl.pallas_call(
        functools.partial(_attn_kernel, tk=tk),
        grid=(b, DA_HEADS, n // tq),
        in_specs=[pl.BlockSpec((1, tq, LANES), lambda bi, h, i: (bi, i, h)),
                  pl.BlockSpec((1, nk, LANES), lambda bi, h, i: (bi, 0, h)),
                  pl.BlockSpec((1, 1, nchunk, VT_ROWS, tk), lambda bi, h, i: (bi, h, 0, 0, 0)),
                  pl.BlockSpec((4, DA_D), lambda bi, h, i: (0, 0)),
                  pl.BlockSpec((1, DA_DV), lambda bi, h, i: (0, 0))],
        out_specs=pl.BlockSpec((1, tq, LANES), lambda bi, h, i: (bi, i, h)),
        out_shape=jax.ShapeDtypeStruct((b, n, DA_V), BF16),
        scratch_shapes=[pltpu.VMEM((2 * tq, LANES), BF16), pltpu.VMEM((VT_ROWS, 2 * tq), F32),
                        pltpu.VMEM((1, 2 * tq), F32),
                        pltpu.VMEM((tk, ATTN_COL_TILE), F32), pltpu.VMEM((tk, ATTN_COL_TILE), BF16),
                        pltpu.VMEM((1, ATTN_COL_TILE), F32)],
        compiler_params=_cparams(("parallel", "parallel", "parallel")),
        name="attn",
    )(q, k_all, vt_all, lam_vecs, subln_g)


def _solve_unit_triangular(a_list, rhs_list, bd_mask):
    def mm(xs, ys):
        return [_dot(x.astype(BF16), y.astype(BF16)) for x, y in zip(xs, ys)]

    d = [jnp.where(bd_mask, a, 0.0) for a in a_list]
    n = [a - di for a, di in zip(a_list, d)]
    d2 = mm(d, d)
    dd2 = mm(d, d2)
    p = [(-di + d2i) - t for di, d2i, t in zip(d, d2, dd2)]
    d4 = mm(d2, d2)
    pd4 = mm(p, d4)
    p = [pi + d4i + t for pi, d4i, t in zip(p, d4, pd4)]
    d8 = mm(d4, d4)
    pd8 = mm(p, d8)
    p = [pi + d8i + t for pi, d8i, t in zip(p, d8, pd8)]
    m = [ni + t for ni, t in zip(n, mm(p, n))]
    z = [ri + t for ri, t in zip(rhs_list, mm(p, rhs_list))]
    m2 = mm(m, m)
    z = [zi + t for zi, t in zip(z, mm(m2, z))]
    return [zi - t for zi, t in zip(z, mm(m, z))]


def _gdn_scan_kernel(qf_ref, kf_ref, vf_ref, gbf_ref, gbtf_ref, qb_ref, kb_ref, vb_ref, gbb_ref, gbtb_ref,
                     s0_ref, of_ref, ob_ref, sfin_ref, s_ref):
    i = pl.program_id(1)

    @pl.when(i == 0)
    def _():
        s_ref[...] = s0_ref[0]

    c64 = CHUNK
    nchunk = qf_ref.shape[1] // c64
    ri = lax.broadcasted_iota(jnp.int32, (c64, c64), 0)
    ci = lax.broadcasted_iota(jnp.int32, (c64, c64), 1)
    bd_mask = (ri // 16) == (ci // 16)
    lower_incl, upper_incl = ri >= ci, ri <= ci
    heads = range(GDN_HEADS)
    rows = [slice(c * c64, (c + 1) * c64) for c in range(nchunk)]
    cols = [slice(h * GDN_DK, (h + 1) * GDN_DK) for h in heads]

    dirs = []
    for d, refs in enumerate(((qf_ref, kf_ref, vf_ref, gbf_ref, gbtf_ref, of_ref),
                              (qb_ref, kb_ref, vb_ref, gbb_ref, gbtb_ref, ob_ref))):
        rev = d == 1
        incl = upper_incl if rev else lower_incl
        dirs.append(dict(
            refs=refs, incl=incl, strict=(ri < ci) if rev else (ri > ci),
            tri_c=jnp.where(incl, 1.0, 0.0).astype(BF16),
            tri_r=jnp.where(lower_incl if rev else upper_incl, 1.0, 0.0).astype(BF16),
            last=0 if rev else c64 - 1,
            order=list(range(nchunk - 1, -1, -1) if rev else range(nchunk))))

    gall, gc_col, gc_row = {}, {}, {}
    for d, dr in enumerate(dirs):
        gb_ref, gbt_ref = dr["refs"][3], dr["refs"][4]
        for c in range(nchunk):
            gall[d, c] = gb_ref[0, rows[c], :]
            g1, g2, g3 = _split3(gall[d, c])
            tc, tr = dr["tri_c"], dr["tri_r"]
            gc_col[d, c] = _dot(tc, g1) + _dot(tc, g2) + _dot(tc, g3)
            t1, t2, t3 = _split3(gbt_ref[0, :, rows[c]])
            gc_row[d, c] = _dot(t1, tr) + _dot(t2, tr) + _dot(t3, tr)

    probs = [(d, c, h) for d in range(2) for c in range(nchunk) for h in heads]
    ln = lambda d, h: d * GDN_HEADS + h
    k = {(d, c, h): dirs[d]["refs"][1][0, rows[c], cols[h]] for d, c, h in probs}
    q = {(d, c, h): dirs[d]["refs"][0][0, rows[c], cols[h]] for d, c, h in probs}
    kbf = {p: k[p].astype(BF16) for p in probs}
    kk = {p: _dot_nt(kbf[p], kbf[p]) for p in probs}
    qk = {p: _dot_nt(q[p].astype(BF16), kbf[p]) for p in probs}
    a_list, rhs_list = [], []
    kd, qd, aqk, glast = {}, {}, {}, {}
    for p in probs:
        d, c, h = p
        dr = dirs[d]
        lane = ln(d, h)
        gcol = gc_col[d, c][:, lane:lane + 1]
        bcol = gall[d, c][:, GDN_AB // 2 + lane:GDN_AB // 2 + lane + 1]
        diff = gcol - gc_row[d, c][lane:lane + 1, :]
        e_strict = jnp.where(dr["strict"], jnp.exp(jnp.where(dr["strict"], diff, 0.0)), 0.0)
        e_incl = jnp.where(dr["incl"], jnp.exp(jnp.where(dr["incl"], diff, 0.0)), 0.0)
        eg = jnp.exp(gcol)
        glast[p] = gcol[dr["last"]:dr["last"] + 1, :]
        a_list.append(bcol * kk[p] * e_strict)
        v = dr["refs"][2][0, rows[c], cols[h]]
        rhs_list.append(jnp.concatenate([(bcol * eg) * k[p], bcol * v], axis=1))
        kd[p] = (k[p] * jnp.exp(glast[p] - gcol)).astype(BF16)
        qd[p] = (q[p] * eg).astype(BF16)
        aqk[p] = (qk[p] * e_incl).astype(BF16)
    sol = dict(zip(probs, _solve_unit_triangular(a_list, rhs_list, bd_mask)))

    chains = [(d, h) for d in range(2) for h in heads]
    for t in range(nchunk):
        cur = {(d, h): (d, dirs[d]["order"][t], h) for d, h in chains}
        s = {dh: s_ref[dh[0], dh[1]] for dh in chains}
        sb = {dh: s[dh].astype(BF16) for dh in chains}
        ws = {dh: _dot(sol[cur[dh]][:, :GDN_DK].astype(BF16), sb[dh]) for dh in chains}
        qs = {dh: _dot(qd[cur[dh]], sb[dh]) for dh in chains}
        u = {dh: (sol[cur[dh]][:, GDN_DK:] - ws[dh]).astype(BF16) for dh in chains}
        au = {dh: _dot(aqk[cur[dh]], u[dh]) for dh in chains}
        ku = {dh: _dot_tn(kd[cur[dh]], u[dh]) for dh in chains}
        for dh in chains:
            d, c, h = cur[dh]
            s_ref[d, h] = jnp.exp(glast[cur[dh]]) * s[dh] + ku[dh]
            dirs[d]["refs"][5][0, rows[c], cols[h]] = qs[dh] + au[dh]

    @pl.when(i == pl.num_programs(1) - 1)
    def _():
        sfin_ref[0] = s_ref[...]


def _gdn_scan_call(q, k, v, gb, gbt, s0, sc):
    b, n, _ = q.shape
    nsup = n // sc
    fwd = lambda bi, i: (bi, i, 0)
    bwd = lambda bi, i: (bi, nsup - 1 - i, 0)
    st = lambda bi, i: (bi, 0, 0, 0, 0)
    state = (1, 2, GDN_HEADS, GDN_DK, GDN_DV)

    def in_specs(row, tr):
        return [pl.BlockSpec((1, sc, GDN_QK), row), pl.BlockSpec((1, sc, GDN_QK), row),
                pl.BlockSpec((1, sc, GDN_V), row), pl.BlockSpec((1, sc, LANES), row),
                pl.BlockSpec((1, GDN_AB, sc), tr)]

    return pl.pallas_call(
        _gdn_scan_kernel,
        grid=(b, nsup),
        in_specs=(in_specs(fwd, lambda bi, i: (bi, 0, i))
                  + in_specs(bwd, lambda bi, i: (bi, 0, nsup - 1 - i))
                  + [pl.BlockSpec(state, st)]),
        out_specs=[pl.BlockSpec((1, sc, GDN_V), fwd), pl.BlockSpec((1, sc, GDN_V), bwd),
                   pl.BlockSpec(state, st)],
        out_shape=[jax.ShapeDtypeStruct((b, n, GDN_V), F32), jax.ShapeDtypeStruct((b, n, GDN_V), F32),
                   jax.ShapeDtypeStruct((b,) + state[1:], F32)],
        scratch_shapes=[pltpu.VMEM(state[1:], F32)],
        compiler_params=_cparams(("parallel", "arbitrary")),
        name="gdn_scan",
    )(q, k, v, gb, gbt, q, k, v, gb, gbt, s0)


def _outproj_kernel(oda_ref, of_ref, ob_ref, gate_ref, x_ref, gt1_ref, gng_ref, wout_ref,
                    n2g_ref, sh2_ref, sc2_ref, wr_ref, xnew_ref, h_ref, aff_ref, afft_ref):
    og = of_ref[0] + ob_ref[0]
    gate = gate_ref[0]
    ys = []
    for h in range(GDN_HEADS):
        cols = slice(h * GDN_DV, (h + 1) * GDN_DV)
        t = og[:, cols]
        y = t * lax.rsqrt(jnp.mean(t * t, axis=-1, keepdims=True) + EPS) * gng_ref[...]
        ys.append((y * _silu(gate[:, cols])).astype(BF16))
    y_gdn = jnp.concatenate(ys, axis=1)
    proj = _dot(oda_ref[0], wout_ref[0:DA_V, :]) + _dot(y_gdn, wout_ref[DA_V:, :])
    xn = x_ref[0] + gt1_ref[0] * proj
    xnew_ref[0] = xn
    ms = jnp.mean(xn * xn, axis=-1, keepdims=True)
    hm = (xn * lax.rsqrt(ms + EPS) * n2g_ref[...]) * (1.0 + sc2_ref[0]) + sh2_ref[0]
    h_ref[0] = hm.astype(BF16)
    logits = _dot_x3(hm, wr_ref[...])
    lane = lax.broadcasted_iota(jnp.int32, logits.shape, 1)
    logits = jnp.where(lane < N_EXPERTS, logits, NEG_BIG)
    e = jnp.exp(logits - jnp.max(logits, axis=-1, keepdims=True))
    aff = e / jnp.sum(e, axis=-1, keepdims=True)
    aff_ref[0] = aff
    afft_ref[0] = aff.T[:N_EXPERTS, :]


def _outproj_call(o_da, o_f, o_b, gate, x, gt1, gdn_norm_g, w_out_bf, norm2_g, sh2, sc2, wr_pad, tm):
    b, n, d = x.shape
    row = lambda bi, i: (bi, i, 0)
    vec = lambda bi, i: (bi, 0, 0)
    const2 = lambda bi, i: (0, 0)
    return pl.pallas_call(
        _outproj_kernel,
        grid=(b, n // tm),
        in_specs=[pl.BlockSpec((1, tm, DA_V), row), pl.BlockSpec((1, tm, GDN_V), row),
                  pl.BlockSpec((1, tm, GDN_V), row), pl.BlockSpec((1, tm, GDN_V), row),
                  pl.BlockSpec((1, tm, d), row), pl.BlockSpec((1, 1, d), vec),
                  pl.BlockSpec((1, GDN_DV), const2), pl.BlockSpec((DA_V + GDN_V, d), const2),
                  pl.BlockSpec((1, d), const2), pl.BlockSpec((1, 1, d), vec), pl.BlockSpec((1, 1, d), vec),
                  pl.BlockSpec((d, LANES), const2)],
        out_specs=[pl.BlockSpec((1, tm, d), row), pl.BlockSpec((1, tm, d), row),
                   pl.BlockSpec((1, tm, LANES), row),
                   pl.BlockSpec((1, N_EXPERTS, tm), lambda bi, i: (bi, 0, i))],
        out_shape=[jax.ShapeDtypeStruct((b, n, d), F32), jax.ShapeDtypeStruct((b, n, d), BF16),
                   jax.ShapeDtypeStruct((b, n, LANES), F32),
                   jax.ShapeDtypeStruct((b, N_EXPERTS, n), F32)],
        compiler_params=_cparams(("parallel", "parallel")),
        name="outproj",
    )(o_da, o_f, o_b, gate, x, gt1, gdn_norm_g, w_out_bf, norm2_g, sh2, sc2, wr_pad)


PREFIX_BLK = 256
BISECT_STEPS = 64


def _select_kernel(afft_ref, aff_ref, post_ref, pos_ref, st_ref, rng_ref, *, cap, tb, tj):
    at = afft_ref[0]
    n = at.shape[1]

    def count_ge(t):
        return jnp.sum(jnp.where(at >= t, 1.0, 0.0), axis=1, keepdims=True)

    def bisect(_, bounds):
        lo, hi = bounds
        mid = 0.5 * (lo + hi)
        ok = count_ge(mid) >= cap
        return jnp.where(ok, mid, lo), jnp.where(ok, hi, mid)

    _, hi = lax.fori_loop(0, BISECT_STEPS, bisect,
                          (jnp.zeros((N_EXPERTS, 1), F32), jnp.full((N_EXPERTS, 1), 2.0, F32)))

    def below(h):
        return jnp.max(jnp.where(at < h, at, -1.0), axis=1, keepdims=True)

    def short(state):
        return jnp.sum(jnp.where(count_ge(state[0]) < cap, 1.0, 0.0)) > 0.0

    def step_down(state):
        t, h = state
        lacking = count_ge(t) < cap
        h = jnp.where(lacking, t, h)
        return jnp.where(lacking, below(h), t), h

    thr, _ = lax.while_loop(short, step_down, (below(hi), hi))
    need = cap - jnp.sum(jnp.where(at > thr, 1.0, 0.0), axis=1, keepdims=True)

    pi = lax.broadcasted_iota(jnp.int32, (PREFIX_BLK, PREFIX_BLK), 0)
    pj = lax.broadcasted_iota(jnp.int32, (PREFIX_BLK, PREFIX_BLK), 1)
    upper = jnp.where(pi <= pj, 1.0, 0.0).astype(BF16)
    lower = jnp.where(pi >= pj, 1.0, 0.0).astype(BF16)

    def prefix_lanes(m):
        carry = jnp.zeros((m.shape[0], 1), F32)
        outs = []
        for c in range(m.shape[1] // PREFIX_BLK):
            mc = m[:, c * PREFIX_BLK:(c + 1) * PREFIX_BLK]
            loc = _dot(mc.astype(BF16), upper)
            outs.append(loc - mc + carry)
            carry = carry + loc[:, PREFIX_BLK - 1:PREFIX_BLK]
        return jnp.concatenate(outs, axis=1)

    def prefix_rows(m):
        carry = jnp.zeros((1, m.shape[1]), F32)
        outs = []
        for c in range(m.shape[0] // PREFIX_BLK):
            mc = m[c * PREFIX_BLK:(c + 1) * PREFIX_BLK, :]
            loc = _dot(lower, mc.astype(BF16))
            outs.append(loc - mc + carry)
            carry = carry + loc[PREFIX_BLK - 1:PREFIX_BLK, :]
        return jnp.concatenate(outs, axis=0)

    eq_t = jnp.where(at == thr, 1.0, 0.0)
    sel_t = jnp.where(at > thr, 1.0, jnp.where(prefix_lanes(eq_t) < need, eq_t, 0.0))
    post_ref[0] = jnp.where(sel_t > 0.0, prefix_lanes(sel_t), -1.0)

    ti = lax.broadcasted_iota(jnp.int32, (n, LANES), 0)
    bi = lax.broadcasted_iota(jnp.int32, (n, LANES), 1)
    before = jnp.where(ti < bi * tb, 1.0, 0.0).astype(BF16)
    st = _dot(sel_t.astype(BF16), before)
    st_ref[0] = st.astype(jnp.int32)

    through = jnp.where(ti < (bi + 1) * tb, 1.0, 0.0).astype(BF16)
    st_end = _dot(sel_t.astype(BF16), through)
    blk_lane = lax.broadcasted_iota(jnp.int32, (N_EXPERTS, LANES), 1)
    is_blk = blk_lane < n // tb
    rng = jnp.zeros((N_EXPERTS, LANES), F32)
    ntile = cap // tj
    for j in range(ntile):
        first = jnp.sum(jnp.where(is_blk & (st_end <= j * tj), 1.0, 0.0), axis=1, keepdims=True)
        last = jnp.sum(jnp.where(is_blk & (st < (j + 1) * tj), 1.0, 0.0), axis=1, keepdims=True) - 1.0
        rng = rng + jnp.where(blk_lane == j, first, 0.0) + jnp.where(blk_lane == ntile + j, last, 0.0)
    rng_ref[0] = rng.astype(jnp.int32)

    a = aff_ref[0]
    er = lax.broadcasted_iota(jnp.int32, (N_EXPERTS, LANES), 0)
    ec = lax.broadcasted_iota(jnp.int32, (N_EXPERTS, LANES), 1)
    diag = er == ec
    thr_row = jnp.sum(jnp.where(diag, thr, 0.0), axis=0, keepdims=True)
    need_row = jnp.sum(jnp.where(diag, need, 0.0), axis=0, keepdims=True)
    valid = lax.broadcasted_iota(jnp.int32, a.shape, 1) < N_EXPERTS
    eq = jnp.where(valid & (a == thr_row), 1.0, 0.0)
    sel = jnp.where(valid & (a > thr_row), 1.0, jnp.where(prefix_rows(eq) < need_row, eq, 0.0))
    pos_ref[0] = jnp.where(sel > 0.0, prefix_rows(sel), -1.0)


def _select_call(afft, aff, cap, tb, tj):
    b, e, n = afft.shape
    assert n // tb + 1 <= LANES and 2 * (cap // tj) <= LANES
    return pl.pallas_call(
        functools.partial(_select_kernel, cap=cap, tb=tb, tj=tj),
        grid=(b,),
        in_specs=[pl.BlockSpec((1, e, n), lambda bi: (bi, 0, 0)),
                  pl.BlockSpec((1, n, LANES), lambda bi: (bi, 0, 0))],
        out_specs=[pl.BlockSpec((1, e, n), lambda bi: (bi, 0, 0)),
                   pl.BlockSpec((1, n, LANES), lambda bi: (bi, 0, 0)),
                   pl.BlockSpec((1, e, LANES), lambda bi: (bi, 0, 0)),
                   pl.BlockSpec((1, e, LANES), lambda bi: (bi, 0, 0))],
        out_shape=[jax.ShapeDtypeStruct((b, e, n), F32), jax.ShapeDtypeStruct((b, n, LANES), F32),
                   jax.ShapeDtypeStruct((b, e, LANES), jnp.int32),
                   jax.ShapeDtypeStruct((b, e, LANES), jnp.int32)],
        compiler_params=_cparams(("parallel",)),
        name="select",
    )(afft, aff)


def _moe_ffn_kernel(rng_ref, h_ref, post_ref, afft_ref, wg_ref, wu_ref, wd_ref, ye_ref,
                    acc_ref, gacc_ref, wgb_ref, wub_ref, wdb_ref, *, tj, tb):
    b = pl.program_id(0)
    e = pl.program_id(1)
    cap = ye_ref.shape[2]
    ntile = cap // tj
    slot = lax.broadcasted_iota(jnp.int32, (tj, tb), 0).astype(F32)
    wgb_ref[...] = wg_ref[0].astype(BF16)
    wub_ref[...] = wu_ref[0].astype(BF16)
    wdb_ref[...] = wd_ref[0].astype(BF16)

    for j in range(cap // tj):
        lo = j * tj
        acc_ref[...] = jnp.zeros(acc_ref.shape, F32)
        gacc_ref[...] = jnp.zeros(gacc_ref.shape, F32)

        def blk_body(bi, carry):
            prow = post_ref[0, 0, pl.ds(bi, 1), :]
            hit = prow == (slot + float(lo))
            onehot = jnp.where(hit, 1.0, 0.0).astype(BF16)
            off = pl.multiple_of(bi * tb, tb)
            acc_ref[...] += _dot(onehot, h_ref[0, pl.ds(off, tb), :])
            arow = afft_ref[0, 0, pl.ds(bi, 1), :]
            gacc_ref[...] += jnp.sum(jnp.where(hit, arow, 0.0), axis=1, keepdims=True)
            return carry

        lax.fori_loop(rng_ref[b, e, j], rng_ref[b, e, ntile + j] + 1, blk_body, 0)
        xe = acc_ref[...].astype(BF16)
        hid = _silu(_dot(xe, wgb_ref[...])) * _dot(xe, wub_ref[...])
        ye = _dot(hid.astype(BF16), wdb_ref[...]) * gacc_ref[...]
        ye_ref[0, 0, lo:lo + tj, :] = ye.astype(ye_ref.dtype)


def _moe_ffn_call(starts, h, post4, afft4, wg, wu, wd, cap, tj, tb):
    b, n, d = h.shape
    e, _, f = wg.shape
    nblk = n // tb
    wspec = lambda shp: pl.BlockSpec((1,) + shp, lambda bi, ei: (ei, 0, 0))
    return pl.pallas_call(
        functools.partial(_moe_ffn_kernel, tj=tj, tb=tb),
        grid=(b, e),
        in_specs=[pl.BlockSpec(memory_space=pltpu.SMEM),
                  pl.BlockSpec((1, n, d), lambda bi, ei: (bi, 0, 0), pipeline_mode=pl.Buffered(1)),
                  pl.BlockSpec((1, 1, nblk, tb), lambda bi, ei: (bi, ei, 0, 0)),
                  pl.BlockSpec((1, 1, nblk, tb), lambda bi, ei: (bi, ei, 0, 0)),
                  wspec((d, f)), wspec((d, f)), wspec((f, d))],
        out_specs=pl.BlockSpec((1, 1, cap, d), lambda bi, ei: (bi, ei, 0, 0)),
        out_shape=jax.ShapeDtypeStruct((b, e, cap, d), BF16),
        scratch_shapes=[pltpu.VMEM((tj, d), F32), pltpu.VMEM((tj, 1), F32),
                        pltpu.VMEM((d, f), BF16), pltpu.VMEM((d, f), BF16), pltpu.VMEM((f, d), BF16)],
        compiler_params=_cparams(("parallel", "arbitrary")),
        name="moe_ffn",
    )(starts, h, post4, afft4, wg, wu, wd)


def _combine_kernel(st_ref, pos_ref, ye_ref, x_ref, gt2_ref, g_ref, o_ref, *, tb):
    b = pl.program_id(0)
    blk = pl.program_id(1)
    cap = ye_ref.shape[2]
    win = 2 * tb
    slot = lax.broadcasted_iota(jnp.int32, (tb, win), 1).astype(F32)
    pos = pos_ref[0]
    moe = jnp.zeros((tb, ye_ref.shape[3]), F32)
    for e in range(N_EXPERTS):
        base = jnp.minimum(st_ref[b, e, blk] // tb, cap // tb - 2) * tb
        onehot = jnp.where(pos[:, e:e + 1] == slot + base.astype(F32), 1.0, 0.0).astype(BF16)
        moe = moe + _dot(onehot, ye_ref[0, e, pl.ds(pl.multiple_of(base, tb), win), :])
    y = x_ref[0] + gt2_ref[0] * moe
    ms = jnp.mean(y * y, axis=-1, keepdims=True)
    o_ref[0] = y * lax.rsqrt(ms + EPS) * g_ref[...]


def _combine_call(starts, pos, ye, x_new, gt2, final_g, tb):
    b, e, cap, d = ye.shape
    n = pos.shape[1]
    assert cap % tb == 0 and cap // tb >= 2
    row = lambda bi, i: (bi, i, 0)
    return pl.pallas_call(
        functools.partial(_combine_kernel, tb=tb),
        grid=(b, n // tb),
        in_specs=[pl.BlockSpec(memory_space=pltpu.SMEM),
                  pl.BlockSpec((1, tb, LANES), row),
                  pl.BlockSpec((1, e, cap, d), lambda bi, i: (bi, 0, 0, 0), pipeline_mode=pl.Buffered(1)),
                  pl.BlockSpec((1, tb, d), row),
                  pl.BlockSpec((1, 1, d), lambda bi, i: (bi, 0, 0)),
                  pl.BlockSpec((1, d), lambda bi, i: (0, 0))],
        out_specs=pl.BlockSpec((1, tb, d), row),
        out_shape=jax.ShapeDtypeStruct((b, n, d), F32),
        compiler_params=_cparams(("parallel", "arbitrary")),
        name="combine",
    )(starts, pos, ye, x_new, gt2, final_g)


def _rope_tables(n):
    t = np.arange(n)
    rows = (t // GRID_W).astype(np.float64)
    cols = (t % GRID_W).astype(np.float64)
    inv_freq = np.power(ROPE_THETA, -np.arange(0, ROPE_AXIS_DIM, 2, dtype=np.float64) / ROPE_AXIS_DIM)
    ang_row = rows[:, None] * inv_freq[None, :]
    ang_col = cols[:, None] * inv_freq[None, :]

    def axis_tables(ang):
        c = np.cos(ang).astype(np.float32)
        s = np.sin(ang).astype(np.float32)
        return np.concatenate([c, c], axis=1), np.concatenate([-s, s], axis=1)

    cr, sr = axis_tables(ang_row)
    cc, sc = axis_tables(ang_col)
    cos64 = np.concatenate([cr, cc], axis=1)
    sin64 = np.concatenate([sr, sc], axis=1)
    return (jnp.asarray(np.concatenate([cos64, cos64], axis=1)),
            jnp.asarray(np.concatenate([sin64, sin64], axis=1)))


def _pad_lanes(v):
    return jnp.pad(v.reshape(1, -1), ((0, 0), (0, LANES - v.size)))


def kernel(x, c, ctx, c_ctx, w_mod, b_mod, norm1_g, w_in, conv_w, a_log, dt_bias, gdn_norm_g,
           lam_q1, lam_k1, lam_q2, lam_k2, da_subln_g, w_out, norm2_g,
           w_router, w_gate, w_up, w_down, final_g):
    b, n, d = x.shape
    nc = ctx.shape[1]
    layer = 0

    cvec = jnp.concatenate([c, c_ctx[None, :], jnp.zeros((8 - b - 1, d), F32)], axis=0)
    mod = _mod_call(cvec, w_mod[layer], b_mod[layer])
    sh1, sc1, gt1, sh2, sc2, gt2 = [mod[:b, i * d:(i + 1) * d].reshape(b, 1, d) for i in range(6)]
    sh1c = jnp.broadcast_to(mod[b:b + 1, 0:d].reshape(1, 1, d), (b, 1, d))
    sc1c = jnp.broadcast_to(mod[b:b + 1, d:2 * d].reshape(1, 1, d), (b, 1, d))

    w_in_bf = jnp.pad(w_in[layer].astype(BF16), ((0, 0), (0, IN_COLS_PAD - w_in.shape[2])))
    g1 = norm1_g[layer].reshape(1, d)
    cos_l, sin_l = _rope_tables(n)
    cos_c, sin_c = jnp.ones((nc, LANES), F32), jnp.zeros((nc, LANES), F32)
    conv_w8 = jnp.pad(conv_w[layer], ((0, 8 - conv_w.shape[1]), (0, 0)))
    alog_row = _pad_lanes(a_log[layer])
    dtb_row = _pad_lanes(dt_bias[layer])
    gdn_args = (conv_w8, alog_row, dtb_row)
    q, k, vt, ql, kl, vl, gate, gbl, gbtl = _inproj_call(
        x, g1, sh1, sc1, cos_l, sin_l, w_in_bf, *gdn_args, tm=512)
    _, kc, vct, qc, kcg, vcg, _, gbc, gbtc = _inproj_call(
        ctx, g1, sh1c, sc1c, cos_c, sin_c, w_in_bf, *gdn_args, tm=nc)

    lam_vecs = jnp.stack([lam_q1[layer], lam_k1[layer], lam_q2[layer], lam_k2[layer]], axis=0)
    tk = vt.shape[-1]
    reps = tk // nc
    k_all = jnp.concatenate([kc] * reps + [k], axis=1)
    vct_pad = jnp.pad(vct, ((0, 0), (0, 0), (0, 0), (0, 0), (0, tk - nc)))
    vt_all = jnp.concatenate([vct_pad, vt], axis=2)
    o_da = _attn_call(q, k_all, vt_all, lam_vecs, da_subln_g[layer].reshape(1, DA_DV), tq=2048)

    zeros_state = jnp.zeros((b, 2, GDN_HEADS, GDN_DK, GDN_DV), F32)
    _, _, s_ctx = _gdn_scan_call(qc, kcg, vcg, gbc, gbtc, zeros_state, sc=nc)
    o_fwd, o_bwd, _ = _gdn_scan_call(ql, kl, vl, gbl, gbtl, s_ctx, sc=256)
    o_dirs = [o_fwd, o_bwd]

    wr_pad = jnp.pad(w_router[layer], ((0, 0), (0, LANES - N_EXPERTS)))
    x_new, h, aff, afft = _outproj_call(
        o_da, o_dirs[0], o_dirs[1], gate, x, gt1, gdn_norm_g[layer].reshape(1, GDN_DV),
        w_out[layer].astype(BF16), norm2_g[layer].reshape(1, d), sh2, sc2, wr_pad, tm=512)

    cap = CAP_FACTOR * n // N_EXPERTS
    tb = 256
    tj = 256
    post, pos, starts, tile_rng = _select_call(afft, aff, cap, tb, tj)
    nblk = n // tb
    ye = _moe_ffn_call(tile_rng, h, post.reshape(b, N_EXPERTS, nblk, tb),
                       afft.reshape(b, N_EXPERTS, nblk, tb),
                       w_gate[layer], w_up[layer], w_down[layer], cap, tj, tb)
    return _combine_call(starts, pos, ye, x_new, gt2, final_g.reshape(1, d), tb)
```

```python
import functools
import math

import jax
import jax.numpy as jnp
import numpy as np
from jax import lax
from jax.experimental import pallas as pl
from jax.experimental.pallas import tpu as pltpu

F32 = jnp.float32
BF16 = jnp.bfloat16

D_MODEL = 1024
GRID_W = 64
EPS = 1e-6
DA_HEADS = 4
DA_D = 64
DA_DV = 2 * DA_D
ROPE_AXIS_DIM = DA_D // 2
ROPE_THETA = 10000.0
GDN_HEADS = 4
GDN_DK = 128
GDN_DV = 128
CHUNK = 64
N_EXPERTS = 16
CAP_FACTOR = 2
LAM_INIT = 0.8 - 0.6 * math.exp(-0.3 * 0)

DA_QK = DA_HEADS * 2 * DA_D
DA_V = DA_HEADS * DA_DV
GDN_QK = GDN_HEADS * GDN_DK
GDN_V = GDN_HEADS * GDN_DV
GDN_QKV = 2 * GDN_QK + GDN_V
GDN_AB = 2 * 2 * GDN_HEADS
COL_Q, COL_K, COL_V = 0, DA_QK, 2 * DA_QK
COL_QKV = 2 * DA_QK + DA_V
COL_GATE = COL_QKV + GDN_QKV
COL_AB = COL_GATE + GDN_V
LANES = 128
IN_COLS_PAD = COL_AB + LANES
MXU_WIDTH = 256
ATTN_COL_TILE = MXU_WIDTH
VT_ROWS = DA_DV + 8

VMEM_LIMIT = 56 * 1024 * 1024
NEG_BIG = -1e30
LOG2_E = math.log2(math.e)


def _cparams(sem):
    return pltpu.CompilerParams(dimension_semantics=sem, vmem_limit_bytes=VMEM_LIMIT)


def _sigmoid(x):
    return 1.0 / (1.0 + jnp.exp(-x))


def _silu(x):
    return x * _sigmoid(x)


def _split3(a):
    a1 = a.astype(BF16)
    r1 = a - a1.astype(F32)
    a2 = r1.astype(BF16)
    a3 = (r1 - a2.astype(F32)).astype(BF16)
    return a1, a2, a3


def _dot(a, b):
    return jnp.dot(a, b, preferred_element_type=F32)


def _dot_x3(a, b):
    a1, a2, _ = _split3(a)
    b1, b2, _ = _split3(b)
    return _dot(a1, b1) + (_dot(a1, b2) + _dot(a2, b1))


def _dot_nt(a, b):
    return lax.dot_general(a, b, (((1,), (1,)), ((), ())), preferred_element_type=F32)


def _dot_tn(a, b):
    return lax.dot_general(a, b, (((0,), (0,)), ((), ())), preferred_element_type=F32)


def _mod_kernel(c_ref, w_ref, b_ref, o_ref):
    s = _silu(c_ref[...])
    o_ref[...] = jnp.dot(s, w_ref[...], precision=lax.Precision.HIGHEST,
                         preferred_element_type=F32) + b_ref[...]


def _mod_call(cvec, w_mod, b_mod):
    d, n = w_mod.shape
    tn = 1024
    return pl.pallas_call(
        _mod_kernel,
        grid=(n // tn,),
        in_specs=[pl.BlockSpec((8, d), lambda j: (0, 0)),
                  pl.BlockSpec((d, tn), lambda j: (0, j)),
                  pl.BlockSpec((1, tn), lambda j: (0, j))],
        out_specs=pl.BlockSpec((8, tn), lambda j: (0, j)),
        out_shape=jax.ShapeDtypeStruct((8, n), F32),
        compiler_params=_cparams(("arbitrary",)),
        name="mod",
    )(cvec, w_mod, b_mod.reshape(1, n))


def _inproj_kernel(x_ref, xp_ref, xn_ref, g_ref, sh_ref, sc_ref, cos_ref, sin_ref, w_ref,
                   cw_ref, al_ref, dtb_ref,
                   q_ref, k_ref, vt_ref, gq_ref, gk_ref, gv_ref, gate_ref, gb_ref, gbt_ref):
    i = pl.program_id(1)
    nblk = pl.num_programs(1)
    tm = x_ref.shape[1]

    def modulated(xb):
        ms = jnp.mean(xb * xb, axis=-1, keepdims=True)
        hm = (xb * lax.rsqrt(ms + EPS) * g_ref[...]) * (1.0 + sc_ref[0]) + sh_ref[0]
        return hm.astype(BF16)

    hb = modulated(x_ref[0])

    def proj(lo, hi):
        return _dot(hb, w_ref[:, lo:hi])

    cos = cos_ref[...]
    sin = sin_ref[...]
    lane = lax.broadcasted_iota(jnp.int32, (tm, LANES), 1)
    first_half = (lane % ROPE_AXIS_DIM) < (ROPE_AXIS_DIM // 2)

    def rope(t):
        outs = []
        for j in range(t.shape[1] // LANES):
            s = t[:, LANES * j:LANES * (j + 1)]
            partner = jnp.where(first_half,
                                pltpu.roll(s, LANES - ROPE_AXIS_DIM // 2, 1),
                                pltpu.roll(s, ROPE_AXIS_DIM // 2, 1))
            outs.append(s * cos + partner * sin)
        return jnp.concatenate(outs, axis=1)

    q_ref[0] = (rope(proj(COL_Q, COL_K)) * (DA_D ** -0.5 * LOG2_E)).astype(BF16)
    k_ref[0] = rope(proj(COL_K, COL_V)).astype(BF16)
    vt = proj(COL_V, COL_QKV).T.astype(BF16)
    vt_ref[0, :, 0, 0:DA_DV, :] = vt.reshape(DA_HEADS, DA_DV, tm)
    extra = lax.broadcasted_iota(jnp.int32, (DA_HEADS, VT_ROWS - DA_DV, tm), 1)
    vt_ref[0, :, 0, DA_DV:VT_ROWS, :] = jnp.where(extra == 0, 1.0, 0.0).astype(BF16)
    gate_ref[0] = proj(COL_GATE, COL_AB)

    qkv = proj(COL_QKV, COL_GATE)
    halo = _dot(modulated(jnp.concatenate([xp_ref[0], xn_ref[0]], axis=0)),
                w_ref[:, COL_QKV:COL_GATE])
    prev = jnp.where(i > 0, halo[7:8, :], 0.0)
    nxt = jnp.where(i < nblk - 1, halo[8:9, :], 0.0)
    row = lax.broadcasted_iota(jnp.int32, (tm, 1), 0)
    qkv_m1 = jnp.where(row == 0, prev, pltpu.roll(qkv, 1, 0))
    qkv_p1 = jnp.where(row == tm - 1, nxt, pltpu.roll(qkv, tm - 1, 0))
    cw = cw_ref[...]
    s = _silu(qkv_m1 * cw[0:1] + qkv * cw[1:2] + qkv_p1 * cw[2:3])

    def l2n(t):
        return t * lax.rsqrt(jnp.sum(t * t, axis=-1, keepdims=True) + EPS)

    for h in range(GDN_HEADS):
        lo, hi = h * GDN_DK, (h + 1) * GDN_DK
        gq_ref[0, :, lo:hi] = l2n(s[:, lo:hi]) * GDN_DK ** -0.5
        gk_ref[0, :, lo:hi] = l2n(s[:, GDN_QK + lo:GDN_QK + hi])
    gv_ref[0] = s[:, 2 * GDN_QK:]

    ab = proj(COL_AB, IN_COLS_PAD)
    z = ab + dtb_ref[...]
    softplus = jnp.maximum(z, 0.0) + jnp.log1p(jnp.exp(-jnp.abs(z)))
    gval = -jnp.exp(al_ref[...]) * softplus
    gb = jnp.where(lane < GDN_AB // 2, gval, jnp.where(lane < GDN_AB, _sigmoid(ab), 0.0))
    gb_ref[0] = gb
    gbt_ref[0] = gb.T[:GDN_AB, :]


def _inproj_call(x, norm_g, shift, scale, cos, sin, w_bf, conv_w8, alog_row, dtb_row, tm):
    b, n, d = x.shape
    nb8 = n // 8
    step8 = tm // 8
    row = lambda bi, i: (bi, i, 0)
    vec = lambda bi, i: (bi, 0, 0)
    const = lambda bi, i: (0, 0)
    rows_out = [(DA_QK, BF16), (DA_QK, BF16), None, (GDN_QK, F32), (GDN_QK, F32), (GDN_V, F32),
                (GDN_V, F32), (LANES, F32)]
    out_specs = [pl.BlockSpec((1, tm, o[0]), row) if o else
                 pl.BlockSpec((1, DA_HEADS, 1, VT_ROWS, tm), lambda bi, i: (bi, 0, i, 0, 0))
                 for o in rows_out]
    out_shape = [jax.ShapeDtypeStruct((b, n, o[0]), o[1]) if o else
                 jax.ShapeDtypeStruct((b, DA_HEADS, n // tm, VT_ROWS, tm), BF16) for o in rows_out]
    out_specs.append(pl.BlockSpec((1, GDN_AB, tm), lambda bi, i: (bi, 0, i)))
    out_shape.append(jax.ShapeDtypeStruct((b, GDN_AB, n), F32))
    return pl.pallas_call(
        _inproj_kernel,
        grid=(b, n // tm),
        in_specs=[pl.BlockSpec((1, tm, d), row),
                  pl.BlockSpec((1, 8, d), lambda bi, i: (bi, jnp.maximum(i * step8 - 1, 0), 0)),
                  pl.BlockSpec((1, 8, d), lambda bi, i: (bi, jnp.minimum((i + 1) * step8, nb8 - 1), 0)),
                  pl.BlockSpec((1, d), const),
                  pl.BlockSpec((1, 1, d), vec),
                  pl.BlockSpec((1, 1, d), vec),
                  pl.BlockSpec((tm, LANES), lambda bi, i: (i, 0)),
                  pl.BlockSpec((tm, LANES), lambda bi, i: (i, 0)),
                  pl.BlockSpec((d, IN_COLS_PAD), const),
                  pl.BlockSpec((8, GDN_QKV), const),
                  pl.BlockSpec((1, LANES), const),
                  pl.BlockSpec((1, LANES), const)],
        out_specs=out_specs,
        out_shape=out_shape,
        compiler_params=_cparams(("parallel", "parallel")),
        name="inproj",
    )(x, x, x, norm_g, shift, scale, cos, sin, w_bf, conv_w8, alog_row, dtb_row)


def _attn_kernel(q_ref, k_ref, vt_ref, lam_ref, g_ref, o_ref,
                 q2_ref, acc_ref, m_ref, s_ref, p_ref, a_ref, *, tk):
    q = q_ref[0]
    tq = q.shape[0]
    lane = lax.broadcasted_iota(jnp.int32, (tq, LANES), 1)
    zero = jnp.zeros_like(q)
    q2_ref[0:tq, :] = jnp.where(lane < DA_D, q, zero)
    q2_ref[tq:2 * tq, :] = jnp.where(lane >= DA_D, q, zero)
    m_ref[...] = jnp.full(m_ref.shape, NEG_BIG, F32)
    acc_ref[...] = jnp.zeros(acc_ref.shape, F32)

    nct = 2 * tq // ATTN_COL_TILE
    col = lambda ct: slice(ct * ATTN_COL_TILE, (ct + 1) * ATTN_COL_TILE)

    def scores(kk, ct):
        return _dot_nt(kk, q2_ref[col(ct), :])

    def softmax_update(ct, s):
        cols = col(ct)
        m_old = m_ref[:, cols]
        m_new = jnp.maximum(m_old, jnp.max(s, axis=0, keepdims=True))
        alpha = jnp.exp2(m_old - m_new)
        p = jnp.exp2(s - m_new)
        m_ref[:, cols] = m_new
        return p.astype(BF16), alpha

    def value_update(ct, vt, p, alpha):
        cols = col(ct)
        acc_ref[:, cols] = alpha * acc_ref[:, cols] + _dot(vt, p)

    nchunk = k_ref.shape[1] // tk
    last = nct - 1
    s_ref[...] = scores(k_ref[0, 0:tk, :], 0)
    p_ref[...] = jnp.zeros(p_ref.shape, BF16)
    a_ref[...] = jnp.ones(a_ref.shape, F32)

    def body(j, carry):
        kk = k_ref[0, pl.ds(pl.multiple_of(j * tk, tk), tk), :]
        vt = vt_ref[0, 0, j]
        s_cur = s_ref[...]
        pending = (last, vt_ref[0, 0, jnp.maximum(j - 1, 0)], p_ref[...], a_ref[...])
        for ct in range(nct):
            if ct < last:
                s_ahead = scores(kk, ct + 1)
            else:
                jn = jnp.minimum(j + 1, nchunk - 1)
                s_ahead = scores(k_ref[0, pl.ds(pl.multiple_of(jn * tk, tk), tk), :], 0)
            p, alpha = softmax_update(ct, s_cur)
            value_update(*pending)
            pending = (ct, vt, p, alpha)
            s_cur = s_ahead
        s_ref[...] = s_cur
        p_ref[...] = pending[2]
        a_ref[...] = pending[3]
        return carry

    lax.fori_loop(0, nchunk, body, 0)
    value_update(last, vt_ref[0, 0, nchunk - 1], p_ref[...], a_ref[...])

    lv = lam_ref[...]
    lam = (jnp.exp(jnp.sum(lv[0:1] * lv[1:2], axis=1, keepdims=True))
           - jnp.exp(jnp.sum(lv[2:3] * lv[3:4], axis=1, keepdims=True)) + LAM_INIT)
    den = acc_ref[DA_DV:DA_DV + 1, :]
    ot = (acc_ref[0:DA_DV, 0:tq] / den[:, 0:tq]
          - lam * (acc_ref[0:DA_DV, tq:2 * tq] / den[:, tq:2 * tq]))
    o = ot.T
    ms = jnp.mean(o * o, axis=-1, keepdims=True)
    o_ref[0] = ((o * lax.rsqrt(ms + EPS) * g_ref[...]) * (1.0 - LAM_INIT)).astype(o_ref.dtype)


def _attn_call(q, k_all, vt_all, lam_vecs, subln_g, tq):
    b, n, _ = q.shape
    nk = k_all.shape[1]
    _, _, nchunk, _, tk = vt_all.shape
    return pl.pallas_call(
        functools.partial(_attn_kernel, tk=tk),
        grid=(b, DA_HEADS, n // tq),
        in_specs=[pl.BlockSpec((1, tq, LANES), lambda bi, h, i: (bi, i, h)),
                  pl.BlockSpec((1, nk, LANES), lambda bi, h, i: (bi, 0, h)),
                  pl.BlockSpec((1, 1, nchunk, VT_ROWS, tk), lambda bi, h, i: (bi, h, 0, 0, 0)),
                  pl.BlockSpec((4, DA_D), lambda bi, h, i: (0, 0)),
                  pl.BlockSpec((1, DA_DV), lambda bi, h, i: (0, 0))],
        out_specs=pl.BlockSpec((1, tq, LANES), lambda bi, h, i: (bi, i, h)),
        out_shape=jax.ShapeDtypeStruct((b, n, DA_V), BF16),
        scratch_shapes=[pltpu.VMEM((2 * tq, LANES), BF16), pltpu.VMEM((VT_ROWS, 2 * tq), F32),
                        pltpu.VMEM((1, 2 * tq), F32),
                        pltpu.VMEM((tk, ATTN_COL_TILE), F32), pltpu.VMEM((tk, ATTN_COL_TILE), BF16),
                        pltpu.VMEM((1, ATTN_COL_TILE), F32)],
        compiler_params=_cparams(("parallel", "parallel", "parallel")),
        name="attn",
    )(q, k_all, vt_all, lam_vecs, subln_g)


def _solve_unit_triangular(a_list, rhs_list, bd_mask):
    def mm(xs, ys):
        return [_dot(x.astype(BF16), y.astype(BF16)) for x, y in zip(xs, ys)]

    d = [jnp.where(bd_mask, a, 0.0) for a in a_list]
    n = [a - di for a, di in zip(a_list, d)]
    d2 = mm(d, d)
    dd2 = mm(d, d2)
    p = [(-di + d2i) - t for di, d2i, t in zip(d, d2, dd2)]
    d4 = mm(d2, d2)
    pd4 = mm(p, d4)
    p = [pi + d4i + t for pi, d4i, t in zip(p, d4, pd4)]
    d8 = mm(d4, d4)
    pd8 = mm(p, d8)
    p = [pi + d8i + t for pi, d8i, t in zip(p, d8, pd8)]
    m = [ni + t for ni, t in zip(n, mm(p, n))]
    z = [ri + t for ri, t in zip(rhs_list, mm(p, rhs_list))]
    m2 = mm(m, m)
    z = [zi + t for zi, t in zip(z, mm(m2, z))]
    return [zi - t for zi, t in zip(z, mm(m, z))]


def _gdn_scan_kernel(qf_ref, kf_ref, vf_ref, gbf_ref, gbtf_ref, qb_ref, kb_ref, vb_ref, gbb_ref, gbtb_ref,
                     s0_ref, of_ref, ob_ref, sfin_ref, s_ref):
    i = pl.program_id(1)

    @pl.when(i == 0)
    def _():
        s_ref[...] = s0_ref[0]

    c64 = CHUNK
    nchunk = qf_ref.shape[1] // c64
    ri = lax.broadcasted_iota(jnp.int32, (c64, c64), 0)
    ci = lax.broadcasted_iota(jnp.int32, (c64, c64), 1)
    bd_mask = (ri // 16) == (ci // 16)
    lower_incl, upper_incl = ri >= ci, ri <= ci
    heads = range(GDN_HEADS)
    rows = [slice(c * c64, (c + 1) * c64) for c in range(nchunk)]
    cols = [slice(h * GDN_DK, (h + 1) * GDN_DK) for h in heads]

    dirs = []
    for d, refs in enumerate(((qf_ref, kf_ref, vf_ref, gbf_ref, gbtf_ref, of_ref),
                              (qb_ref, kb_ref, vb_ref, gbb_ref, gbtb_ref, ob_ref))):
        rev = d == 1
        incl = upper_incl if rev else lower_incl
        dirs.append(dict(
            refs=refs, incl=incl, strict=(ri < ci) if rev else (ri > ci),
            tri_c=jnp.where(incl, 1.0, 0.0).astype(BF16),
            tri_r=jnp.where(lower_incl if rev else upper_incl, 1.0, 0.0).astype(BF16),
            last=0 if rev else c64 - 1,
            order=list(range(nchunk - 1, -1, -1) if rev else range(nchunk))))

    gall, gc_col, gc_row = {}, {}, {}
    for d, dr in enumerate(dirs):
        gb_ref, gbt_ref = dr["refs"][3], dr["refs"][4]
        for c in range(nchunk):
            gall[d, c] = gb_ref[0, rows[c], :]
            g1, g2, g3 = _split3(gall[d, c])
            tc, tr = dr["tri_c"], dr["tri_r"]
            gc_col[d, c] = _dot(tc, g1) + _dot(tc, g2) + _dot(tc, g3)
            t1, t2, t3 = _split3(gbt_ref[0, :, rows[c]])
            gc_row[d, c] = _dot(t1, tr) + _dot(t2, tr) + _dot(t3, tr)

    probs = [(d, c, h) for d in range(2) for c in range(nchunk) for h in heads]
    ln = lambda d, h: d * GDN_HEADS + h
    k = {(d, c, h): dirs[d]["refs"][1][0, rows[c], cols[h]] for d, c, h in probs}
    q = {(d, c, h): dirs[d]["refs"][0][0, rows[c], cols[h]] for d, c, h in probs}
    kbf = {p: k[p].astype(BF16) for p in probs}
    kk = {p: _dot_nt(kbf[p], kbf[p]) for p in probs}
    qk = {p: _dot_nt(q[p].astype(BF16), kbf[p]) for p in probs}
    a_list, rhs_list = [], []
    kd, qd, aqk, glast = {}, {}, {}, {}
    for p in probs:
        d, c, h = p
        dr = dirs[d]
        lane = ln(d, h)
        gcol = gc_col[d, c][:, lane:lane + 1]
        bcol = gall[d, c][:, GDN_AB // 2 + lane:GDN_AB // 2 + lane + 1]
        diff = gcol - gc_row[d, c][lane:lane + 1, :]
        e_strict = jnp.where(dr["strict"], jnp.exp(jnp.where(dr["strict"], diff, 0.0)), 0.0)
        e_incl = jnp.where(dr["incl"], jnp.exp(jnp.where(dr["incl"], diff, 0.0)), 0.0)
        eg = jnp.exp(gcol)
        glast[p] = gcol[dr["last"]:dr["last"] + 1, :]
        a_list.append(bcol * kk[p] * e_strict)
        v = dr["refs"][2][0, rows[c], cols[h]]
        rhs_list.append(jnp.concatenate([(bcol * eg) * k[p], bcol * v], axis=1))
        kd[p] = (k[p] * jnp.exp(glast[p] - gcol)).astype(BF16)
        qd[p] = (q[p] * eg).astype(BF16)
        aqk[p] = (qk[p] * e_incl).astype(BF16)
    sol = dict(zip(probs, _solve_unit_triangular(a_list, rhs_list, bd_mask)))

    chains = [(d, h) for d in range(2) for h in heads]
    for t in range(nchunk):
        cur = {(d, h): (d, dirs[d]["order"][t], h) for d, h in chains}
        s = {dh: s_ref[dh[0], dh[1]] for dh in chains}
        sb = {dh: s[dh].astype(BF16) for dh in chains}
        ws = {dh: _dot(sol[cur[dh]][:, :GDN_DK].astype(BF16), sb[dh]) for dh in chains}
        qs = {dh: _dot(qd[cur[dh]], sb[dh]) for dh in chains}
        u = {dh: (sol[cur[dh]][:, GDN_DK:] - ws[dh]).astype(BF16) for dh in chains}
        au = {dh: _dot(aqk[cur[dh]], u[dh]) for dh in chains}
        ku = {dh: _dot_tn(kd[cur[dh]], u[dh]) for dh in chains}
        for dh in chains:
            d, c, h = cur[dh]
            s_ref[d, h] = jnp.exp(glast[cur[dh]]) * s[dh] + ku[dh]
            dirs[d]["refs"][5][0, rows[c], cols[h]] = qs[dh] + au[dh]

    @pl.when(i == pl.num_programs(1) - 1)
    def _():
        sfin_ref[0] = s_ref[...]


def _gdn_scan_call(q, k, v, gb, gbt, s0, sc):
    b, n, _ = q.shape
    nsup = n // sc
    fwd = lambda bi, i: (bi, i, 0)
    bwd = lambda bi, i: (bi, nsup - 1 - i, 0)
    st = lambda bi, i: (bi, 0, 0, 0, 0)
    state = (1, 2, GDN_HEADS, GDN_DK, GDN_DV)

    def in_specs(row, tr):
        return [pl.BlockSpec((1, sc, GDN_QK), row), pl.BlockSpec((1, sc, GDN_QK), row),
                pl.BlockSpec((1, sc, GDN_V), row), pl.BlockSpec((1, sc, LANES), row),
                pl.BlockSpec((1, GDN_AB, sc), tr)]

    return pl.pallas_call(
        _gdn_scan_kernel,
        grid=(b, nsup),
        in_specs=(in_specs(fwd, lambda bi, i: (bi, 0, i))
                  + in_specs(bwd, lambda bi, i: (bi, 0, nsup - 1 - i))
                  + [pl.BlockSpec(state, st)]),
        out_specs=[pl.BlockSpec((1, sc, GDN_V), fwd), pl.BlockSpec((1, sc, GDN_V), bwd),
                   pl.BlockSpec(state, st)],
        out_shape=[jax.ShapeDtypeStruct((b, n, GDN_V), F32), jax.ShapeDtypeStruct((b, n, GDN_V), F32),
                   jax.ShapeDtypeStruct((b,) + state[1:], F32)],
        scratch_shapes=[pltpu.VMEM(state[1:], F32)],
        compiler_params=_cparams(("parallel", "arbitrary")),
        name="gdn_scan",
    )(q, k, v, gb, gbt, q, k, v, gb, gbt, s0)


def _outproj_kernel(oda_ref, of_ref, ob_ref, gate_ref, x_ref, gt1_ref, gng_ref, wout_ref,
                    n2g_ref, sh2_ref, sc2_ref, wr_ref, xnew_ref, ht_ref, aff_ref, afft_ref):
    og = of_ref[0] + ob_ref[0]
    gate = gate_ref[0]
    ys = []
    for h in range(GDN_HEADS):
        cols = slice(h * GDN_DV, (h + 1) * GDN_DV)
        t = og[:, cols]
        y = t * lax.rsqrt(jnp.mean(t * t, axis=-1, keepdims=True) + EPS) * gng_ref[...]
        ys.append((y * _silu(gate[:, cols])).astype(BF16))
    y_gdn = jnp.concatenate(ys, axis=1)
    proj = _dot(oda_ref[0], wout_ref[0:DA_V, :]) + _dot(y_gdn, wout_ref[DA_V:, :])
    xn = x_ref[0] + gt1_ref[0] * proj
    xnew_ref[0] = xn
    ms = jnp.mean(xn * xn, axis=-1, keepdims=True)
    hm = (xn * lax.rsqrt(ms + EPS) * n2g_ref[...]) * (1.0 + sc2_ref[0]) + sh2_ref[0]
    hmt = hm.T.astype(BF16)
    tb = ht_ref.shape[3]
    for c in range(ht_ref.shape[1]):
        ht_ref[0, c] = hmt[:, c * tb:(c + 1) * tb]
    logits = _dot_x3(hm, wr_ref[...])
    lane = lax.broadcasted_iota(jnp.int32, logits.shape, 1)
    logits = jnp.where(lane < N_EXPERTS, logits, NEG_BIG)
    e = jnp.exp(logits - jnp.max(logits, axis=-1, keepdims=True))
    aff = e / jnp.sum(e, axis=-1, keepdims=True)
    aff_ref[0] = aff
    afft_ref[0] = aff.T[:N_EXPERTS, :]


def _outproj_call(o_da, o_f, o_b, gate, x, gt1, gdn_norm_g, w_out_bf, norm2_g, sh2, sc2, wr_pad, tm, tb):
    b, n, d = x.shape
    assert tm % tb == 0
    row = lambda bi, i: (bi, i, 0)
    vec = lambda bi, i: (bi, 0, 0)
    const2 = lambda bi, i: (0, 0)
    return pl.pallas_call(
        _outproj_kernel,
        grid=(b, n // tm),
        in_specs=[pl.BlockSpec((1, tm, DA_V), row), pl.BlockSpec((1, tm, GDN_V), row),
                  pl.BlockSpec((1, tm, GDN_V), row), pl.BlockSpec((1, tm, GDN_V), row),
                  pl.BlockSpec((1, tm, d), row), pl.BlockSpec((1, 1, d), vec),
                  pl.BlockSpec((1, GDN_DV), const2), pl.BlockSpec((DA_V + GDN_V, d), const2),
                  pl.BlockSpec((1, d), const2), pl.BlockSpec((1, 1, d), vec), pl.BlockSpec((1, 1, d), vec),
                  pl.BlockSpec((d, LANES), const2)],
        out_specs=[pl.BlockSpec((1, tm, d), row),
                   pl.BlockSpec((1, tm // tb, d, tb), lambda bi, i: (bi, i, 0, 0)),
                   pl.BlockSpec((1, tm, LANES), row),
                   pl.BlockSpec((1, N_EXPERTS, tm), lambda bi, i: (bi, 0, i))],
        out_shape=[jax.ShapeDtypeStruct((b, n, d), F32), jax.ShapeDtypeStruct((b, n // tb, d, tb), BF16),
                   jax.ShapeDtypeStruct((b, n, LANES), F32),
                   jax.ShapeDtypeStruct((b, N_EXPERTS, n), F32)],
        compiler_params=_cparams(("parallel", "parallel")),
        name="outproj",
    )(o_da, o_f, o_b, gate, x, gt1, gdn_norm_g, w_out_bf, norm2_g, sh2, sc2, wr_pad)


GATHER_GROUP = 4
PREFIX_BLK = 256
BISECT_STEPS = 64


def _select_kernel(afft_ref, aff_ref, post_ref, pos_ref, st_ref, rng_ref, *, cap, tb, tj):
    at = afft_ref[0]
    n = at.shape[1]

    def count_ge(t):
        return jnp.sum(jnp.where(at >= t, 1.0, 0.0), axis=1, keepdims=True)

    def bisect(_, bounds):
        lo, hi = bounds
        mid = 0.5 * (lo + hi)
        ok = count_ge(mid) >= cap
        return jnp.where(ok, mid, lo), jnp.where(ok, hi, mid)

    _, hi = lax.fori_loop(0, BISECT_STEPS, bisect,
                          (jnp.zeros((N_EXPERTS, 1), F32), jnp.full((N_EXPERTS, 1), 2.0, F32)))

    def below(h):
        return jnp.max(jnp.where(at < h, at, -1.0), axis=1, keepdims=True)

    def short(state):
        return jnp.sum(jnp.where(count_ge(state[0]) < cap, 1.0, 0.0)) > 0.0

    def step_down(state):
        t, h = state
        lacking = count_ge(t) < cap
        h = jnp.where(lacking, t, h)
        return jnp.where(lacking, below(h), t), h

    thr, _ = lax.while_loop(short, step_down, (below(hi), hi))
    need = cap - jnp.sum(jnp.where(at > thr, 1.0, 0.0), axis=1, keepdims=True)

    pi = lax.broadcasted_iota(jnp.int32, (PREFIX_BLK, PREFIX_BLK), 0)
    pj = lax.broadcasted_iota(jnp.int32, (PREFIX_BLK, PREFIX_BLK), 1)
    upper = jnp.where(pi <= pj, 1.0, 0.0).astype(BF16)
    lower = jnp.where(pi >= pj, 1.0, 0.0).astype(BF16)

    def prefix_lanes(m):
        carry = jnp.zeros((m.shape[0], 1), F32)
        outs = []
        for c in range(m.shape[1] // PREFIX_BLK):
            mc = m[:, c * PREFIX_BLK:(c + 1) * PREFIX_BLK]
            loc = _dot(mc.astype(BF16), upper)
            outs.append(loc - mc + carry)
            carry = carry + loc[:, PREFIX_BLK - 1:PREFIX_BLK]
        return jnp.concatenate(outs, axis=1)

    def prefix_rows(m):
        carry = jnp.zeros((1, m.shape[1]), F32)
        outs = []
        for c in range(m.shape[0] // PREFIX_BLK):
            mc = m[c * PREFIX_BLK:(c + 1) * PREFIX_BLK, :]
            loc = _dot(lower, mc.astype(BF16))
            outs.append(loc - mc + carry)
            carry = carry + loc[PREFIX_BLK - 1:PREFIX_BLK, :]
        return jnp.concatenate(outs, axis=0)

    eq_t = jnp.where(at == thr, 1.0, 0.0)
    sel_t = jnp.where(at > thr, 1.0, jnp.where(prefix_lanes(eq_t) < need, eq_t, 0.0))
    post_ref[0] = jnp.where(sel_t > 0.0, prefix_lanes(sel_t), -1.0)

    ti = lax.broadcasted_iota(jnp.int32, (n, LANES), 0)
    bi = lax.broadcasted_iota(jnp.int32, (n, LANES), 1)
    before = jnp.where(ti < bi * tb, 1.0, 0.0).astype(BF16)
    st = _dot(sel_t.astype(BF16), before)
    st_ref[0] = st.astype(jnp.int32)

    through = jnp.where(ti < (bi + 1) * tb, 1.0, 0.0).astype(BF16)
    st_end = _dot(sel_t.astype(BF16), through)
    blk_lane = lax.broadcasted_iota(jnp.int32, (N_EXPERTS, LANES), 1)
    is_blk = blk_lane < n // tb
    rng = jnp.zeros((N_EXPERTS, LANES), F32)
    ntile = cap // tj
    for j in range(ntile):
        first = jnp.sum(jnp.where(is_blk & (st_end <= j * tj), 1.0, 0.0), axis=1, keepdims=True)
        last = jnp.sum(jnp.where(is_blk & (st < (j + 1) * tj), 1.0, 0.0), axis=1, keepdims=True) - 1.0
        rng = rng + jnp.where(blk_lane == j, first, 0.0) + jnp.where(blk_lane == ntile + j, last, 0.0)
    rng_ref[0] = rng.astype(jnp.int32)

    a = aff_ref[0]
    er = lax.broadcasted_iota(jnp.int32, (N_EXPERTS, LANES), 0)
    ec = lax.broadcasted_iota(jnp.int32, (N_EXPERTS, LANES), 1)
    diag = er == ec
    thr_row = jnp.sum(jnp.where(diag, thr, 0.0), axis=0, keepdims=True)
    need_row = jnp.sum(jnp.where(diag, need, 0.0), axis=0, keepdims=True)
    valid = lax.broadcasted_iota(jnp.int32, a.shape, 1) < N_EXPERTS
    eq = jnp.where(valid & (a == thr_row), 1.0, 0.0)
    sel = jnp.where(valid & (a > thr_row), 1.0, jnp.where(prefix_rows(eq) < need_row, eq, 0.0))
    pos_ref[0] = jnp.where(sel > 0.0, prefix_rows(sel), -1.0)


def _select_call(afft, aff, cap, tb, tj):
    b, e, n = afft.shape
    assert n // tb + 1 <= LANES and 2 * (cap // tj) <= LANES
    return pl.pallas_call(
        functools.partial(_select_kernel, cap=cap, tb=tb, tj=tj),
        grid=(b,),
        in_specs=[pl.BlockSpec((1, e, n), lambda bi: (bi, 0, 0)),
                  pl.BlockSpec((1, n, LANES), lambda bi: (bi, 0, 0))],
        out_specs=[pl.BlockSpec((1, e, n), lambda bi: (bi, 0, 0)),
                   pl.BlockSpec((1, n, LANES), lambda bi: (bi, 0, 0)),
                   pl.BlockSpec((1, e, LANES), lambda bi: (bi, 0, 0)),
                   pl.BlockSpec((1, e, LANES), lambda bi: (bi, 0, 0))],
        out_shape=[jax.ShapeDtypeStruct((b, e, n), F32), jax.ShapeDtypeStruct((b, n, LANES), F32),
                   jax.ShapeDtypeStruct((b, e, LANES), jnp.int32),
                   jax.ShapeDtypeStruct((b, e, LANES), jnp.int32)],
        compiler_params=_cparams(("parallel",)),
        name="select",
    )(afft, aff)


def _moe_ffn_kernel(rng_ref, ht_ref, post_ref, afft_ref, wg_ref, wu_ref, wd_ref, ye_ref,
                    acc_ref, gacc_ref, wgb_ref, wub_ref, wdb_ref, *, tj, tb):
    b = pl.program_id(0)
    e = pl.program_id(1)
    cap = ye_ref.shape[2]
    ntile = cap // tj
    slot = lax.broadcasted_iota(jnp.int32, (tj, tb), 0).astype(F32)
    wgb_ref[...] = wg_ref[0].astype(BF16)
    wub_ref[...] = wu_ref[0].astype(BF16)
    wdb_ref[...] = wd_ref[0].astype(BF16)

    for j in range(cap // tj):
        lo = j * tj
        acc_ref[...] = jnp.zeros(acc_ref.shape, F32)
        gacc_ref[...] = jnp.zeros(gacc_ref.shape, F32)

        first = rng_ref[b, e, j]
        last = rng_ref[b, e, ntile + j]

        def group_body(g, carry):
            part = jnp.zeros(acc_ref.shape, F32)
            gate = jnp.zeros(gacc_ref.shape, F32)
            for u in range(GATHER_GROUP):
                bi = jnp.minimum(first + g * GATHER_GROUP + u, last)
                live = first + g * GATHER_GROUP + u <= last
                prow = post_ref[0, 0, pl.ds(bi, 1), :]
                hit = (prow == (slot + float(lo))) & live
                onehot = jnp.where(hit, 1.0, 0.0).astype(BF16)
                part = part + _dot_nt(ht_ref[0, bi], onehot)
                arow = afft_ref[0, 0, pl.ds(bi, 1), :]
                gate = gate + jnp.sum(jnp.where(hit, arow, 0.0), axis=1, keepdims=True)
            acc_ref[...] += part
            gacc_ref[...] += gate
            return carry

        lax.fori_loop(0, (last - first + GATHER_GROUP) // GATHER_GROUP, group_body, 0)
        xe = acc_ref[...].T.astype(BF16)
        hid = _silu(_dot(xe, wgb_ref[...])) * _dot(xe, wub_ref[...])
        ye = _dot(hid.astype(BF16), wdb_ref[...]) * gacc_ref[...]
        ye_ref[0, 0, lo:lo + tj, :] = ye.astype(ye_ref.dtype)


def _moe_ffn_call(tile_rng, ht4, post4, afft4, wg, wu, wd, cap, tj):
    b, nblk, d, tb = ht4.shape
    e, _, f = wg.shape
    wspec = lambda shp: pl.BlockSpec((1,) + shp, lambda bi, ei: (ei, 0, 0))
    return pl.pallas_call(
        functools.partial(_moe_ffn_kernel, tj=tj, tb=tb),
        grid=(b, e),
        in_specs=[pl.BlockSpec(memory_space=pltpu.SMEM),
                  pl.BlockSpec((1, nblk, d, tb), lambda bi, ei: (bi, 0, 0, 0), pipeline_mode=pl.Buffered(1)),
                  pl.BlockSpec((1, 1, nblk, tb), lambda bi, ei: (bi, ei, 0, 0)),
                  pl.BlockSpec((1, 1, nblk, tb), lambda bi, ei: (bi, ei, 0, 0)),
                  wspec((d, f)), wspec((d, f)), wspec((f, d))],
        out_specs=pl.BlockSpec((1, 1, cap, d), lambda bi, ei: (bi, ei, 0, 0)),
        out_shape=jax.ShapeDtypeStruct((b, e, cap, d), BF16),
        scratch_shapes=[pltpu.VMEM((d, tj), F32), pltpu.VMEM((tj, 1), F32),
                        pltpu.VMEM((d, f), BF16), pltpu.VMEM((d, f), BF16), pltpu.VMEM((f, d), BF16)],
        compiler_params=_cparams(("parallel", "arbitrary")),
        name="moe_ffn",
    )(tile_rng, ht4, post4, afft4, wg, wu, wd)


def _combine_kernel(st_ref, pos_ref, ye_ref, x_ref, gt2_ref, g_ref, o_ref, *, tb):
    b = pl.program_id(0)
    blk = pl.program_id(1)
    cap = ye_ref.shape[2]
    win = 2 * tb
    slot = lax.broadcasted_iota(jnp.int32, (tb, win), 1).astype(F32)
    pos = pos_ref[0]
    moe = jnp.zeros((tb, ye_ref.shape[3]), F32)
    for e in range(N_EXPERTS):
        base = jnp.minimum(st_ref[b, e, blk] // tb, cap // tb - 2) * tb
        onehot = jnp.where(pos[:, e:e + 1] == slot + base.astype(F32), 1.0, 0.0).astype(BF16)
        moe = moe + _dot(onehot, ye_ref[0, e, pl.ds(pl.multiple_of(base, tb), win), :])
    y = x_ref[0] + gt2_ref[0] * moe
    ms = jnp.mean(y * y, axis=-1, keepdims=True)
    o_ref[0] = y * lax.rsqrt(ms + EPS) * g_ref[...]


def _combine_call(starts, pos, ye, x_new, gt2, final_g, tb):
    b, e, cap, d = ye.shape
    n = pos.shape[1]
    assert cap % tb == 0 and cap // tb >= 2
    row = lambda bi, i: (bi, i, 0)
    return pl.pallas_call(
        functools.partial(_combine_kernel, tb=tb),
        grid=(b, n // tb),
        in_specs=[pl.BlockSpec(memory_space=pltpu.SMEM),
                  pl.BlockSpec((1, tb, LANES), row),
                  pl.BlockSpec((1, e, cap, d), lambda bi, i: (bi, 0, 0, 0), pipeline_mode=pl.Buffered(1)),
                  pl.BlockSpec((1, tb, d), row),
                  pl.BlockSpec((1, 1, d), lambda bi, i: (bi, 0, 0)),
                  pl.BlockSpec((1, d), lambda bi, i: (0, 0))],
        out_specs=pl.BlockSpec((1, tb, d), row),
        out_shape=jax.ShapeDtypeStruct((b, n, d), F32),
        compiler_params=_cparams(("parallel", "arbitrary")),
        name="combine",
    )(starts, pos, ye, x_new, gt2, final_g)


def _rope_tables(n):
    t = np.arange(n)
    rows = (t // GRID_W).astype(np.float64)
    cols = (t % GRID_W).astype(np.float64)
    inv_freq = np.power(ROPE_THETA, -np.arange(0, ROPE_AXIS_DIM, 2, dtype=np.float64) / ROPE_AXIS_DIM)
    ang_row = rows[:, None] * inv_freq[None, :]
    ang_col = cols[:, None] * inv_freq[None, :]

    def axis_tables(ang):
        c = np.cos(ang).astype(np.float32)
        s = np.sin(ang).astype(np.float32)
        return np.concatenate([c, c], axis=1), np.concatenate([-s, s], axis=1)

    cr, sr = axis_tables(ang_row)
    cc, sc = axis_tables(ang_col)
    cos64 = np.concatenate([cr, cc], axis=1)
    sin64 = np.concatenate([sr, sc], axis=1)
    return (jnp.asarray(np.concatenate([cos64, cos64], axis=1)),
            jnp.asarray(np.concatenate([sin64, sin64], axis=1)))


def _pad_lanes(v):
    return jnp.pad(v.reshape(1, -1), ((0, 0), (0, LANES - v.size)))


def kernel(x, c, ctx, c_ctx, w_mod, b_mod, norm1_g, w_in, conv_w, a_log, dt_bias, gdn_norm_g,
           lam_q1, lam_k1, lam_q2, lam_k2, da_subln_g, w_out, norm2_g,
           w_router, w_gate, w_up, w_down, final_g):
    b, n, d = x.shape
    nc = ctx.shape[1]
    layer = 0

    cvec = jnp.concatenate([c, c_ctx[None, :], jnp.zeros((8 - b - 1, d), F32)], axis=0)
    mod = _mod_call(cvec, w_mod[layer], b_mod[layer])
    sh1, sc1, gt1, sh2, sc2, gt2 = [mod[:b, i * d:(i + 1) * d].reshape(b, 1, d) for i in range(6)]
    sh1c = jnp.broadcast_to(mod[b:b + 1, 0:d].reshape(1, 1, d), (b, 1, d))
    sc1c = jnp.broadcast_to(mod[b:b + 1, d:2 * d].reshape(1, 1, d), (b, 1, d))

    w_in_bf = jnp.pad(w_in[layer].astype(BF16), ((0, 0), (0, IN_COLS_PAD - w_in.shape[2])))
    g1 = norm1_g[layer].reshape(1, d)
    cos_l, sin_l = _rope_tables(n)
    cos_c, sin_c = jnp.ones((nc, LANES), F32), jnp.zeros((nc, LANES), F32)
    conv_w8 = jnp.pad(conv_w[layer], ((0, 8 - conv_w.shape[1]), (0, 0)))
    alog_row = _pad_lanes(a_log[layer])
    dtb_row = _pad_lanes(dt_bias[layer])
    gdn_args = (conv_w8, alog_row, dtb_row)
    q, k, vt, ql, kl, vl, gate, gbl, gbtl = _inproj_call(
        x, g1, sh1, sc1, cos_l, sin_l, w_in_bf, *gdn_args, tm=512)
    _, kc, vct, qc, kcg, vcg, _, gbc, gbtc = _inproj_call(
        ctx, g1, sh1c, sc1c, cos_c, sin_c, w_in_bf, *gdn_args, tm=nc)

    lam_vecs = jnp.stack([lam_q1[layer], lam_k1[layer], lam_q2[layer], lam_k2[layer]], axis=0)
    tk = vt.shape[-1]
    reps = tk // nc
    k_all = jnp.concatenate([kc] * reps + [k], axis=1)
    vct_pad = jnp.pad(vct, ((0, 0), (0, 0), (0, 0), (0, 0), (0, tk - nc)))
    vt_all = jnp.concatenate([vct_pad, vt], axis=2)
    o_da = _attn_call(q, k_all, vt_all, lam_vecs, da_subln_g[layer].reshape(1, DA_DV), tq=2048)

    zeros_state = jnp.zeros((b, 2, GDN_HEADS, GDN_DK, GDN_DV), F32)
    _, _, s_ctx = _gdn_scan_call(qc, kcg, vcg, gbc, gbtc, zeros_state, sc=nc)
    o_fwd, o_bwd, _ = _gdn_scan_call(ql, kl, vl, gbl, gbtl, s_ctx, sc=256)
    o_dirs = [o_fwd, o_bwd]

    wr_pad = jnp.pad(w_router[layer], ((0, 0), (0, LANES - N_EXPERTS)))
    cap = CAP_FACTOR * n // N_EXPERTS
    tb = 256
    tj = 256
    x_new, ht4, aff, afft = _outproj_call(
        o_da, o_dirs[0], o_dirs[1], gate, x, gt1, gdn_norm_g[layer].reshape(1, GDN_DV),
        w_out[layer].astype(BF16), norm2_g[layer].reshape(1, d), sh2, sc2, wr_pad, tm=512, tb=tb)

    post, pos, starts, tile_rng = _select_call(afft, aff, cap, tb, tj)
    nblk = n // tb
    ye = _moe_ffn_call(tile_rng, ht4, post.reshape(b, N_EXPERTS, nblk, tb),
                       afft.reshape(b, N_EXPERTS, nblk, tb),
                       w_gate[layer], w_up[layer], w_down[layer], cap, tj)
    return _combine_call(starts, pos, ye, x_new, gt2, final_g.reshape(1, d), tb)
```

```python
import functools
import math

import jax
import jax.numpy as jnp
import numpy as np
from jax import lax
from jax.experimental import pallas as pl
from jax.experimental.pallas import tpu as pltpu

F32 = jnp.float32
BF16 = jnp.bfloat16

D_MODEL = 1024
GRID_W = 64
EPS = 1e-6
DA_HEADS = 4
DA_D = 64
DA_DV = 2 * DA_D
ROPE_AXIS_DIM = DA_D // 2
ROPE_THETA = 10000.0
GDN_HEADS = 4
GDN_DK = 128
GDN_DV = 128
CHUNK = 64
N_EXPERTS = 16
CAP_FACTOR = 2
LAM_INIT = 0.8 - 0.6 * math.exp(-0.3 * 0)

DA_QK = DA_HEADS * 2 * DA_D
DA_V = DA_HEADS * DA_DV
GDN_QK = GDN_HEADS * GDN_DK
GDN_V = GDN_HEADS * GDN_DV
GDN_QKV = 2 * GDN_QK + GDN_V
GDN_AB = 2 * 2 * GDN_HEADS
COL_Q, COL_K, COL_V = 0, DA_QK, 2 * DA_QK
COL_QKV = 2 * DA_QK + DA_V
COL_GATE = COL_QKV + GDN_QKV
COL_AB = COL_GATE + GDN_V
LANES = 128
IN_COLS_PAD = COL_AB + LANES
MXU_WIDTH = 256
ATTN_COL_TILE = MXU_WIDTH
VT_ROWS = DA_DV + 8

VMEM_LIMIT = 56 * 1024 * 1024
NEG_BIG = -1e30
LOG2_E = math.log2(math.e)


def _cparams(sem):
    return pltpu.CompilerParams(dimension_semantics=sem, vmem_limit_bytes=VMEM_LIMIT)


def _sigmoid(x):
    return 1.0 / (1.0 + jnp.exp(-x))


def _silu(x):
    return x * _sigmoid(x)


def _split3(a):
    a1 = a.astype(BF16)
    r1 = a - a1.astype(F32)
    a2 = r1.astype(BF16)
    a3 = (r1 - a2.astype(F32)).astype(BF16)
    return a1, a2, a3


def _dot(a, b):
    return jnp.dot(a, b, preferred_element_type=F32)


def _dot_x3(a, b):
    a1, a2, _ = _split3(a)
    b1, b2, _ = _split3(b)
    return _dot(a1, b1) + (_dot(a1, b2) + _dot(a2, b1))


def _dot_nt(a, b):
    return lax.dot_general(a, b, (((1,), (1,)), ((), ())), preferred_element_type=F32)


def _dot_tn(a, b):
    return lax.dot_general(a, b, (((0,), (0,)), ((), ())), preferred_element_type=F32)


def _mod_kernel(c_ref, w_ref, b_ref, o_ref):
    s = _silu(c_ref[...])
    o_ref[...] = jnp.dot(s, w_ref[...], precision=lax.Precision.HIGHEST,
                         preferred_element_type=F32) + b_ref[...]


def _mod_call(cvec, w_mod, b_mod):
    d, n = w_mod.shape
    tn = 1024
    return pl.pallas_call(
        _mod_kernel,
        grid=(n // tn,),
        in_specs=[pl.BlockSpec((8, d), lambda j: (0, 0)),
                  pl.BlockSpec((d, tn), lambda j: (0, j)),
                  pl.BlockSpec((1, tn), lambda j: (0, j))],
        out_specs=pl.BlockSpec((8, tn), lambda j: (0, j)),
        out_shape=jax.ShapeDtypeStruct((8, n), F32),
        compiler_params=_cparams(("arbitrary",)),
        name="mod",
    )(cvec, w_mod, b_mod.reshape(1, n))


def _inproj_kernel(x_ref, xp_ref, xn_ref, g_ref, sh_ref, sc_ref, cos_ref, sin_ref, w_ref,
                   cw_ref, al_ref, dtb_ref,
                   q_ref, k_ref, vt_ref, gq_ref, gk_ref, gv_ref, gate_ref, gb_ref, gbt_ref):
    i = pl.program_id(1)
    nblk = pl.num_programs(1)
    tm = x_ref.shape[1]

    def modulated(xb):
        ms = jnp.mean(xb * xb, axis=-1, keepdims=True)
        hm = (xb * lax.rsqrt(ms + EPS) * g_ref[...]) * (1.0 + sc_ref[0]) + sh_ref[0]
        return hm.astype(BF16)

    hb = modulated(x_ref[0])

    def proj(lo, hi):
        return _dot(hb, w_ref[:, lo:hi])

    cos = cos_ref[...]
    sin = sin_ref[...]
    lane = lax.broadcasted_iota(jnp.int32, (tm, LANES), 1)
    first_half = (lane % ROPE_AXIS_DIM) < (ROPE_AXIS_DIM // 2)

    def rope(t):
        outs = []
        for j in range(t.shape[1] // LANES):
            s = t[:, LANES * j:LANES * (j + 1)]
            partner = jnp.where(first_half,
                                pltpu.roll(s, LANES - ROPE_AXIS_DIM // 2, 1),
                                pltpu.roll(s, ROPE_AXIS_DIM // 2, 1))
            outs.append(s * cos + partner * sin)
        return jnp.concatenate(outs, axis=1)

    q_ref[0] = (rope(proj(COL_Q, COL_K)) * (DA_D ** -0.5 * LOG2_E)).astype(BF16)
    k_ref[0] = rope(proj(COL_K, COL_V)).astype(BF16)
    vt = proj(COL_V, COL_QKV).T.astype(BF16)
    vt_ref[0, :, 0, 0:DA_DV, :] = vt.reshape(DA_HEADS, DA_DV, tm)
    extra = lax.broadcasted_iota(jnp.int32, (DA_HEADS, VT_ROWS - DA_DV, tm), 1)
    vt_ref[0, :, 0, DA_DV:VT_ROWS, :] = jnp.where(extra == 0, 1.0, 0.0).astype(BF16)
    gate_ref[0] = proj(COL_GATE, COL_AB)

    qkv = proj(COL_QKV, COL_GATE)
    halo = _dot(modulated(jnp.concatenate([xp_ref[0], xn_ref[0]], axis=0)),
                w_ref[:, COL_QKV:COL_GATE])
    prev = jnp.where(i > 0, halo[7:8, :], 0.0)
    nxt = jnp.where(i < nblk - 1, halo[8:9, :], 0.0)
    row = lax.broadcasted_iota(jnp.int32, (tm, 1), 0)
    qkv_m1 = jnp.where(row == 0, prev, pltpu.roll(qkv, 1, 0))
    qkv_p1 = jnp.where(row == tm - 1, nxt, pltpu.roll(qkv, tm - 1, 0))
    cw = cw_ref[...]
    s = _silu(qkv_m1 * cw[0:1] + qkv * cw[1:2] + qkv_p1 * cw[2:3])

    def l2n(t):
        return t * lax.rsqrt(jnp.sum(t * t, axis=-1, keepdims=True) + EPS)

    for h in range(GDN_HEADS):
        lo, hi = h * GDN_DK, (h + 1) * GDN_DK
        gq_ref[0, :, lo:hi] = l2n(s[:, lo:hi]) * GDN_DK ** -0.5
        gk_ref[0, :, lo:hi] = l2n(s[:, GDN_QK + lo:GDN_QK + hi])
    gv_ref[0] = s[:, 2 * GDN_QK:]

    ab = proj(COL_AB, IN_COLS_PAD)
    z = ab + dtb_ref[...]
    softplus = jnp.maximum(z, 0.0) + jnp.log1p(jnp.exp(-jnp.abs(z)))
    gval = -jnp.exp(al_ref[...]) * softplus
    gb = jnp.where(lane < GDN_AB // 2, gval, jnp.where(lane < GDN_AB, _sigmoid(ab), 0.0))
    gb_ref[0] = gb
    gbt_ref[0] = gb.T[:GDN_AB, :]


def _inproj_call(x, norm_g, shift, scale, cos, sin, w_bf, conv_w8, alog_row, dtb_row, tm):
    b, n, d = x.shape
    nb8 = n // 8
    step8 = tm // 8
    row = lambda bi, i: (bi, i, 0)
    vec = lambda bi, i: (bi, 0, 0)
    const = lambda bi, i: (0, 0)
    rows_out = [(DA_QK, BF16), (DA_QK, BF16), None, (GDN_QK, F32), (GDN_QK, F32), (GDN_V, F32),
                (GDN_V, F32), (LANES, F32)]
    out_specs = [pl.BlockSpec((1, tm, o[0]), row) if o else
                 pl.BlockSpec((1, DA_HEADS, 1, VT_ROWS, tm), lambda bi, i: (bi, 0, i, 0, 0))
                 for o in rows_out]
    out_shape = [jax.ShapeDtypeStruct((b, n, o[0]), o[1]) if o else
                 jax.ShapeDtypeStruct((b, DA_HEADS, n // tm, VT_ROWS, tm), BF16) for o in rows_out]
    out_specs.append(pl.BlockSpec((1, GDN_AB, tm), lambda bi, i: (bi, 0, i)))
    out_shape.append(jax.ShapeDtypeStruct((b, GDN_AB, n), F32))
    return pl.pallas_call(
        _inproj_kernel,
        grid=(b, n // tm),
        in_specs=[pl.BlockSpec((1, tm, d), row),
                  pl.BlockSpec((1, 8, d), lambda bi, i: (bi, jnp.maximum(i * step8 - 1, 0), 0)),
                  pl.BlockSpec((1, 8, d), lambda bi, i: (bi, jnp.minimum((i + 1) * step8, nb8 - 1), 0)),
                  pl.BlockSpec((1, d), const),
                  pl.BlockSpec((1, 1, d), vec),
                  pl.BlockSpec((1, 1, d), vec),
                  pl.BlockSpec((tm, LANES), lambda bi, i: (i, 0)),
                  pl.BlockSpec((tm, LANES), lambda bi, i: (i, 0)),
                  pl.BlockSpec((d, IN_COLS_PAD), const),
                  pl.BlockSpec((8, GDN_QKV), const),
                  pl.BlockSpec((1, LANES), const),
                  pl.BlockSpec((1, LANES), const)],
        out_specs=out_specs,
        out_shape=out_shape,
        compiler_params=_cparams(("parallel", "parallel")),
        name="inproj",
    )(x, x, x, norm_g, shift, scale, cos, sin, w_bf, conv_w8, alog_row, dtb_row)


def _attn_kernel(q_ref, k_ref, vt_ref, lam_ref, g_ref, o_ref,
                 q2_ref, acc_ref, m_ref, s_ref, p_ref, a_ref, *, tk):
    q = q_ref[0]
    tq = q.shape[0]
    lane = lax.broadcasted_iota(jnp.int32, (tq, LANES), 1)
    zero = jnp.zeros_like(q)
    q2_ref[0:tq, :] = jnp.where(lane < DA_D, q, zero)
    q2_ref[tq:2 * tq, :] = jnp.where(lane >= DA_D, q, zero)
    m_ref[...] = jnp.full(m_ref.shape, NEG_BIG, F32)
    acc_ref[...] = jnp.zeros(acc_ref.shape, F32)

    nct = 2 * tq // ATTN_COL_TILE
    col = lambda ct: slice(ct * ATTN_COL_TILE, (ct + 1) * ATTN_COL_TILE)

    def scores(kk, ct):
        return _dot_nt(kk, q2_ref[col(ct), :])

    def softmax_update(ct, s):
        cols = col(ct)
        m_old = m_ref[:, cols]
        m_new = jnp.maximum(m_old, jnp.max(s, axis=0, keepdims=True))
        alpha = jnp.exp2(m_old - m_new)
        p = jnp.exp2(s - m_new)
        m_ref[:, cols] = m_new
        return p.astype(BF16), alpha

    def value_update(ct, vt, p, alpha):
        cols = col(ct)
        acc_ref[:, cols] = alpha * acc_ref[:, cols] + _dot(vt, p)

    nchunk = k_ref.shape[1] // tk
    last = nct - 1
    s_ref[...] = scores(k_ref[0, 0:tk, :], 0)
    p_ref[...] = jnp.zeros(p_ref.shape, BF16)
    a_ref[...] = jnp.ones(a_ref.shape, F32)

    def body(j, carry):
        kk = k_ref[0, pl.ds(pl.multiple_of(j * tk, tk), tk), :]
        vt = vt_ref[0, 0, j]
        s_cur = s_ref[...]
        pending = (last, vt_ref[0, 0, jnp.maximum(j - 1, 0)], p_ref[...], a_ref[...])
        for ct in range(nct):
            if ct < last:
                s_ahead = scores(kk, ct + 1)
            else:
                jn = jnp.minimum(j + 1, nchunk - 1)
                s_ahead = scores(k_ref[0, pl.ds(pl.multiple_of(jn * tk, tk), tk), :], 0)
            p, alpha = softmax_update(ct, s_cur)
            value_update(*pending)
            pending = (ct, vt, p, alpha)
            s_cur = s_ahead
        s_ref[...] = s_cur
        p_ref[...] = pending[2]
        a_ref[...] = pending[3]
        return carry

    lax.fori_loop(0, nchunk, body, 0)
    value_update(last, vt_ref[0, 0, nchunk - 1], p_ref[...], a_ref[...])

    lv = lam_ref[...]
    lam = (jnp.exp(jnp.sum(lv[0:1] * lv[1:2], axis=1, keepdims=True))
           - jnp.exp(jnp.sum(lv[2:3] * lv[3:4], axis=1, keepdims=True)) + LAM_INIT)
    den = acc_ref[DA_DV:DA_DV + 1, :]
    ot = (acc_ref[0:DA_DV, 0:tq] / den[:, 0:tq]
          - lam * (acc_ref[0:DA_DV, tq:2 * tq] / den[:, tq:2 * tq]))
    o = ot.T
    ms = jnp.mean(o * o, axis=-1, keepdims=True)
    o_ref[0] = ((o * lax.rsqrt(ms + EPS) * g_ref[...]) * (1.0 - LAM_INIT)).astype(o_ref.dtype)


def _attn_call(q, k_all, vt_all, lam_vecs, subln_g, tq):
    b, n, _ = q.shape
    nk = k_all.shape[1]
    _, _, nchunk, _, tk = vt_all.shape
    return pl.pallas_call(
        functools.partial(_attn_kernel, tk=tk),
        grid=(b, DA_HEADS, n // tq),
        in_specs=[pl.BlockSpec((1, tq, LANES), lambda bi, h, i: (bi, i, h)),
                  pl.BlockSpec((1, nk, LANES), lambda bi, h, i: (bi, 0, h)),
                  pl.BlockSpec((1, 1, nchunk, VT_ROWS, tk), lambda bi, h, i: (bi, h, 0, 0, 0)),
                  pl.BlockSpec((4, DA_D), lambda bi, h, i: (0, 0)),
                  pl.BlockSpec((1, DA_DV), lambda bi, h, i: (0, 0))],
        out_specs=pl.BlockSpec((1, tq, LANES), lambda bi, h, i: (bi, i, h)),
        out_shape=jax.ShapeDtypeStruct((b, n, DA_V), BF16),
        scratch_shapes=[pltpu.VMEM((2 * tq, LANES), BF16), pltpu.VMEM((VT_ROWS, 2 * tq), F32),
                        pltpu.VMEM((1, 2 * tq), F32),
                        pltpu.VMEM((tk, ATTN_COL_TILE), F32), pltpu.VMEM((tk, ATTN_COL_TILE), BF16),
                        pltpu.VMEM((1, ATTN_COL_TILE), F32)],
        compiler_params=_cparams(("parallel", "parallel", "parallel")),
        name="attn",
    )(q, k_all, vt_all, lam_vecs, subln_g)


def _solve_unit_triangular(a_list, rhs_list, bd_mask):
    def mm(xs, ys):
        return [_dot(x.astype(BF16), y.astype(BF16)) for x, y in zip(xs, ys)]

    d = [jnp.where(bd_mask, a, 0.0) for a in a_list]
    n = [a - di for a, di in zip(a_list, d)]
    d2 = mm(d, d)
    dd2 = mm(d, d2)
    p = [(-di + d2i) - t for di, d2i, t in zip(d, d2, dd2)]
    d4 = mm(d2, d2)
    pd4 = mm(p, d4)
    p = [pi + d4i + t for pi, d4i, t in zip(p, d4, pd4)]
    d8 = mm(d4, d4)
    pd8 = mm(p, d8)
    p = [pi + d8i + t for pi, d8i, t in zip(p, d8, pd8)]
    m = [ni + t for ni, t in zip(n, mm(p, n))]
    z = [ri + t for ri, t in zip(rhs_list, mm(p, rhs_list))]
    m2 = mm(m, m)
    z = [zi + t for zi, t in zip(z, mm(m2, z))]
    return [zi - t for zi, t in zip(z, mm(m, z))]


def _gdn_scan_kernel(qf_ref, kf_ref, vf_ref, gbf_ref, gbtf_ref, qb_ref, kb_ref, vb_ref, gbb_ref, gbtb_ref,
                     s0_ref, of_ref, ob_ref, sfin_ref, s_ref):
    i = pl.program_id(1)

    @pl.when(i == 0)
    def _():
        s_ref[...] = s0_ref[0]

    c64 = CHUNK
    nchunk = qf_ref.shape[1] // c64
    ri = lax.broadcasted_iota(jnp.int32, (c64, c64), 0)
    ci = lax.broadcasted_iota(jnp.int32, (c64, c64), 1)
    bd_mask = (ri // 16) == (ci // 16)
    lower_incl, upper_incl = ri >= ci, ri <= ci
    heads = range(GDN_HEADS)
    rows = [slice(c * c64, (c + 1) * c64) for c in range(nchunk)]
    cols = [slice(h * GDN_DK, (h + 1) * GDN_DK) for h in heads]

    dirs = []
    for d, refs in enumerate(((qf_ref, kf_ref, vf_ref, gbf_ref, gbtf_ref, of_ref),
                              (qb_ref, kb_ref, vb_ref, gbb_ref, gbtb_ref, ob_ref))):
        rev = d == 1
        incl = upper_incl if rev else lower_incl
        dirs.append(dict(
            refs=refs, incl=incl, strict=(ri < ci) if rev else (ri > ci),
            tri_c=jnp.where(incl, 1.0, 0.0).astype(BF16),
            tri_r=jnp.where(lower_incl if rev else upper_incl, 1.0, 0.0).astype(BF16),
            last=0 if rev else c64 - 1,
            order=list(range(nchunk - 1, -1, -1) if rev else range(nchunk))))

    gall, gc_col, gc_row = {}, {}, {}
    for d, dr in enumerate(dirs):
        gb_ref, gbt_ref = dr["refs"][3], dr["refs"][4]
        for c in range(nchunk):
            gall[d, c] = gb_ref[0, rows[c], :]
            g1, g2, g3 = _split3(gall[d, c])
            tc, tr = dr["tri_c"], dr["tri_r"]
            gc_col[d, c] = _dot(tc, g1) + _dot(tc, g2) + _dot(tc, g3)
            t1, t2, t3 = _split3(gbt_ref[0, :, rows[c]])
            gc_row[d, c] = _dot(t1, tr) + _dot(t2, tr) + _dot(t3, tr)

    probs = [(d, c, h) for d in range(2) for c in range(nchunk) for h in heads]
    ln = lambda d, h: d * GDN_HEADS + h
    k = {(d, c, h): dirs[d]["refs"][1][0, rows[c], cols[h]] for d, c, h in probs}
    q = {(d, c, h): dirs[d]["refs"][0][0, rows[c], cols[h]] for d, c, h in probs}
    kbf = {p: k[p].astype(BF16) for p in probs}
    kk = {p: _dot_nt(kbf[p], kbf[p]) for p in probs}
    qk = {p: _dot_nt(q[p].astype(BF16), kbf[p]) for p in probs}
    a_list, rhs_list = [], []
    kd, qd, aqk, glast = {}, {}, {}, {}
    for p in probs:
        d, c, h = p
        dr = dirs[d]
        lane = ln(d, h)
        gcol = gc_col[d, c][:, lane:lane + 1]
        bcol = gall[d, c][:, GDN_AB // 2 + lane:GDN_AB // 2 + lane + 1]
        diff = gcol - gc_row[d, c][lane:lane + 1, :]
        e_strict = jnp.where(dr["strict"], jnp.exp(jnp.where(dr["strict"], diff, 0.0)), 0.0)
        e_incl = jnp.where(dr["incl"], jnp.exp(jnp.where(dr["incl"], diff, 0.0)), 0.0)
        eg = jnp.exp(gcol)
        glast[p] = gcol[dr["last"]:dr["last"] + 1, :]
        a_list.append(bcol * kk[p] * e_strict)
        v = dr["refs"][2][0, rows[c], cols[h]]
        rhs_list.append(jnp.concatenate([(bcol * eg) * k[p], bcol * v], axis=1))
        kd[p] = (k[p] * jnp.exp(glast[p] - gcol)).astype(BF16)
        qd[p] = (q[p] * eg).astype(BF16)
        aqk[p] = (qk[p] * e_incl).astype(BF16)
    sol = dict(zip(probs, _solve_unit_triangular(a_list, rhs_list, bd_mask)))

    chains = [(d, h) for d in range(2) for h in heads]
    for t in range(nchunk):
        cur = {(d, h): (d, dirs[d]["order"][t], h) for d, h in chains}
        s = {dh: s_ref[dh[0], dh[1]] for dh in chains}
        sb = {dh: s[dh].astype(BF16) for dh in chains}
        ws = {dh: _dot(sol[cur[dh]][:, :GDN_DK].astype(BF16), sb[dh]) for dh in chains}
        qs = {dh: _dot(qd[cur[dh]], sb[dh]) for dh in chains}
        u = {dh: (sol[cur[dh]][:, GDN_DK:] - ws[dh]).astype(BF16) for dh in chains}
        au = {dh: _dot(aqk[cur[dh]], u[dh]) for dh in chains}
        ku = {dh: _dot_tn(kd[cur[dh]], u[dh]) for dh in chains}
        for dh in chains:
            d, c, h = cur[dh]
            s_ref[d, h] = jnp.exp(glast[cur[dh]]) * s[dh] + ku[dh]
            dirs[d]["refs"][5][0, rows[c], cols[h]] = qs[dh] + au[dh]

    @pl.when(i == pl.num_programs(1) - 1)
    def _():
        sfin_ref[0] = s_ref[...]


def _gdn_scan_call(q, k, v, gb, gbt, s0, sc):
    b, n, _ = q.shape
    nsup = n // sc
    fwd = lambda bi, i: (bi, i, 0)
    bwd = lambda bi, i: (bi, nsup - 1 - i, 0)
    st = lambda bi, i: (bi, 0, 0, 0, 0)
    state = (1, 2, GDN_HEADS, GDN_DK, GDN_DV)

    def in_specs(row, tr):
        return [pl.BlockSpec((1, sc, GDN_QK), row), pl.BlockSpec((1, sc, GDN_QK), row),
                pl.BlockSpec((1, sc, GDN_V), row), pl.BlockSpec((1, sc, LANES), row),
                pl.BlockSpec((1, GDN_AB, sc), tr)]

    return pl.pallas_call(
        _gdn_scan_kernel,
        grid=(b, nsup),
        in_specs=(in_specs(fwd, lambda bi, i: (bi, 0, i))
                  + in_specs(bwd, lambda bi, i: (bi, 0, nsup - 1 - i))
                  + [pl.BlockSpec(state, st)]),
        out_specs=[pl.BlockSpec((1, sc, GDN_V), fwd), pl.BlockSpec((1, sc, GDN_V), bwd),
                   pl.BlockSpec(state, st)],
        out_shape=[jax.ShapeDtypeStruct((b, n, GDN_V), F32), jax.ShapeDtypeStruct((b, n, GDN_V), F32),
                   jax.ShapeDtypeStruct((b,) + state[1:], F32)],
        scratch_shapes=[pltpu.VMEM(state[1:], F32)],
        compiler_params=_cparams(("parallel", "arbitrary")),
        name="gdn_scan",
    )(q, k, v, gb, gbt, q, k, v, gb, gbt, s0)


def _outproj_kernel(oda_ref, of_ref, ob_ref, gate_ref, x_ref, gt1_ref, gng_ref, wout_ref,
                    n2g_ref, sh2_ref, sc2_ref, wr_ref, xnew_ref, ht_ref, aff_ref, afft_ref):
    og = of_ref[0] + ob_ref[0]
    gate = gate_ref[0]
    ys = []
    for h in range(GDN_HEADS):
        cols = slice(h * GDN_DV, (h + 1) * GDN_DV)
        t = og[:, cols]
        y = t * lax.rsqrt(jnp.mean(t * t, axis=-1, keepdims=True) + EPS) * gng_ref[...]
        ys.append((y * _silu(gate[:, cols])).astype(BF16))
    y_gdn = jnp.concatenate(ys, axis=1)
    proj = _dot(oda_ref[0], wout_ref[0:DA_V, :]) + _dot(y_gdn, wout_ref[DA_V:, :])
    xn = x_ref[0] + gt1_ref[0] * proj
    xnew_ref[0] = xn
    ms = jnp.mean(xn * xn, axis=-1, keepdims=True)
    hm = (xn * lax.rsqrt(ms + EPS) * n2g_ref[...]) * (1.0 + sc2_ref[0]) + sh2_ref[0]
    hmt = hm.T.astype(BF16)
    tb = ht_ref.shape[3]
    for c in range(ht_ref.shape[1]):
        ht_ref[0, c] = hmt[:, c * tb:(c + 1) * tb]
    logits = _dot_x3(hm, wr_ref[...])
    lane = lax.broadcasted_iota(jnp.int32, logits.shape, 1)
    logits = jnp.where(lane < N_EXPERTS, logits, NEG_BIG)
    e = jnp.exp(logits - jnp.max(logits, axis=-1, keepdims=True))
    aff = e / jnp.sum(e, axis=-1, keepdims=True)
    aff_ref[0] = aff
    afft_ref[0] = aff.T[:N_EXPERTS, :]


def _outproj_call(o_da, o_f, o_b, gate, x, gt1, gdn_norm_g, w_out_bf, norm2_g, sh2, sc2, wr_pad, tm, tb):
    b, n, d = x.shape
    assert tm % tb == 0
    row = lambda bi, i: (bi, i, 0)
    vec = lambda bi, i: (bi, 0, 0)
    const2 = lambda bi, i: (0, 0)
    return pl.pallas_call(
        _outproj_kernel,
        grid=(b, n // tm),
        in_specs=[pl.BlockSpec((1, tm, DA_V), row), pl.BlockSpec((1, tm, GDN_V), row),
                  pl.BlockSpec((1, tm, GDN_V), row), pl.BlockSpec((1, tm, GDN_V), row),
                  pl.BlockSpec((1, tm, d), row), pl.BlockSpec((1, 1, d), vec),
                  pl.BlockSpec((1, GDN_DV), const2), pl.BlockSpec((DA_V + GDN_V, d), const2),
                  pl.BlockSpec((1, d), const2), pl.BlockSpec((1, 1, d), vec), pl.BlockSpec((1, 1, d), vec),
                  pl.BlockSpec((d, LANES), const2)],
        out_specs=[pl.BlockSpec((1, tm, d), row),
                   pl.BlockSpec((1, tm // tb, d, tb), lambda bi, i: (bi, i, 0, 0)),
                   pl.BlockSpec((1, tm, LANES), row),
                   pl.BlockSpec((1, N_EXPERTS, tm), lambda bi, i: (bi, 0, i))],
        out_shape=[jax.ShapeDtypeStruct((b, n, d), F32), jax.ShapeDtypeStruct((b, n // tb, d, tb), BF16),
                   jax.ShapeDtypeStruct((b, n, LANES), F32),
                   jax.ShapeDtypeStruct((b, N_EXPERTS, n), F32)],
        compiler_params=_cparams(("parallel", "parallel")),
        name="outproj",
    )(o_da, o_f, o_b, gate, x, gt1, gdn_norm_g, w_out_bf, norm2_g, sh2, sc2, wr_pad)


GATHER_GROUP = 4
COMBINE_ALIGN = 16
PREFIX_BLK = 256
BISECT_STEPS = 64


def _select_kernel(afft_ref, aff_ref, post_ref, pos_ref, st_ref, rng_ref, *, cap, tb, tj):
    at = afft_ref[0]
    n = at.shape[1]

    def count_ge(t):
        return jnp.sum(jnp.where(at >= t, 1.0, 0.0), axis=1, keepdims=True)

    def bisect(_, bounds):
        lo, hi = bounds
        mid = 0.5 * (lo + hi)
        ok = count_ge(mid) >= cap
        return jnp.where(ok, mid, lo), jnp.where(ok, hi, mid)

    _, hi = lax.fori_loop(0, BISECT_STEPS, bisect,
                          (jnp.zeros((N_EXPERTS, 1), F32), jnp.full((N_EXPERTS, 1), 2.0, F32)))

    def below(h):
        return jnp.max(jnp.where(at < h, at, -1.0), axis=1, keepdims=True)

    def short(state):
        return jnp.sum(jnp.where(count_ge(state[0]) < cap, 1.0, 0.0)) > 0.0

    def step_down(state):
        t, h = state
        lacking = count_ge(t) < cap
        h = jnp.where(lacking, t, h)
        return jnp.where(lacking, below(h), t), h

    thr, _ = lax.while_loop(short, step_down, (below(hi), hi))
    need = cap - jnp.sum(jnp.where(at > thr, 1.0, 0.0), axis=1, keepdims=True)

    pi = lax.broadcasted_iota(jnp.int32, (PREFIX_BLK, PREFIX_BLK), 0)
    pj = lax.broadcasted_iota(jnp.int32, (PREFIX_BLK, PREFIX_BLK), 1)
    upper = jnp.where(pi <= pj, 1.0, 0.0).astype(BF16)
    lower = jnp.where(pi >= pj, 1.0, 0.0).astype(BF16)

    def prefix_lanes(m):
        carry = jnp.zeros((m.shape[0], 1), F32)
        outs = []
        for c in range(m.shape[1] // PREFIX_BLK):
            mc = m[:, c * PREFIX_BLK:(c + 1) * PREFIX_BLK]
            loc = _dot(mc.astype(BF16), upper)
            outs.append(loc - mc + carry)
            carry = carry + loc[:, PREFIX_BLK - 1:PREFIX_BLK]
        return jnp.concatenate(outs, axis=1)

    def prefix_rows(m):
        carry = jnp.zeros((1, m.shape[1]), F32)
        outs = []
        for c in range(m.shape[0] // PREFIX_BLK):
            mc = m[c * PREFIX_BLK:(c + 1) * PREFIX_BLK, :]
            loc = _dot(lower, mc.astype(BF16))
            outs.append(loc - mc + carry)
            carry = carry + loc[PREFIX_BLK - 1:PREFIX_BLK, :]
        return jnp.concatenate(outs, axis=0)

    eq_t = jnp.where(at == thr, 1.0, 0.0)
    sel_t = jnp.where(at > thr, 1.0, jnp.where(prefix_lanes(eq_t) < need, eq_t, 0.0))
    post_ref[0] = jnp.where(sel_t > 0.0, prefix_lanes(sel_t), -1.0)

    ti = lax.broadcasted_iota(jnp.int32, (n, LANES), 0)
    bi = lax.broadcasted_iota(jnp.int32, (n, LANES), 1)
    before = jnp.where(ti < bi * tb, 1.0, 0.0).astype(BF16)
    st = _dot(sel_t.astype(BF16), before)
    st_ref[0] = st.astype(jnp.int32)

    through = jnp.where(ti < (bi + 1) * tb, 1.0, 0.0).astype(BF16)
    st_end = _dot(sel_t.astype(BF16), through)
    blk_lane = lax.broadcasted_iota(jnp.int32, (N_EXPERTS, LANES), 1)
    is_blk = blk_lane < n // tb
    rng = jnp.zeros((N_EXPERTS, LANES), F32)
    ntile = cap // tj
    for j in range(ntile):
        first = jnp.sum(jnp.where(is_blk & (st_end <= j * tj), 1.0, 0.0), axis=1, keepdims=True)
        last = jnp.sum(jnp.where(is_blk & (st < (j + 1) * tj), 1.0, 0.0), axis=1, keepdims=True) - 1.0
        rng = rng + jnp.where(blk_lane == j, first, 0.0) + jnp.where(blk_lane == ntile + j, last, 0.0)
    rng_ref[0] = rng.astype(jnp.int32)

    a = aff_ref[0]
    er = lax.broadcasted_iota(jnp.int32, (N_EXPERTS, LANES), 0)
    ec = lax.broadcasted_iota(jnp.int32, (N_EXPERTS, LANES), 1)
    diag = er == ec
    thr_row = jnp.sum(jnp.where(diag, thr, 0.0), axis=0, keepdims=True)
    need_row = jnp.sum(jnp.where(diag, need, 0.0), axis=0, keepdims=True)
    valid = lax.broadcasted_iota(jnp.int32, a.shape, 1) < N_EXPERTS
    eq = jnp.where(valid & (a == thr_row), 1.0, 0.0)
    sel = jnp.where(valid & (a > thr_row), 1.0, jnp.where(prefix_rows(eq) < need_row, eq, 0.0))
    pos_ref[0] = jnp.where(sel > 0.0, prefix_rows(sel), -1.0)


def _select_call(afft, aff, cap, tb, tj):
    b, e, n = afft.shape
    assert n // tb + 1 <= LANES and 2 * (cap // tj) <= LANES
    return pl.pallas_call(
        functools.partial(_select_kernel, cap=cap, tb=tb, tj=tj),
        grid=(b,),
        in_specs=[pl.BlockSpec((1, e, n), lambda bi: (bi, 0, 0)),
                  pl.BlockSpec((1, n, LANES), lambda bi: (bi, 0, 0))],
        out_specs=[pl.BlockSpec((1, e, n), lambda bi: (bi, 0, 0)),
                   pl.BlockSpec((1, n, LANES), lambda bi: (bi, 0, 0)),
                   pl.BlockSpec((1, e, LANES), lambda bi: (bi, 0, 0)),
                   pl.BlockSpec((1, e, LANES), lambda bi: (bi, 0, 0))],
        out_shape=[jax.ShapeDtypeStruct((b, e, n), F32), jax.ShapeDtypeStruct((b, n, LANES), F32),
                   jax.ShapeDtypeStruct((b, e, LANES), jnp.int32),
                   jax.ShapeDtypeStruct((b, e, LANES), jnp.int32)],
        compiler_params=_cparams(("parallel",)),
        name="select",
    )(afft, aff)


def _moe_ffn_kernel(rng_ref, ht_ref, post_ref, afft_ref, wg_ref, wu_ref, wd_ref, ye_ref,
                    acc_ref, gacc_ref, wgb_ref, wub_ref, wdb_ref, *, tj, tb):
    b = pl.program_id(0)
    e = pl.program_id(1)
    cap = ye_ref.shape[2]
    ntile = cap // tj
    slot = lax.broadcasted_iota(jnp.int32, (tj, tb), 0).astype(F32)
    wgb_ref[...] = wg_ref[0].astype(BF16)
    wub_ref[...] = wu_ref[0].astype(BF16)
    wdb_ref[...] = wd_ref[0].astype(BF16)

    for j in range(cap // tj):
        lo = j * tj
        acc_ref[...] = jnp.zeros(acc_ref.shape, F32)
        gacc_ref[...] = jnp.zeros(gacc_ref.shape, F32)

        first = rng_ref[b, e, j]
        last = rng_ref[b, e, ntile + j]

        def group_body(g, carry):
            part = jnp.zeros(acc_ref.shape, F32)
            gate = jnp.zeros(gacc_ref.shape, F32)
            for u in range(GATHER_GROUP):
                bi = jnp.minimum(first + g * GATHER_GROUP + u, last)
                live = first + g * GATHER_GROUP + u <= last
                prow = post_ref[0, 0, pl.ds(bi, 1), :]
                hit = (prow == (slot + float(lo))) & live
                onehot = jnp.where(hit, 1.0, 0.0).astype(BF16)
                part = part + _dot_nt(ht_ref[0, bi], onehot)
                arow = afft_ref[0, 0, pl.ds(bi, 1), :]
                gate = gate + jnp.sum(jnp.where(hit, arow, 0.0), axis=1, keepdims=True)
            acc_ref[...] += part
            gacc_ref[...] += gate
            return carry

        lax.fori_loop(0, (last - first + GATHER_GROUP) // GATHER_GROUP, group_body, 0)
        xe = acc_ref[...].T.astype(BF16)
        hid = _silu(_dot(xe, wgb_ref[...])) * _dot(xe, wub_ref[...])
        ye = _dot(hid.astype(BF16), wdb_ref[...]) * gacc_ref[...]
        ye_ref[0, 0, lo:lo + tj, :] = ye.astype(ye_ref.dtype)


def _moe_ffn_call(tile_rng, ht4, post4, afft4, wg, wu, wd, cap, tj):
    b, nblk, d, tb = ht4.shape
    e, _, f = wg.shape
    wspec = lambda shp: pl.BlockSpec((1,) + shp, lambda bi, ei: (ei, 0, 0))
    return pl.pallas_call(
        functools.partial(_moe_ffn_kernel, tj=tj, tb=tb),
        grid=(b, e),
        in_specs=[pl.BlockSpec(memory_space=pltpu.SMEM),
                  pl.BlockSpec((1, nblk, d, tb), lambda bi, ei: (bi, 0, 0, 0), pipeline_mode=pl.Buffered(1)),
                  pl.BlockSpec((1, 1, nblk, tb), lambda bi, ei: (bi, ei, 0, 0)),
                  pl.BlockSpec((1, 1, nblk, tb), lambda bi, ei: (bi, ei, 0, 0)),
                  wspec((d, f)), wspec((d, f)), wspec((f, d))],
        out_specs=pl.BlockSpec((1, 1, cap, d), lambda bi, ei: (bi, ei, 0, 0)),
        out_shape=jax.ShapeDtypeStruct((b, e, cap, d), BF16),
        scratch_shapes=[pltpu.VMEM((d, tj), F32), pltpu.VMEM((tj, 1), F32),
                        pltpu.VMEM((d, f), BF16), pltpu.VMEM((d, f), BF16), pltpu.VMEM((f, d), BF16)],
        compiler_params=_cparams(("parallel", "arbitrary")),
        name="moe_ffn",
    )(tile_rng, ht4, post4, afft4, wg, wu, wd)


def _combine_kernel(st_ref, pos_ref, ye_ref, x_ref, gt2_ref, g_ref, o_ref, rest_ref, *, tb):
    b = pl.program_id(0)
    blk = pl.program_id(1)
    cap = ye_ref.shape[2]
    win = tb
    slot = lax.broadcasted_iota(jnp.int32, (tb, win), 1).astype(F32)
    pos = pos_ref[0]

    def window_base(e):
        return jnp.minimum(st_ref[b, e, blk] // COMBINE_ALIGN, (cap - win) // COMBINE_ALIGN) * COMBINE_ALIGN

    def window(e, start):
        return ye_ref[0, e, pl.ds(pl.multiple_of(start, COMBINE_ALIGN), win), :]

    moe = jnp.zeros((tb, ye_ref.shape[3]), F32)
    for e in range(N_EXPERTS):
        base = window_base(e)
        onehot = jnp.where(pos[:, e:e + 1] == slot + base.astype(F32), 1.0, 0.0).astype(BF16)
        moe = moe + _dot(onehot, window(e, base))

    rest_ref[...] = jnp.zeros(rest_ref.shape, F32)
    for e in range(N_EXPERTS):
        done = window_base(e) + win

        @pl.when(st_ref[b, e, blk + 1] > done)
        def _():
            start = jnp.minimum(done, cap - win)
            pcol = pos[:, e:e + 1]
            hit = (pcol == slot + start.astype(F32)) & (pcol >= done.astype(F32))
            rest_ref[...] += _dot(jnp.where(hit, 1.0, 0.0).astype(BF16), window(e, start))

    y = x_ref[0] + gt2_ref[0] * (moe + rest_ref[...])
    ms = jnp.mean(y * y, axis=-1, keepdims=True)
    o_ref[0] = y * lax.rsqrt(ms + EPS) * g_ref[...]


def _combine_call(starts, pos, ye, x_new, gt2, final_g, tb):
    b, e, cap, d = ye.shape
    n = pos.shape[1]
    assert cap % COMBINE_ALIGN == 0 and tb % COMBINE_ALIGN == 0 and cap >= tb
    row = lambda bi, i: (bi, i, 0)
    return pl.pallas_call(
        functools.partial(_combine_kernel, tb=tb),
        grid=(b, n // tb),
        in_specs=[pl.BlockSpec(memory_space=pltpu.SMEM),
                  pl.BlockSpec((1, tb, LANES), row),
                  pl.BlockSpec((1, e, cap, d), lambda bi, i: (bi, 0, 0, 0), pipeline_mode=pl.Buffered(1)),
                  pl.BlockSpec((1, tb, d), row),
                  pl.BlockSpec((1, 1, d), lambda bi, i: (bi, 0, 0)),
                  pl.BlockSpec((1, d), lambda bi, i: (0, 0))],
        out_specs=pl.BlockSpec((1, tb, d), row),
        out_shape=jax.ShapeDtypeStruct((b, n, d), F32),
        scratch_shapes=[pltpu.VMEM((tb, d), F32)],
        compiler_params=_cparams(("parallel", "arbitrary")),
        name="combine",
    )(starts, pos, ye, x_new, gt2, final_g)


def _rope_tables(n):
    t = np.arange(n)
    rows = (t // GRID_W).astype(np.float64)
    cols = (t % GRID_W).astype(np.float64)
    inv_freq = np.power(ROPE_THETA, -np.arange(0, ROPE_AXIS_DIM, 2, dtype=np.float64) / ROPE_AXIS_DIM)
    ang_row = rows[:, None] * inv_freq[None, :]
    ang_col = cols[:, None] * inv_freq[None, :]

    def axis_tables(ang):
        c = np.cos(ang).astype(np.float32)
        s = np.sin(ang).astype(np.float32)
        return np.concatenate([c, c], axis=1), np.concatenate([-s, s], axis=1)

    cr, sr = axis_tables(ang_row)
    cc, sc = axis_tables(ang_col)
    cos64 = np.concatenate([cr, cc], axis=1)
    sin64 = np.concatenate([sr, sc], axis=1)
    return (jnp.asarray(np.concatenate([cos64, cos64], axis=1)),
            jnp.asarray(np.concatenate([sin64, sin64], axis=1)))


def _pad_lanes(v):
    return jnp.pad(v.reshape(1, -1), ((0, 0), (0, LANES - v.size)))


def kernel(x, c, ctx, c_ctx, w_mod, b_mod, norm1_g, w_in, conv_w, a_log, dt_bias, gdn_norm_g,
           lam_q1, lam_k1, lam_q2, lam_k2, da_subln_g, w_out, norm2_g,
           w_router, w_gate, w_up, w_down, final_g):
    b, n, d = x.shape
    nc = ctx.shape[1]
    layer = 0

    cvec = jnp.concatenate([c, c_ctx[None, :], jnp.zeros((8 - b - 1, d), F32)], axis=0)
    mod = _mod_call(cvec, w_mod[layer], b_mod[layer])
    sh1, sc1, gt1, sh2, sc2, gt2 = [mod[:b, i * d:(i + 1) * d].reshape(b, 1, d) for i in range(6)]
    sh1c = jnp.broadcast_to(mod[b:b + 1, 0:d].reshape(1, 1, d), (b, 1, d))
    sc1c = jnp.broadcast_to(mod[b:b + 1, d:2 * d].reshape(1, 1, d), (b, 1, d))

    w_in_bf = jnp.pad(w_in[layer].astype(BF16), ((0, 0), (0, IN_COLS_PAD - w_in.shape[2])))
    g1 = norm1_g[layer].reshape(1, d)
    cos_l, sin_l = _rope_tables(n)
    cos_c, sin_c = jnp.ones((nc, LANES), F32), jnp.zeros((nc, LANES), F32)
    conv_w8 = jnp.pad(conv_w[layer], ((0, 8 - conv_w.shape[1]), (0, 0)))
    alog_row = _pad_lanes(a_log[layer])
    dtb_row = _pad_lanes(dt_bias[layer])
    gdn_args = (conv_w8, alog_row, dtb_row)
    q, k, vt, ql, kl, vl, gate, gbl, gbtl = _inproj_call(
        x, g1, sh1, sc1, cos_l, sin_l, w_in_bf, *gdn_args, tm=512)
    _, kc, vct, qc, kcg, vcg, _, gbc, gbtc = _inproj_call(
        ctx, g1, sh1c, sc1c, cos_c, sin_c, w_in_bf, *gdn_args, tm=nc)

    lam_vecs = jnp.stack([lam_q1[layer], lam_k1[layer], lam_q2[layer], lam_k2[layer]], axis=0)
    tk = vt.shape[-1]
    reps = tk // nc
    k_all = jnp.concatenate([kc] * reps + [k], axis=1)
    vct_pad = jnp.pad(vct, ((0, 0), (0, 0), (0, 0), (0, 0), (0, tk - nc)))
    vt_all = jnp.concatenate([vct_pad, vt], axis=2)
    o_da = _attn_call(q, k_all, vt_all, lam_vecs, da_subln_g[layer].reshape(1, DA_DV), tq=2048)

    zeros_state = jnp.zeros((b, 2, GDN_HEADS, GDN_DK, GDN_DV), F32)
    _, _, s_ctx = _gdn_scan_call(qc, kcg, vcg, gbc, gbtc, zeros_state, sc=nc)
    o_fwd, o_bwd, _ = _gdn_scan_call(ql, kl, vl, gbl, gbtl, s_ctx, sc=256)
    o_dirs = [o_fwd, o_bwd]

    wr_pad = jnp.pad(w_router[layer], ((0, 0), (0, LANES - N_EXPERTS)))
    cap = CAP_FACTOR * n // N_EXPERTS
    tb = 256
    tj = 256
    x_new, ht4, aff, afft = _outproj_call(
        o_da, o_dirs[0], o_dirs[1], gate, x, gt1, gdn_norm_g[layer].reshape(1, GDN_DV),
        w_out[layer].astype(BF16), norm2_g[layer].reshape(1, d), sh2, sc2, wr_pad, tm=512, tb=tb)

    post, pos, starts, tile_rng = _select_call(afft, aff, cap, tb, tj)
    nblk = n // tb
    ye = _moe_ffn_call(tile_rng, ht4, post.reshape(b, N_EXPERTS, nblk, tb),
                       afft.reshape(b, N_EXPERTS, nblk, tb),
                       w_gate[layer], w_up[layer], w_down[layer], cap, tj)
    return _combine_call(starts, pos, ye, x_new, gt2, final_g.reshape(1, d), tb)
```

```python
import functools
import math

import jax
import jax.numpy as jnp
import numpy as np
from jax import lax
from jax.experimental import pallas as pl
from jax.experimental.pallas import tpu as pltpu

F32 = jnp.float32
BF16 = jnp.bfloat16

D_MODEL = 1024
GRID_W = 64
EPS = 1e-6
DA_HEADS = 4
DA_D = 64
DA_DV = 2 * DA_D
ROPE_AXIS_DIM = DA_D // 2
ROPE_THETA = 10000.0
GDN_HEADS = 4
GDN_DK = 128
GDN_DV = 128
CHUNK = 64
N_EXPERTS = 16
CAP_FACTOR = 2
LAM_INIT = 0.8 - 0.6 * math.exp(-0.3 * 0)

DA_QK = DA_HEADS * 2 * DA_D
DA_V = DA_HEADS * DA_DV
GDN_QK = GDN_HEADS * GDN_DK
GDN_V = GDN_HEADS * GDN_DV
GDN_QKV = 2 * GDN_QK + GDN_V
GDN_AB = 2 * 2 * GDN_HEADS
COL_Q, COL_K, COL_V = 0, DA_QK, 2 * DA_QK
COL_QKV = 2 * DA_QK + DA_V
COL_GATE = COL_QKV + GDN_QKV
COL_AB = COL_GATE + GDN_V
LANES = 128
IN_COLS_PAD = COL_AB + LANES
MXU_WIDTH = 256
ATTN_COL_TILE = MXU_WIDTH
VT_ROWS = DA_DV + 8
VT_BLK = MXU_WIDTH

VMEM_LIMIT = 56 * 1024 * 1024
NEG_BIG = -1e30
LOG2_E = math.log2(math.e)


def _cparams(sem):
    return pltpu.CompilerParams(dimension_semantics=sem, vmem_limit_bytes=VMEM_LIMIT)


def _sigmoid(x):
    return 1.0 / (1.0 + jnp.exp(-x))


def _silu(x):
    return x * _sigmoid(x)


def _split3(a):
    a1 = a.astype(BF16)
    r1 = a - a1.astype(F32)
    a2 = r1.astype(BF16)
    a3 = (r1 - a2.astype(F32)).astype(BF16)
    return a1, a2, a3


def _dot(a, b):
    return jnp.dot(a, b, preferred_element_type=F32)


def _dot_x3(a, b):
    a1, a2, _ = _split3(a)
    b1, b2, _ = _split3(b)
    return _dot(a1, b1) + (_dot(a1, b2) + _dot(a2, b1))


def _dot_nt(a, b):
    return lax.dot_general(a, b, (((1,), (1,)), ((), ())), preferred_element_type=F32)


def _dot_tn(a, b):
    return lax.dot_general(a, b, (((0,), (0,)), ((), ())), preferred_element_type=F32)


def _mod_kernel(c_ref, w_ref, b_ref, o_ref):
    s = _silu(c_ref[...])
    o_ref[...] = jnp.dot(s, w_ref[...], precision=lax.Precision.HIGHEST,
                         preferred_element_type=F32) + b_ref[...]


def _mod_call(cvec, w_mod, b_mod):
    d, n = w_mod.shape
    tn = 1024
    return pl.pallas_call(
        _mod_kernel,
        grid=(n // tn,),
        in_specs=[pl.BlockSpec((8, d), lambda j: (0, 0)),
                  pl.BlockSpec((d, tn), lambda j: (0, j)),
                  pl.BlockSpec((1, tn), lambda j: (0, j))],
        out_specs=pl.BlockSpec((8, tn), lambda j: (0, j)),
        out_shape=jax.ShapeDtypeStruct((8, n), F32),
        compiler_params=_cparams(("arbitrary",)),
        name="mod",
    )(cvec, w_mod, b_mod.reshape(1, n))


def _inproj_kernel(x_ref, xp_ref, xn_ref, g_ref, sh_ref, sc_ref, cos_ref, sin_ref, w_ref,
                   cw_ref, al_ref, dtb_ref,
                   q_ref, k_ref, vt_ref, gq_ref, gk_ref, gv_ref, gate_ref, gb_ref, gbt_ref):
    i = pl.program_id(1)
    nblk = pl.num_programs(1)
    tm = x_ref.shape[1]

    def modulated(xb):
        ms = jnp.mean(xb * xb, axis=-1, keepdims=True)
        hm = (xb * lax.rsqrt(ms + EPS) * g_ref[...]) * (1.0 + sc_ref[0]) + sh_ref[0]
        return hm.astype(BF16)

    hb = modulated(x_ref[0])

    def proj(lo, hi):
        return _dot(hb, w_ref[:, lo:hi])

    cos = cos_ref[...]
    sin = sin_ref[...]
    lane = lax.broadcasted_iota(jnp.int32, (tm, LANES), 1)
    first_half = (lane % ROPE_AXIS_DIM) < (ROPE_AXIS_DIM // 2)

    def rope(t):
        outs = []
        for j in range(t.shape[1] // LANES):
            s = t[:, LANES * j:LANES * (j + 1)]
            partner = jnp.where(first_half,
                                pltpu.roll(s, LANES - ROPE_AXIS_DIM // 2, 1),
                                pltpu.roll(s, ROPE_AXIS_DIM // 2, 1))
            outs.append(s * cos + partner * sin)
        return jnp.concatenate(outs, axis=1)

    q_ref[0] = (rope(proj(COL_Q, COL_K)) * (DA_D ** -0.5 * LOG2_E)).astype(BF16)
    k_ref[0] = rope(proj(COL_K, COL_V)).astype(BF16)
    vt = proj(COL_V, COL_QKV).T.astype(BF16).reshape(DA_HEADS, DA_DV, tm)
    extra = lax.broadcasted_iota(jnp.int32, (DA_HEADS, VT_ROWS - DA_DV, VT_BLK), 1)
    ones_row = jnp.where(extra == 0, 1.0, 0.0).astype(BF16)
    for c in range(tm // VT_BLK):
        vt_ref[0, :, c, 0:DA_DV, :] = vt[:, :, c * VT_BLK:(c + 1) * VT_BLK]
        vt_ref[0, :, c, DA_DV:VT_ROWS, :] = ones_row
    gate_ref[0] = proj(COL_GATE, COL_AB)

    qkv = proj(COL_QKV, COL_GATE)
    halo = _dot(modulated(jnp.concatenate([xp_ref[0], xn_ref[0]], axis=0)),
                w_ref[:, COL_QKV:COL_GATE])
    prev = jnp.where(i > 0, halo[7:8, :], 0.0)
    nxt = jnp.where(i < nblk - 1, halo[8:9, :], 0.0)
    row = lax.broadcasted_iota(jnp.int32, (tm, 1), 0)
    qkv_m1 = jnp.where(row == 0, prev, pltpu.roll(qkv, 1, 0))
    qkv_p1 = jnp.where(row == tm - 1, nxt, pltpu.roll(qkv, tm - 1, 0))
    cw = cw_ref[...]
    s = _silu(qkv_m1 * cw[0:1] + qkv * cw[1:2] + qkv_p1 * cw[2:3])

    def l2n(t):
        return t * lax.rsqrt(jnp.sum(t * t, axis=-1, keepdims=True) + EPS)

    for h in range(GDN_HEADS):
        lo, hi = h * GDN_DK, (h + 1) * GDN_DK
        gq_ref[0, :, lo:hi] = l2n(s[:, lo:hi]) * GDN_DK ** -0.5
        gk_ref[0, :, lo:hi] = l2n(s[:, GDN_QK + lo:GDN_QK + hi])
    gv_ref[0] = s[:, 2 * GDN_QK:]

    ab = proj(COL_AB, IN_COLS_PAD)
    z = ab + dtb_ref[...]
    softplus = jnp.maximum(z, 0.0) + jnp.log1p(jnp.exp(-jnp.abs(z)))
    gval = -jnp.exp(al_ref[...]) * softplus
    gb = jnp.where(lane < GDN_AB // 2, gval, jnp.where(lane < GDN_AB, _sigmoid(ab), 0.0))
    gb_ref[0] = gb
    gbt_ref[0] = gb.T[:GDN_AB, :]


def _inproj_call(x, norm_g, shift, scale, cos, sin, w_bf, conv_w8, alog_row, dtb_row, tm):
    b, n, d = x.shape
    nb8 = n // 8
    step8 = tm // 8
    row = lambda bi, i: (bi, i, 0)
    vec = lambda bi, i: (bi, 0, 0)
    const = lambda bi, i: (0, 0)
    rows_out = [(DA_QK, BF16), (DA_QK, BF16), None, (GDN_QK, F32), (GDN_QK, F32), (GDN_V, F32),
                (GDN_V, F32), (LANES, F32)]
    out_specs = [pl.BlockSpec((1, tm, o[0]), row) if o else
                 pl.BlockSpec((1, DA_HEADS, tm // VT_BLK, VT_ROWS, VT_BLK), lambda bi, i: (bi, 0, i, 0, 0))
                 for o in rows_out]
    out_shape = [jax.ShapeDtypeStruct((b, n, o[0]), o[1]) if o else
                 jax.ShapeDtypeStruct((b, DA_HEADS, n // VT_BLK, VT_ROWS, VT_BLK), BF16) for o in rows_out]
    out_specs.append(pl.BlockSpec((1, GDN_AB, tm), lambda bi, i: (bi, 0, i)))
    out_shape.append(jax.ShapeDtypeStruct((b, GDN_AB, n), F32))
    return pl.pallas_call(
        _inproj_kernel,
        grid=(b, n // tm),
        in_specs=[pl.BlockSpec((1, tm, d), row),
                  pl.BlockSpec((1, 8, d), lambda bi, i: (bi, jnp.maximum(i * step8 - 1, 0), 0)),
                  pl.BlockSpec((1, 8, d), lambda bi, i: (bi, jnp.minimum((i + 1) * step8, nb8 - 1), 0)),
                  pl.BlockSpec((1, d), const),
                  pl.BlockSpec((1, 1, d), vec),
                  pl.BlockSpec((1, 1, d), vec),
                  pl.BlockSpec((tm, LANES), lambda bi, i: (i, 0)),
                  pl.BlockSpec((tm, LANES), lambda bi, i: (i, 0)),
                  pl.BlockSpec((d, IN_COLS_PAD), const),
                  pl.BlockSpec((8, GDN_QKV), const),
                  pl.BlockSpec((1, LANES), const),
                  pl.BlockSpec((1, LANES), const)],
        out_specs=out_specs,
        out_shape=out_shape,
        compiler_params=_cparams(("parallel", "parallel")),
        name="inproj",
    )(x, x, x, norm_g, shift, scale, cos, sin, w_bf, conv_w8, alog_row, dtb_row)


def _attn_kernel(q_ref, k_ref, vt_ref, lam_ref, g_ref, o_ref,
                 q2_ref, acc_ref, m_ref, s_ref, p_ref, a_ref, *, tk):
    q = q_ref[0]
    tq = q.shape[0]
    lane = lax.broadcasted_iota(jnp.int32, (tq, LANES), 1)
    zero = jnp.zeros_like(q)
    q2_ref[0:tq, :] = jnp.where(lane < DA_D, q, zero)
    q2_ref[tq:2 * tq, :] = jnp.where(lane >= DA_D, q, zero)
    m_ref[...] = jnp.full(m_ref.shape, NEG_BIG, F32)
    acc_ref[...] = jnp.zeros(acc_ref.shape, F32)

    nct = 2 * tq // ATTN_COL_TILE
    col = lambda ct: slice(ct * ATTN_COL_TILE, (ct + 1) * ATTN_COL_TILE)

    def scores(kk, ct):
        return _dot_nt(kk, q2_ref[col(ct), :])

    def softmax_update(ct, s):
        cols = col(ct)
        m_old = m_ref[:, cols]
        m_new = jnp.maximum(m_old, jnp.max(s, axis=0, keepdims=True))
        alpha = jnp.exp2(m_old - m_new)
        p = jnp.exp2(s - m_new)
        m_ref[:, cols] = m_new
        return p.astype(BF16), alpha

    vblk = vt_ref.shape[4]
    nsub = tk // vblk

    def value_update(ct, j, p, alpha):
        cols = col(ct)
        pv = _dot(vt_ref[0, 0, j * nsub], p[0:vblk, :])
        for c in range(1, nsub):
            pv = pv + _dot(vt_ref[0, 0, j * nsub + c], p[c * vblk:(c + 1) * vblk, :])
        acc_ref[:, cols] = alpha * acc_ref[:, cols] + pv

    nchunk = k_ref.shape[1] // tk
    last = nct - 1
    s_ref[...] = scores(k_ref[0, 0:tk, :], 0)
    p_ref[...] = jnp.zeros(p_ref.shape, BF16)
    a_ref[...] = jnp.ones(a_ref.shape, F32)

    def body(j, carry):
        kk = k_ref[0, pl.ds(pl.multiple_of(j * tk, vblk), tk), :]
        s_cur = s_ref[...]
        pending = (last, jnp.maximum(j - 1, 0), p_ref[...], a_ref[...])
        for ct in range(nct):
            if ct < last:
                s_ahead = scores(kk, ct + 1)
            else:
                jn = jnp.minimum(j + 1, nchunk - 1)
                s_ahead = scores(k_ref[0, pl.ds(pl.multiple_of(jn * tk, vblk), tk), :], 0)
            p, alpha = softmax_update(ct, s_cur)
            value_update(*pending)
            pending = (ct, j, p, alpha)
            s_cur = s_ahead
        s_ref[...] = s_cur
        p_ref[...] = pending[2]
        a_ref[...] = pending[3]
        return carry

    lax.fori_loop(0, nchunk, body, 0)
    value_update(last, nchunk - 1, p_ref[...], a_ref[...])

    lv = lam_ref[...]
    lam = (jnp.exp(jnp.sum(lv[0:1] * lv[1:2], axis=1, keepdims=True))
           - jnp.exp(jnp.sum(lv[2:3] * lv[3:4], axis=1, keepdims=True)) + LAM_INIT)
    den = acc_ref[DA_DV:DA_DV + 1, :]
    ot = (acc_ref[0:DA_DV, 0:tq] / den[:, 0:tq]
          - lam * (acc_ref[0:DA_DV, tq:2 * tq] / den[:, tq:2 * tq]))
    o = ot.T
    ms = jnp.mean(o * o, axis=-1, keepdims=True)
    o_ref[0] = ((o * lax.rsqrt(ms + EPS) * g_ref[...]) * (1.0 - LAM_INIT)).astype(o_ref.dtype)


def _attn_call(q, k_all, vt_all, lam_vecs, subln_g, tq, tk):
    b, n, _ = q.shape
    nk = k_all.shape[1]
    _, _, nvt, _, vblk = vt_all.shape
    assert nk == nvt * vblk and tk % vblk == 0 and nk % tk == 0
    return pl.pallas_call(
        functools.partial(_attn_kernel, tk=tk),
        grid=(b, DA_HEADS, n // tq),
        in_specs=[pl.BlockSpec((1, tq, LANES), lambda bi, h, i: (bi, i, h)),
                  pl.BlockSpec((1, nk, LANES), lambda bi, h, i: (bi, 0, h)),
                  pl.BlockSpec((1, 1, nvt, VT_ROWS, vblk), lambda bi, h, i: (bi, h, 0, 0, 0)),
                  pl.BlockSpec((4, DA_D), lambda bi, h, i: (0, 0)),
                  pl.BlockSpec((1, DA_DV), lambda bi, h, i: (0, 0))],
        out_specs=pl.BlockSpec((1, tq, LANES), lambda bi, h, i: (bi, i, h)),
        out_shape=jax.ShapeDtypeStruct((b, n, DA_V), BF16),
        scratch_shapes=[pltpu.VMEM((2 * tq, LANES), BF16), pltpu.VMEM((VT_ROWS, 2 * tq), F32),
                        pltpu.VMEM((1, 2 * tq), F32),
                        pltpu.VMEM((tk, ATTN_COL_TILE), F32), pltpu.VMEM((tk, ATTN_COL_TILE), BF16),
                        pltpu.VMEM((1, ATTN_COL_TILE), F32)],
        compiler_params=_cparams(("parallel", "parallel", "parallel")),
        name="attn",
    )(q, k_all, vt_all, lam_vecs, subln_g)


def _solve_unit_triangular(a_list, rhs_list, bd_mask):
    def mm(xs, ys):
        return [_dot(x.astype(BF16), y.astype(BF16)) for x, y in zip(xs, ys)]

    d = [jnp.where(bd_mask, a, 0.0) for a in a_list]
    n = [a - di for a, di in zip(a_list, d)]
    d2 = mm(d, d)
    dd2 = mm(d, d2)
    p = [(-di + d2i) - t for di, d2i, t in zip(d, d2, dd2)]
    d4 = mm(d2, d2)
    pd4 = mm(p, d4)
    p = [pi + d4i + t for pi, d4i, t in zip(p, d4, pd4)]
    d8 = mm(d4, d4)
    pd8 = mm(p, d8)
    p = [pi + d8i + t for pi, d8i, t in zip(p, d8, pd8)]
    m = [ni + t for ni, t in zip(n, mm(p, n))]
    z = [ri + t for ri, t in zip(rhs_list, mm(p, rhs_list))]
    m2 = mm(m, m)
    z = [zi + t for zi, t in zip(z, mm(m2, z))]
    return [zi - t for zi, t in zip(z, mm(m, z))]


def _gdn_scan_kernel(qf_ref, kf_ref, vf_ref, gbf_ref, gbtf_ref, qb_ref, kb_ref, vb_ref, gbb_ref, gbtb_ref,
                     s0_ref, of_ref, ob_ref, sfin_ref, s_ref):
    i = pl.program_id(1)

    @pl.when(i == 0)
    def _():
        s_ref[...] = s0_ref[0]

    c64 = CHUNK
    nchunk = qf_ref.shape[1] // c64
    ri = lax.broadcasted_iota(jnp.int32, (c64, c64), 0)
    ci = lax.broadcasted_iota(jnp.int32, (c64, c64), 1)
    bd_mask = (ri // 16) == (ci // 16)
    lower_incl, upper_incl = ri >= ci, ri <= ci
    heads = range(GDN_HEADS)
    rows = [slice(c * c64, (c + 1) * c64) for c in range(nchunk)]
    cols = [slice(h * GDN_DK, (h + 1) * GDN_DK) for h in heads]

    dirs = []
    for d, refs in enumerate(((qf_ref, kf_ref, vf_ref, gbf_ref, gbtf_ref, of_ref),
                              (qb_ref, kb_ref, vb_ref, gbb_ref, gbtb_ref, ob_ref))):
        rev = d == 1
        incl = upper_incl if rev else lower_incl
        dirs.append(dict(
            refs=refs, incl=incl, strict=(ri < ci) if rev else (ri > ci),
            tri_c=jnp.where(incl, 1.0, 0.0).astype(BF16),
            tri_r=jnp.where(lower_incl if rev else upper_incl, 1.0, 0.0).astype(BF16),
            last=0 if rev else c64 - 1,
            order=list(range(nchunk - 1, -1, -1) if rev else range(nchunk))))

    gall, gc_col, gc_row = {}, {}, {}
    for d, dr in enumerate(dirs):
        gb_ref, gbt_ref = dr["refs"][3], dr["refs"][4]
        for c in range(nchunk):
            gall[d, c] = gb_ref[0, rows[c], :]
            g1, g2, g3 = _split3(gall[d, c])
            tc, tr = dr["tri_c"], dr["tri_r"]
            gc_col[d, c] = _dot(tc, g1) + _dot(tc, g2) + _dot(tc, g3)
            t1, t2, t3 = _split3(gbt_ref[0, :, rows[c]])
            gc_row[d, c] = _dot(t1, tr) + _dot(t2, tr) + _dot(t3, tr)

    probs = [(d, c, h) for d in range(2) for c in range(nchunk) for h in heads]
    ln = lambda d, h: d * GDN_HEADS + h
    k = {(d, c, h): dirs[d]["refs"][1][0, rows[c], cols[h]] for d, c, h in probs}
    q = {(d, c, h): dirs[d]["refs"][0][0, rows[c], cols[h]] for d, c, h in probs}
    kbf = {p: k[p].astype(BF16) for p in probs}
    kk = {p: _dot_nt(kbf[p], kbf[p]) for p in probs}
    qk = {p: _dot_nt(q[p].astype(BF16), kbf[p]) for p in probs}
    a_list, rhs_list = [], []
    kd, qd, aqk, glast = {}, {}, {}, {}
    for p in probs:
        d, c, h = p
        dr = dirs[d]
        lane = ln(d, h)
        gcol = gc_col[d, c][:, lane:lane + 1]
        bcol = gall[d, c][:, GDN_AB // 2 + lane:GDN_AB // 2 + lane + 1]
        diff = gcol - gc_row[d, c][lane:lane + 1, :]
        e_strict = jnp.where(dr["strict"], jnp.exp(jnp.where(dr["strict"], diff, 0.0)), 0.0)
        e_incl = jnp.where(dr["incl"], jnp.exp(jnp.where(dr["incl"], diff, 0.0)), 0.0)
        eg = jnp.exp(gcol)
        glast[p] = gcol[dr["last"]:dr["last"] + 1, :]
        a_list.append(bcol * kk[p] * e_strict)
        v = dr["refs"][2][0, rows[c], cols[h]]
        rhs_list.append(jnp.concatenate([(bcol * eg) * k[p], bcol * v], axis=1))
        kd[p] = (k[p] * jnp.exp(glast[p] - gcol)).astype(BF16)
        qd[p] = (q[p] * eg).astype(BF16)
        aqk[p] = (qk[p] * e_incl).astype(BF16)
    sol = dict(zip(probs, _solve_unit_triangular(a_list, rhs_list, bd_mask)))

    chains = [(d, h) for d in range(2) for h in heads]
    for t in range(nchunk):
        cur = {(d, h): (d, dirs[d]["order"][t], h) for d, h in chains}
        s = {dh: s_ref[dh[0], dh[1]] for dh in chains}
        sb = {dh: s[dh].astype(BF16) for dh in chains}
        ws = {dh: _dot(sol[cur[dh]][:, :GDN_DK].astype(BF16), sb[dh]) for dh in chains}
        qs = {dh: _dot(qd[cur[dh]], sb[dh]) for dh in chains}
        u = {dh: (sol[cur[dh]][:, GDN_DK:] - ws[dh]).astype(BF16) for dh in chains}
        au = {dh: _dot(aqk[cur[dh]], u[dh]) for dh in chains}
        ku = {dh: _dot_tn(kd[cur[dh]], u[dh]) for dh in chains}
        for dh in chains:
            d, c, h = cur[dh]
            s_ref[d, h] = jnp.exp(glast[cur[dh]]) * s[dh] + ku[dh]
            dirs[d]["refs"][5][0, rows[c], cols[h]] = qs[dh] + au[dh]

    @pl.when(i == pl.num_programs(1) - 1)
    def _():
        sfin_ref[0] = s_ref[...]


def _gdn_scan_call(q, k, v, gb, gbt, s0, sc):
    b, n, _ = q.shape
    nsup = n // sc
    fwd = lambda bi, i: (bi, i, 0)
    bwd = lambda bi, i: (bi, nsup - 1 - i, 0)
    st = lambda bi, i: (bi, 0, 0, 0, 0)
    state = (1, 2, GDN_HEADS, GDN_DK, GDN_DV)

    def in_specs(row, tr):
        return [pl.BlockSpec((1, sc, GDN_QK), row), pl.BlockSpec((1, sc, GDN_QK), row),
                pl.BlockSpec((1, sc, GDN_V), row), pl.BlockSpec((1, sc, LANES), row),
                pl.BlockSpec((1, GDN_AB, sc), tr)]

    return pl.pallas_call(
        _gdn_scan_kernel,
        grid=(b, nsup),
        in_specs=(in_specs(fwd, lambda bi, i: (bi, 0, i))
                  + in_specs(bwd, lambda bi, i: (bi, 0, nsup - 1 - i))
                  + [pl.BlockSpec(state, st)]),
        out_specs=[pl.BlockSpec((1, sc, GDN_V), fwd), pl.BlockSpec((1, sc, GDN_V), bwd),
                   pl.BlockSpec(state, st)],
        out_shape=[jax.ShapeDtypeStruct((b, n, GDN_V), F32), jax.ShapeDtypeStruct((b, n, GDN_V), F32),
                   jax.ShapeDtypeStruct((b,) + state[1:], F32)],
        scratch_shapes=[pltpu.VMEM(state[1:], F32)],
        compiler_params=_cparams(("parallel", "arbitrary")),
        name="gdn_scan",
    )(q, k, v, gb, gbt, q, k, v, gb, gbt, s0)


def _outproj_kernel(oda_ref, of_ref, ob_ref, gate_ref, x_ref, gt1_ref, gng_ref, wout_ref,
                    n2g_ref, sh2_ref, sc2_ref, wr_ref, xnew_ref, ht_ref, aff_ref, afft_ref):
    og = of_ref[0] + ob_ref[0]
    gate = gate_ref[0]
    ys = []
    for h in range(GDN_HEADS):
        cols = slice(h * GDN_DV, (h + 1) * GDN_DV)
        t = og[:, cols]
        y = t * lax.rsqrt(jnp.mean(t * t, axis=-1, keepdims=True) + EPS) * gng_ref[...]
        ys.append((y * _silu(gate[:, cols])).astype(BF16))
    y_gdn = jnp.concatenate(ys, axis=1)
    proj = _dot(oda_ref[0], wout_ref[0:DA_V, :]) + _dot(y_gdn, wout_ref[DA_V:, :])
    xn = x_ref[0] + gt1_ref[0] * proj
    xnew_ref[0] = xn
    ms = jnp.mean(xn * xn, axis=-1, keepdims=True)
    hm = (xn * lax.rsqrt(ms + EPS) * n2g_ref[...]) * (1.0 + sc2_ref[0]) + sh2_ref[0]
    hmt = hm.T.astype(BF16)
    tb = ht_ref.shape[3]
    for c in range(ht_ref.shape[1]):
        ht_ref[0, c] = hmt[:, c * tb:(c + 1) * tb]
    logits = _dot_x3(hm, wr_ref[...])
    lane = lax.broadcasted_iota(jnp.int32, logits.shape, 1)
    logits = jnp.where(lane < N_EXPERTS, logits, NEG_BIG)
    e = jnp.exp(logits - jnp.max(logits, axis=-1, keepdims=True))
    aff = e / jnp.sum(e, axis=-1, keepdims=True)
    aff_ref[0] = aff
    afft_ref[0] = aff.T[:N_EXPERTS, :]


def _outproj_call(o_da, o_f, o_b, gate, x, gt1, gdn_norm_g, w_out_bf, norm2_g, sh2, sc2, wr_pad, tm, tb):
    b, n, d = x.shape
    assert tm % tb == 0
    row = lambda bi, i: (bi, i, 0)
    vec = lambda bi, i: (bi, 0, 0)
    const2 = lambda bi, i: (0, 0)
    return pl.pallas_call(
        _outproj_kernel,
        grid=(b, n // tm),
        in_specs=[pl.BlockSpec((1, tm, DA_V), row), pl.BlockSpec((1, tm, GDN_V), row),
                  pl.BlockSpec((1, tm, GDN_V), row), pl.BlockSpec((1, tm, GDN_V), row),
                  pl.BlockSpec((1, tm, d), row), pl.BlockSpec((1, 1, d), vec),
                  pl.BlockSpec((1, GDN_DV), const2), pl.BlockSpec((DA_V + GDN_V, d), const2),
                  pl.BlockSpec((1, d), const2), pl.BlockSpec((1, 1, d), vec), pl.BlockSpec((1, 1, d), vec),
                  pl.BlockSpec((d, LANES), const2)],
        out_specs=[pl.BlockSpec((1, tm, d), row),
                   pl.BlockSpec((1, tm // tb, d, tb), lambda bi, i: (bi, i, 0, 0)),
                   pl.BlockSpec((1, tm, LANES), row),
                   pl.BlockSpec((1, N_EXPERTS, tm), lambda bi, i: (bi, 0, i))],
        out_shape=[jax.ShapeDtypeStruct((b, n, d), F32), jax.ShapeDtypeStruct((b, n // tb, d, tb), BF16),
                   jax.ShapeDtypeStruct((b, n, LANES), F32),
                   jax.ShapeDtypeStruct((b, N_EXPERTS, n), F32)],
        compiler_params=_cparams(("parallel", "parallel")),
        name="outproj",
    )(o_da, o_f, o_b, gate, x, gt1, gdn_norm_g, w_out_bf, norm2_g, sh2, sc2, wr_pad)


GATHER_GROUP = 4
COMBINE_ALIGN = 16
PREFIX_BLK = 256
BISECT_STEPS = 64


def _select_kernel(afft_ref, aff_ref, post_ref, pos_ref, st_ref, rng_ref, *, cap, tb, tj):
    at = afft_ref[0]
    n = at.shape[1]

    def count_ge(t):
        return jnp.sum(jnp.where(at >= t, 1.0, 0.0), axis=1, keepdims=True)

    def bisect(_, bounds):
        lo, hi = bounds
        mid = 0.5 * (lo + hi)
        ok = count_ge(mid) >= cap
        return jnp.where(ok, mid, lo), jnp.where(ok, hi, mid)

    _, hi = lax.fori_loop(0, BISECT_STEPS, bisect,
                          (jnp.zeros((N_EXPERTS, 1), F32), jnp.full((N_EXPERTS, 1), 2.0, F32)))

    def below(h):
        return jnp.max(jnp.where(at < h, at, -1.0), axis=1, keepdims=True)

    def short(state):
        return jnp.sum(jnp.where(count_ge(state[0]) < cap, 1.0, 0.0)) > 0.0

    def step_down(state):
        t, h = state
        lacking = count_ge(t) < cap
        h = jnp.where(lacking, t, h)
        return jnp.where(lacking, below(h), t), h

    thr, _ = lax.while_loop(short, step_down, (below(hi), hi))
    need = cap - jnp.sum(jnp.where(at > thr, 1.0, 0.0), axis=1, keepdims=True)

    pi = lax.broadcasted_iota(jnp.int32, (PREFIX_BLK, PREFIX_BLK), 0)
    pj = lax.broadcasted_iota(jnp.int32, (PREFIX_BLK, PREFIX_BLK), 1)
    upper = jnp.where(pi <= pj, 1.0, 0.0).astype(BF16)
    lower = jnp.where(pi >= pj, 1.0, 0.0).astype(BF16)

    def prefix_lanes(m):
        carry = jnp.zeros((m.shape[0], 1), F32)
        outs = []
        for c in range(m.shape[1] // PREFIX_BLK):
            mc = m[:, c * PREFIX_BLK:(c + 1) * PREFIX_BLK]
            loc = _dot(mc.astype(BF16), upper)
            outs.append(loc - mc + carry)
            carry = carry + loc[:, PREFIX_BLK - 1:PREFIX_BLK]
        return jnp.concatenate(outs, axis=1)

    def prefix_rows(m):
        carry = jnp.zeros((1, m.shape[1]), F32)
        outs = []
        for c in range(m.shape[0] // PREFIX_BLK):
            mc = m[c * PREFIX_BLK:(c + 1) * PREFIX_BLK, :]
            loc = _dot(lower, mc.astype(BF16))
            outs.append(loc - mc + carry)
            carry = carry + loc[PREFIX_BLK - 1:PREFIX_BLK, :]
        return jnp.concatenate(outs, axis=0)

    eq_t = jnp.where(at == thr, 1.0, 0.0)
    sel_t = jnp.where(at > thr, 1.0, jnp.where(prefix_lanes(eq_t) < need, eq_t, 0.0))
    post_ref[0] = jnp.where(sel_t > 0.0, prefix_lanes(sel_t), -1.0)

    ti = lax.broadcasted_iota(jnp.int32, (n, LANES), 0)
    bi = lax.broadcasted_iota(jnp.int32, (n, LANES), 1)
    before = jnp.where(ti < bi * tb, 1.0, 0.0).astype(BF16)
    st = _dot(sel_t.astype(BF16), before)
    st_ref[0] = st.astype(jnp.int32)

    through = jnp.where(ti < (bi + 1) * tb, 1.0, 0.0).astype(BF16)
    st_end = _dot(sel_t.astype(BF16), through)
    blk_lane = lax.broadcasted_iota(jnp.int32, (N_EXPERTS, LANES), 1)
    is_blk = blk_lane < n // tb
    rng = jnp.zeros((N_EXPERTS, LANES), F32)
    ntile = cap // tj
    for j in range(ntile):
        first = jnp.sum(jnp.where(is_blk & (st_end <= j * tj), 1.0, 0.0), axis=1, keepdims=True)
        last = jnp.sum(jnp.where(is_blk & (st < (j + 1) * tj), 1.0, 0.0), axis=1, keepdims=True) - 1.0
        rng = rng + jnp.where(blk_lane == j, first, 0.0) + jnp.where(blk_lane == ntile + j, last, 0.0)
    rng_ref[0] = rng.astype(jnp.int32)

    a = aff_ref[0]
    er = lax.broadcasted_iota(jnp.int32, (N_EXPERTS, LANES), 0)
    ec = lax.broadcasted_iota(jnp.int32, (N_EXPERTS, LANES), 1)
    diag = er == ec
    thr_row = jnp.sum(jnp.where(diag, thr, 0.0), axis=0, keepdims=True)
    need_row = jnp.sum(jnp.where(diag, need, 0.0), axis=0, keepdims=True)
    valid = lax.broadcasted_iota(jnp.int32, a.shape, 1) < N_EXPERTS
    eq = jnp.where(valid & (a == thr_row), 1.0, 0.0)
    sel = jnp.where(valid & (a > thr_row), 1.0, jnp.where(prefix_rows(eq) < need_row, eq, 0.0))
    pos_ref[0] = jnp.where(sel > 0.0, prefix_rows(sel), -1.0)


def _select_call(afft, aff, cap, tb, tj):
    b, e, n = afft.shape
    assert n // tb + 1 <= LANES and 2 * (cap // tj) <= LANES
    return pl.pallas_call(
        functools.partial(_select_kernel, cap=cap, tb=tb, tj=tj),
        grid=(b,),
        in_specs=[pl.BlockSpec((1, e, n), lambda bi: (bi, 0, 0)),
                  pl.BlockSpec((1, n, LANES), lambda bi: (bi, 0, 0))],
        out_specs=[pl.BlockSpec((1, e, n), lambda bi: (bi, 0, 0)),
                   pl.BlockSpec((1, n, LANES), lambda bi: (bi, 0, 0)),
                   pl.BlockSpec((1, e, LANES), lambda bi: (bi, 0, 0)),
                   pl.BlockSpec((1, e, LANES), lambda bi: (bi, 0, 0))],
        out_shape=[jax.ShapeDtypeStruct((b, e, n), F32), jax.ShapeDtypeStruct((b, n, LANES), F32),
                   jax.ShapeDtypeStruct((b, e, LANES), jnp.int32),
                   jax.ShapeDtypeStruct((b, e, LANES), jnp.int32)],
        compiler_params=_cparams(("parallel",)),
        name="select",
    )(afft, aff)


def _moe_ffn_kernel(rng_ref, ht_ref, post_ref, afft_ref, wg_ref, wu_ref, wd_ref, ye_ref,
                    acc_ref, gacc_ref, wgb_ref, wub_ref, wdb_ref, *, tj, tb):
    b = pl.program_id(0)
    e = pl.program_id(1)
    cap = ye_ref.shape[2]
    ntile = cap // tj
    slot = lax.broadcasted_iota(jnp.int32, (tj, tb), 0).astype(F32)
    wgb_ref[...] = wg_ref[0].astype(BF16)
    wub_ref[...] = wu_ref[0].astype(BF16)
    wdb_ref[...] = wd_ref[0].astype(BF16)

    for j in range(cap // tj):
        lo = j * tj
        acc_ref[...] = jnp.zeros(acc_ref.shape, F32)
        gacc_ref[...] = jnp.zeros(gacc_ref.shape, F32)

        first = rng_ref[b, e, j]
        last = rng_ref[b, e, ntile + j]

        def group_body(g, carry):
            part = jnp.zeros(acc_ref.shape, F32)
            gate = jnp.zeros(gacc_ref.shape, F32)
            for u in range(GATHER_GROUP):
                bi = jnp.minimum(first + g * GATHER_GROUP + u, last)
                live = first + g * GATHER_GROUP + u <= last
                prow = post_ref[0, 0, pl.ds(bi, 1), :]
                hit = (prow == (slot + float(lo))) & live
                onehot = jnp.where(hit, 1.0, 0.0).astype(BF16)
                part = part + _dot_nt(ht_ref[0, bi], onehot)
                arow = afft_ref[0, 0, pl.ds(bi, 1), :]
                gate = gate + jnp.sum(jnp.where(hit, arow, 0.0), axis=1, keepdims=True)
            acc_ref[...] += part
            gacc_ref[...] += gate
            return carry

        lax.fori_loop(0, (last - first + GATHER_GROUP) // GATHER_GROUP, group_body, 0)
        xe = acc_ref[...].T.astype(BF16)
        hid = _silu(_dot(xe, wgb_ref[...])) * _dot(xe, wub_ref[...])
        ye = _dot(hid.astype(BF16), wdb_ref[...]) * gacc_ref[...]
        ye_ref[0, 0, lo:lo + tj, :] = ye.astype(ye_ref.dtype)


def _moe_ffn_call(tile_rng, ht4, post4, afft4, wg, wu, wd, cap, tj):
    b, nblk, d, tb = ht4.shape
    e, _, f = wg.shape
    wspec = lambda shp: pl.BlockSpec((1,) + shp, lambda bi, ei: (ei, 0, 0))
    return pl.pallas_call(
        functools.partial(_moe_ffn_kernel, tj=tj, tb=tb),
        grid=(b, e),
        in_specs=[pl.BlockSpec(memory_space=pltpu.SMEM),
                  pl.BlockSpec((1, nblk, d, tb), lambda bi, ei: (bi, 0, 0, 0), pipeline_mode=pl.Buffered(1)),
                  pl.BlockSpec((1, 1, nblk, tb), lambda bi, ei: (bi, ei, 0, 0)),
                  pl.BlockSpec((1, 1, nblk, tb), lambda bi, ei: (bi, ei, 0, 0)),
                  wspec((d, f)), wspec((d, f)), wspec((f, d))],
        out_specs=pl.BlockSpec((1, 1, cap, d), lambda bi, ei: (bi, ei, 0, 0)),
        out_shape=jax.ShapeDtypeStruct((b, e, cap, d), BF16),
        scratch_shapes=[pltpu.VMEM((d, tj), F32), pltpu.VMEM((tj, 1), F32),
                        pltpu.VMEM((d, f), BF16), pltpu.VMEM((d, f), BF16), pltpu.VMEM((f, d), BF16)],
        compiler_params=_cparams(("parallel", "arbitrary")),
        name="moe_ffn",
    )(tile_rng, ht4, post4, afft4, wg, wu, wd)


def _combine_kernel(st_ref, pos_ref, ye_ref, x_ref, gt2_ref, g_ref, o_ref, rest_ref, *, tb):
    b = pl.program_id(0)
    blk = pl.program_id(1)
    cap = ye_ref.shape[2]
    win = tb
    slot = lax.broadcasted_iota(jnp.int32, (tb, win), 1).astype(F32)
    pos = pos_ref[0]

    def window_base(e):
        return jnp.minimum(st_ref[b, e, blk] // COMBINE_ALIGN, (cap - win) // COMBINE_ALIGN) * COMBINE_ALIGN

    def window(e, start):
        return ye_ref[0, e, pl.ds(pl.multiple_of(start, COMBINE_ALIGN), win), :]

    moe = jnp.zeros((tb, ye_ref.shape[3]), F32)
    for e in range(N_EXPERTS):
        base = window_base(e)
        onehot = jnp.where(pos[:, e:e + 1] == slot + base.astype(F32), 1.0, 0.0).astype(BF16)
        moe = moe + _dot(onehot, window(e, base))

    rest_ref[...] = jnp.zeros(rest_ref.shape, F32)
    for e in range(N_EXPERTS):
        done = window_base(e) + win

        @pl.when(st_ref[b, e, blk + 1] > done)
        def _():
            start = jnp.minimum(done, cap - win)
            pcol = pos[:, e:e + 1]
            hit = (pcol == slot + start.astype(F32)) & (pcol >= done.astype(F32))
            rest_ref[...] += _dot(jnp.where(hit, 1.0, 0.0).astype(BF16), window(e, start))

    y = x_ref[0] + gt2_ref[0] * (moe + rest_ref[...])
    ms = jnp.mean(y * y, axis=-1, keepdims=True)
    o_ref[0] = y * lax.rsqrt(ms + EPS) * g_ref[...]


def _combine_call(starts, pos, ye, x_new, gt2, final_g, tb):
    b, e, cap, d = ye.shape
    n = pos.shape[1]
    assert cap % COMBINE_ALIGN == 0 and tb % COMBINE_ALIGN == 0 and cap >= tb
    row = lambda bi, i: (bi, i, 0)
    return pl.pallas_call(
        functools.partial(_combine_kernel, tb=tb),
        grid=(b, n // tb),
        in_specs=[pl.BlockSpec(memory_space=pltpu.SMEM),
                  pl.BlockSpec((1, tb, LANES), row),
                  pl.BlockSpec((1, e, cap, d), lambda bi, i: (bi, 0, 0, 0), pipeline_mode=pl.Buffered(1)),
                  pl.BlockSpec((1, tb, d), row),
                  pl.BlockSpec((1, 1, d), lambda bi, i: (bi, 0, 0)),
                  pl.BlockSpec((1, d), lambda bi, i: (0, 0))],
        out_specs=pl.BlockSpec((1, tb, d), row),
        out_shape=jax.ShapeDtypeStruct((b, n, d), F32),
        scratch_shapes=[pltpu.VMEM((tb, d), F32)],
        compiler_params=_cparams(("parallel", "arbitrary")),
        name="combine",
    )(starts, pos, ye, x_new, gt2, final_g)


def _rope_tables(n):
    t = np.arange(n)
    rows = (t // GRID_W).astype(np.float64)
    cols = (t % GRID_W).astype(np.float64)
    inv_freq = np.power(ROPE_THETA, -np.arange(0, ROPE_AXIS_DIM, 2, dtype=np.float64) / ROPE_AXIS_DIM)
    ang_row = rows[:, None] * inv_freq[None, :]
    ang_col = cols[:, None] * inv_freq[None, :]

    def axis_tables(ang):
        c = np.cos(ang).astype(np.float32)
        s = np.sin(ang).astype(np.float32)
        return np.concatenate([c, c], axis=1), np.concatenate([-s, s], axis=1)

    cr, sr = axis_tables(ang_row)
    cc, sc = axis_tables(ang_col)
    cos64 = np.concatenate([cr, cc], axis=1)
    sin64 = np.concatenate([sr, sc], axis=1)
    return (jnp.asarray(np.concatenate([cos64, cos64], axis=1)),
            jnp.asarray(np.concatenate([sin64, sin64], axis=1)))


def _attn_chunk(nk):
    return max(t for t in range(VT_BLK, 1024 + 1, VT_BLK) if nk % t == 0)


def _pad_lanes(v):
    return jnp.pad(v.reshape(1, -1), ((0, 0), (0, LANES - v.size)))


def kernel(x, c, ctx, c_ctx, w_mod, b_mod, norm1_g, w_in, conv_w, a_log, dt_bias, gdn_norm_g,
           lam_q1, lam_k1, lam_q2, lam_k2, da_subln_g, w_out, norm2_g,
           w_router, w_gate, w_up, w_down, final_g):
    b, n, d = x.shape
    nc = ctx.shape[1]
    layer = 0

    cvec = jnp.concatenate([c, c_ctx[None, :], jnp.zeros((8 - b - 1, d), F32)], axis=0)
    mod = _mod_call(cvec, w_mod[layer], b_mod[layer])
    sh1, sc1, gt1, sh2, sc2, gt2 = [mod[:b, i * d:(i + 1) * d].reshape(b, 1, d) for i in range(6)]
    sh1c = jnp.broadcast_to(mod[b:b + 1, 0:d].reshape(1, 1, d), (b, 1, d))
    sc1c = jnp.broadcast_to(mod[b:b + 1, d:2 * d].reshape(1, 1, d), (b, 1, d))

    w_in_bf = jnp.pad(w_in[layer].astype(BF16), ((0, 0), (0, IN_COLS_PAD - w_in.shape[2])))
    g1 = norm1_g[layer].reshape(1, d)
    cos_l, sin_l = _rope_tables(n)
    cos_c, sin_c = jnp.ones((nc, LANES), F32), jnp.zeros((nc, LANES), F32)
    conv_w8 = jnp.pad(conv_w[layer], ((0, 8 - conv_w.shape[1]), (0, 0)))
    alog_row = _pad_lanes(a_log[layer])
    dtb_row = _pad_lanes(dt_bias[layer])
    gdn_args = (conv_w8, alog_row, dtb_row)
    q, k, vt, ql, kl, vl, gate, gbl, gbtl = _inproj_call(
        x, g1, sh1, sc1, cos_l, sin_l, w_in_bf, *gdn_args, tm=512)
    _, kc, vct, qc, kcg, vcg, _, gbc, gbtc = _inproj_call(
        ctx, g1, sh1c, sc1c, cos_c, sin_c, w_in_bf, *gdn_args, tm=nc)

    lam_vecs = jnp.stack([lam_q1[layer], lam_k1[layer], lam_q2[layer], lam_k2[layer]], axis=0)
    k_all = jnp.concatenate([kc, k], axis=1)
    vt_all = jnp.concatenate([vct, vt], axis=2)
    o_da = _attn_call(q, k_all, vt_all, lam_vecs, da_subln_g[layer].reshape(1, DA_DV),
                      tq=2048, tk=_attn_chunk(nc + n))

    zeros_state = jnp.zeros((b, 2, GDN_HEADS, GDN_DK, GDN_DV), F32)
    _, _, s_ctx = _gdn_scan_call(qc, kcg, vcg, gbc, gbtc, zeros_state, sc=nc)
    o_fwd, o_bwd, _ = _gdn_scan_call(ql, kl, vl, gbl, gbtl, s_ctx, sc=256)
    o_dirs = [o_fwd, o_bwd]

    wr_pad = jnp.pad(w_router[layer], ((0, 0), (0, LANES - N_EXPERTS)))
    cap = CAP_FACTOR * n // N_EXPERTS
    tb = 256
    tj = 256
    x_new, ht4, aff, afft = _outproj_call(
        o_da, o_dirs[0], o_dirs[1], gate, x, gt1, gdn_norm_g[layer].reshape(1, GDN_DV),
        w_out[layer].astype(BF16), norm2_g[layer].reshape(1, d), sh2, sc2, wr_pad, tm=512, tb=tb)

    post, pos, starts, tile_rng = _select_call(afft, aff, cap, tb, tj)
    nblk = n // tb
    ye = _moe_ffn_call(tile_rng, ht4, post.reshape(b, N_EXPERTS, nblk, tb),
                       afft.reshape(b, N_EXPERTS, nblk, tb),
                       w_gate[layer], w_up[layer], w_down[layer], cap, tj)
    return _combine_call(starts, pos, ye, x_new, gt2, final_g.reshape(1, d), tb)
```

```python
import functools
import math

import jax
import jax.numpy as jnp
import numpy as np
from jax import lax
from jax.experimental import pallas as pl
from jax.experimental.pallas import tpu as pltpu

F32 = jnp.float32
BF16 = jnp.bfloat16

D_MODEL = 1024
GRID_W = 64
EPS = 1e-6
DA_HEADS = 4
DA_D = 64
DA_DV = 2 * DA_D
ROPE_AXIS_DIM = DA_D // 2
ROPE_THETA = 10000.0
GDN_HEADS = 4
GDN_DK = 128
GDN_DV = 128
CHUNK = 64
N_EXPERTS = 16
CAP_FACTOR = 2
LAM_INIT = 0.8 - 0.6 * math.exp(-0.3 * 0)

DA_QK = DA_HEADS * 2 * DA_D
DA_V = DA_HEADS * DA_DV
GDN_QK = GDN_HEADS * GDN_DK
GDN_V = GDN_HEADS * GDN_DV
GDN_QKV = 2 * GDN_QK + GDN_V
GDN_AB = 2 * 2 * GDN_HEADS
COL_Q, COL_K, COL_V = 0, DA_QK, 2 * DA_QK
COL_QKV = 2 * DA_QK + DA_V
COL_GATE = COL_QKV + GDN_QKV
COL_AB = COL_GATE + GDN_V
LANES = 128
IN_COLS_PAD = COL_AB + LANES
MXU_WIDTH = 256
ATTN_COL_TILE = MXU_WIDTH
VT_ROWS = DA_DV + 8
VT_BLK = MXU_WIDTH

VMEM_LIMIT = 56 * 1024 * 1024
NEG_BIG = -1e30
LOG2_E = math.log2(math.e)


def _cparams(sem):
    return pltpu.CompilerParams(dimension_semantics=sem, vmem_limit_bytes=VMEM_LIMIT)


def _sigmoid(x):
    return 1.0 / (1.0 + jnp.exp(-x))


def _silu(x):
    return x * _sigmoid(x)


def _split3(a):
    a1 = a.astype(BF16)
    r1 = a - a1.astype(F32)
    a2 = r1.astype(BF16)
    a3 = (r1 - a2.astype(F32)).astype(BF16)
    return a1, a2, a3


def _dot(a, b):
    return jnp.dot(a, b, preferred_element_type=F32)


def _dot_x3(a, b):
    a1, a2, _ = _split3(a)
    b1, b2, _ = _split3(b)
    return _dot(a1, b1) + (_dot(a1, b2) + _dot(a2, b1))


def _dot_nt(a, b):
    return lax.dot_general(a, b, (((1,), (1,)), ((), ())), preferred_element_type=F32)


def _dot_tn(a, b):
    return lax.dot_general(a, b, (((0,), (0,)), ((), ())), preferred_element_type=F32)


def _mod_kernel(c_ref, w_ref, b_ref, o_ref):
    s = _silu(c_ref[...])
    o_ref[...] = jnp.dot(s, w_ref[...], precision=lax.Precision.HIGHEST,
                         preferred_element_type=F32) + b_ref[...]


def _mod_call(cvec, w_mod, b_mod):
    d, n = w_mod.shape
    tn = 1024
    return pl.pallas_call(
        _mod_kernel,
        grid=(n // tn,),
        in_specs=[pl.BlockSpec((8, d), lambda j: (0, 0)),
                  pl.BlockSpec((d, tn), lambda j: (0, j)),
                  pl.BlockSpec((1, tn), lambda j: (0, j))],
        out_specs=pl.BlockSpec((8, tn), lambda j: (0, j)),
        out_shape=jax.ShapeDtypeStruct((8, n), F32),
        compiler_params=_cparams(("arbitrary",)),
        name="mod",
    )(cvec, w_mod, b_mod.reshape(1, n))


def _inproj_kernel(x_ref, xp_ref, xn_ref, g_ref, sh_ref, sc_ref, cos_ref, sin_ref, w_ref,
                   cw_ref, al_ref, dtb_ref,
                   q_ref, k_ref, vt_ref, gq_ref, gk_ref, gv_ref, gate_ref, gb_ref, gbt_ref):
    i = pl.program_id(1)
    nblk = pl.num_programs(1)
    tm = x_ref.shape[1]

    def modulated(xb):
        ms = jnp.mean(xb * xb, axis=-1, keepdims=True)
        hm = (xb * lax.rsqrt(ms + EPS) * g_ref[...]) * (1.0 + sc_ref[0]) + sh_ref[0]
        return hm.astype(BF16)

    hb = modulated(x_ref[0])

    def proj(lo, hi):
        return _dot(hb, w_ref[:, lo:hi])

    cos = cos_ref[...]
    sin = sin_ref[...]
    lane = lax.broadcasted_iota(jnp.int32, (tm, LANES), 1)
    first_half = (lane % ROPE_AXIS_DIM) < (ROPE_AXIS_DIM // 2)

    def rope(t):
        outs = []
        for j in range(t.shape[1] // LANES):
            s = t[:, LANES * j:LANES * (j + 1)]
            partner = jnp.where(first_half,
                                pltpu.roll(s, LANES - ROPE_AXIS_DIM // 2, 1),
                                pltpu.roll(s, ROPE_AXIS_DIM // 2, 1))
            outs.append(s * cos + partner * sin)
        return jnp.concatenate(outs, axis=1)

    q_ref[0] = (rope(proj(COL_Q, COL_K)) * (DA_D ** -0.5 * LOG2_E)).astype(BF16)
    k_ref[0] = rope(proj(COL_K, COL_V)).astype(BF16)
    vt = proj(COL_V, COL_QKV).T.astype(BF16).reshape(DA_HEADS, DA_DV, tm)
    extra = lax.broadcasted_iota(jnp.int32, (DA_HEADS, VT_ROWS - DA_DV, VT_BLK), 1)
    ones_row = jnp.where(extra == 0, 1.0, 0.0).astype(BF16)
    for c in range(tm // VT_BLK):
        vt_ref[0, :, c, 0:DA_DV, :] = vt[:, :, c * VT_BLK:(c + 1) * VT_BLK]
        vt_ref[0, :, c, DA_DV:VT_ROWS, :] = ones_row
    gate_ref[0] = proj(COL_GATE, COL_AB)

    qkv = proj(COL_QKV, COL_GATE)
    halo = _dot(modulated(jnp.concatenate([xp_ref[0], xn_ref[0]], axis=0)),
                w_ref[:, COL_QKV:COL_GATE])
    prev = jnp.where(i > 0, halo[7:8, :], 0.0)
    nxt = jnp.where(i < nblk - 1, halo[8:9, :], 0.0)
    row = lax.broadcasted_iota(jnp.int32, (tm, 1), 0)
    qkv_m1 = jnp.where(row == 0, prev, pltpu.roll(qkv, 1, 0))
    qkv_p1 = jnp.where(row == tm - 1, nxt, pltpu.roll(qkv, tm - 1, 0))
    cw = cw_ref[...]
    s = _silu(qkv_m1 * cw[0:1] + qkv * cw[1:2] + qkv_p1 * cw[2:3])

    def l2n(t):
        return t * lax.rsqrt(jnp.sum(t * t, axis=-1, keepdims=True) + EPS)

    for h in range(GDN_HEADS):
        lo, hi = h * GDN_DK, (h + 1) * GDN_DK
        gq_ref[0, :, lo:hi] = l2n(s[:, lo:hi]) * GDN_DK ** -0.5
        gk_ref[0, :, lo:hi] = l2n(s[:, GDN_QK + lo:GDN_QK + hi])
    gv_ref[0] = s[:, 2 * GDN_QK:]

    ab = proj(COL_AB, IN_COLS_PAD)
    z = ab + dtb_ref[...]
    softplus = jnp.maximum(z, 0.0) + jnp.log1p(jnp.exp(-jnp.abs(z)))
    gval = -jnp.exp(al_ref[...]) * softplus
    gb = jnp.where(lane < GDN_AB // 2, gval, jnp.where(lane < GDN_AB, _sigmoid(ab), 0.0))
    gb_ref[0] = gb
    gbt_ref[0] = gb.T[:GDN_AB, :]


def _inproj_call(x, norm_g, shift, scale, cos, sin, w_bf, conv_w8, alog_row, dtb_row, tm):
    b, n, d = x.shape
    nb8 = n // 8
    step8 = tm // 8
    row = lambda bi, i: (bi, i, 0)
    vec = lambda bi, i: (bi, 0, 0)
    const = lambda bi, i: (0, 0)
    rows_out = [(DA_QK, BF16), (DA_QK, BF16), None, (GDN_QK, F32), (GDN_QK, F32), (GDN_V, F32),
                (GDN_V, F32), (LANES, F32)]
    out_specs = [pl.BlockSpec((1, tm, o[0]), row) if o else
                 pl.BlockSpec((1, DA_HEADS, tm // VT_BLK, VT_ROWS, VT_BLK), lambda bi, i: (bi, 0, i, 0, 0))
                 for o in rows_out]
    out_shape = [jax.ShapeDtypeStruct((b, n, o[0]), o[1]) if o else
                 jax.ShapeDtypeStruct((b, DA_HEADS, n // VT_BLK, VT_ROWS, VT_BLK), BF16) for o in rows_out]
    out_specs.append(pl.BlockSpec((1, GDN_AB, tm), lambda bi, i: (bi, 0, i)))
    out_shape.append(jax.ShapeDtypeStruct((b, GDN_AB, n), F32))
    return pl.pallas_call(
        _inproj_kernel,
        grid=(b, n // tm),
        in_specs=[pl.BlockSpec((1, tm, d), row),
                  pl.BlockSpec((1, 8, d), lambda bi, i: (bi, jnp.maximum(i * step8 - 1, 0), 0)),
                  pl.BlockSpec((1, 8, d), lambda bi, i: (bi, jnp.minimum((i + 1) * step8, nb8 - 1), 0)),
                  pl.BlockSpec((1, d), const),
                  pl.BlockSpec((1, 1, d), vec),
                  pl.BlockSpec((1, 1, d), vec),
                  pl.BlockSpec((tm, LANES), lambda bi, i: (i, 0)),
                  pl.BlockSpec((tm, LANES), lambda bi, i: (i, 0)),
                  pl.BlockSpec((d, IN_COLS_PAD), const),
                  pl.BlockSpec((8, GDN_QKV), const),
                  pl.BlockSpec((1, LANES), const),
                  pl.BlockSpec((1, LANES), const)],
        out_specs=out_specs,
        out_shape=out_shape,
        compiler_params=_cparams(("parallel", "parallel")),
        name="inproj",
    )(x, x, x, norm_g, shift, scale, cos, sin, w_bf, conv_w8, alog_row, dtb_row)


def _attn_kernel(q_ref, k_ref, vt_ref, lam_ref, g_ref, o_ref,
                 q2_ref, acc_ref, m_ref, s_ref, p_ref, a_ref, *, tk):
    q = q_ref[0]
    tq = q.shape[0]
    lane = lax.broadcasted_iota(jnp.int32, (tq, LANES), 1)
    zero = jnp.zeros_like(q)
    q2_ref[0:tq, :] = jnp.where(lane < DA_D, q, zero)
    q2_ref[tq:2 * tq, :] = jnp.where(lane >= DA_D, q, zero)
    m_ref[...] = jnp.full(m_ref.shape, NEG_BIG, F32)
    acc_ref[...] = jnp.zeros(acc_ref.shape, F32)

    nct = 2 * tq // ATTN_COL_TILE
    col = lambda ct: slice(ct * ATTN_COL_TILE, (ct + 1) * ATTN_COL_TILE)

    def scores(kk, ct):
        return _dot_nt(kk, q2_ref[col(ct), :])

    def softmax_update(ct, s):
        cols = col(ct)
        m_old = m_ref[:, cols]
        m_new = jnp.maximum(m_old, jnp.max(s, axis=0, keepdims=True))
        alpha = jnp.exp2(m_old - m_new)
        p = jnp.exp2(s - m_new)
        m_ref[:, cols] = m_new
        return p.astype(BF16), alpha

    vblk = vt_ref.shape[4]
    nsub = tk // vblk

    def value_update(ct, j, p, alpha):
        cols = col(ct)
        pv = _dot(vt_ref[0, 0, j * nsub], p[0:vblk, :])
        for c in range(1, nsub):
            pv = pv + _dot(vt_ref[0, 0, j * nsub + c], p[c * vblk:(c + 1) * vblk, :])
        acc_ref[:, cols] = alpha * acc_ref[:, cols] + pv

    nchunk = k_ref.shape[1] // tk
    last = nct - 1
    s_ref[...] = scores(k_ref[0, 0:tk, :], 0)
    p_ref[...] = jnp.zeros(p_ref.shape, BF16)
    a_ref[...] = jnp.ones(a_ref.shape, F32)

    def body(j, carry):
        kk = k_ref[0, pl.ds(pl.multiple_of(j * tk, vblk), tk), :]
        s_cur = s_ref[...]
        pending = (last, jnp.maximum(j - 1, 0), p_ref[...], a_ref[...])
        for ct in range(nct):
            if ct < last:
                s_ahead = scores(kk, ct + 1)
            else:
                jn = jnp.minimum(j + 1, nchunk - 1)
                s_ahead = scores(k_ref[0, pl.ds(pl.multiple_of(jn * tk, vblk), tk), :], 0)
            p, alpha = softmax_update(ct, s_cur)
            value_update(*pending)
            pending = (ct, j, p, alpha)
            s_cur = s_ahead
        s_ref[...] = s_cur
        p_ref[...] = pending[2]
        a_ref[...] = pending[3]
        return carry

    lax.fori_loop(0, nchunk, body, 0)
    value_update(last, nchunk - 1, p_ref[...], a_ref[...])

    lv = lam_ref[...]
    lam = (jnp.exp(jnp.sum(lv[0:1] * lv[1:2], axis=1, keepdims=True))
           - jnp.exp(jnp.sum(lv[2:3] * lv[3:4], axis=1, keepdims=True)) + LAM_INIT)
    den = acc_ref[DA_DV:DA_DV + 1, :]
    ot = (acc_ref[0:DA_DV, 0:tq] / den[:, 0:tq]
          - lam * (acc_ref[0:DA_DV, tq:2 * tq] / den[:, tq:2 * tq]))
    o = ot.T
    ms = jnp.mean(o * o, axis=-1, keepdims=True)
    o_ref[0] = ((o * lax.rsqrt(ms + EPS) * g_ref[...]) * (1.0 - LAM_INIT)).astype(o_ref.dtype)


def _attn_call(q, k_all, vt_all, lam_vecs, subln_g, tq, tk):
    b, n, _ = q.shape
    nk = k_all.shape[1]
    _, _, nvt, _, vblk = vt_all.shape
    assert nk == nvt * vblk and tk % vblk == 0 and nk % tk == 0
    return pl.pallas_call(
        functools.partial(_attn_kernel, tk=tk),
        grid=(b, DA_HEADS, n // tq),
        in_specs=[pl.BlockSpec((1, tq, LANES), lambda bi, h, i: (bi, i, h)),
                  pl.BlockSpec((1, nk, LANES), lambda bi, h, i: (bi, 0, h)),
                  pl.BlockSpec((1, 1, nvt, VT_ROWS, vblk), lambda bi, h, i: (bi, h, 0, 0, 0)),
                  pl.BlockSpec((4, DA_D), lambda bi, h, i: (0, 0)),
                  pl.BlockSpec((1, DA_DV), lambda bi, h, i: (0, 0))],
        out_specs=pl.BlockSpec((1, tq, LANES), lambda bi, h, i: (bi, i, h)),
        out_shape=jax.ShapeDtypeStruct((b, n, DA_V), BF16),
        scratch_shapes=[pltpu.VMEM((2 * tq, LANES), BF16), pltpu.VMEM((VT_ROWS, 2 * tq), F32),
                        pltpu.VMEM((1, 2 * tq), F32),
                        pltpu.VMEM((tk, ATTN_COL_TILE), F32), pltpu.VMEM((tk, ATTN_COL_TILE), BF16),
                        pltpu.VMEM((1, ATTN_COL_TILE), F32)],
        compiler_params=_cparams(("parallel", "parallel", "parallel")),
        name="attn",
    )(q, k_all, vt_all, lam_vecs, subln_g)


def _solve_unit_triangular(a_list, rhs_list, bd_mask):
    def mm(xs, ys):
        return [_dot(x.astype(BF16), y.astype(BF16)) for x, y in zip(xs, ys)]

    d = [jnp.where(bd_mask, a, 0.0) for a in a_list]
    n = [a - di for a, di in zip(a_list, d)]
    d2 = mm(d, d)
    dd2 = mm(d, d2)
    p = [(-di + d2i) - t for di, d2i, t in zip(d, d2, dd2)]
    d4 = mm(d2, d2)
    pd4 = mm(p, d4)
    p = [pi + d4i + t for pi, d4i, t in zip(p, d4, pd4)]
    d8 = mm(d4, d4)
    pd8 = mm(p, d8)
    p = [pi + d8i + t for pi, d8i, t in zip(p, d8, pd8)]
    m = [ni + t for ni, t in zip(n, mm(p, n))]
    z = [ri + t for ri, t in zip(rhs_list, mm(p, rhs_list))]
    m2 = mm(m, m)
    z = [zi + t for zi, t in zip(z, mm(m2, z))]
    return [zi - t for zi, t in zip(z, mm(m, z))]


def _gdn_scan_kernel(qf_ref, kf_ref, vf_ref, gbf_ref, gbtf_ref, qb_ref, kb_ref, vb_ref, gbb_ref, gbtb_ref,
                     s0_ref, of_ref, ob_ref, sfin_ref, s_ref):
    i = pl.program_id(1)

    @pl.when(i == 0)
    def _():
        s_ref[...] = s0_ref[0]

    c64 = CHUNK
    nchunk = qf_ref.shape[1] // c64
    ri = lax.broadcasted_iota(jnp.int32, (c64, c64), 0)
    ci = lax.broadcasted_iota(jnp.int32, (c64, c64), 1)
    bd_mask = (ri // 16) == (ci // 16)
    lower_incl, upper_incl = ri >= ci, ri <= ci
    heads = range(GDN_HEADS)
    rows = [slice(c * c64, (c + 1) * c64) for c in range(nchunk)]
    cols = [slice(h * GDN_DK, (h + 1) * GDN_DK) for h in heads]

    dirs = []
    for d, refs in enumerate(((qf_ref, kf_ref, vf_ref, gbf_ref, gbtf_ref, of_ref),
                              (qb_ref, kb_ref, vb_ref, gbb_ref, gbtb_ref, ob_ref))):
        rev = d == 1
        incl = upper_incl if rev else lower_incl
        dirs.append(dict(
            refs=refs, incl=incl, strict=(ri < ci) if rev else (ri > ci),
            tri_c=jnp.where(incl, 1.0, 0.0).astype(BF16),
            tri_r=jnp.where(lower_incl if rev else upper_incl, 1.0, 0.0).astype(BF16),
            last=0 if rev else c64 - 1,
            order=list(range(nchunk - 1, -1, -1) if rev else range(nchunk))))

    gall, gc_col, gc_row = {}, {}, {}
    for d, dr in enumerate(dirs):
        gb_ref, gbt_ref = dr["refs"][3], dr["refs"][4]
        for c in range(nchunk):
            gall[d, c] = gb_ref[0, rows[c], :]
            g1, g2, g3 = _split3(gall[d, c])
            tc, tr = dr["tri_c"], dr["tri_r"]
            gc_col[d, c] = _dot(tc, g1) + _dot(tc, g2) + _dot(tc, g3)
            t1, t2, t3 = _split3(gbt_ref[0, :, rows[c]])
            gc_row[d, c] = _dot(t1, tr) + _dot(t2, tr) + _dot(t3, tr)

    probs = [(d, c, h) for d in range(2) for c in range(nchunk) for h in heads]
    ln = lambda d, h: d * GDN_HEADS + h
    k = {(d, c, h): dirs[d]["refs"][1][0, rows[c], cols[h]] for d, c, h in probs}
    q = {(d, c, h): dirs[d]["refs"][0][0, rows[c], cols[h]] for d, c, h in probs}
    kbf = {p: k[p].astype(BF16) for p in probs}
    kk = {p: _dot_nt(kbf[p], kbf[p]) for p in probs}
    qk = {p: _dot_nt(q[p].astype(BF16), kbf[p]) for p in probs}
    a_list, rhs_list = [], []
    kd, qd, aqk, glast = {}, {}, {}, {}
    for p in probs:
        d, c, h = p
        dr = dirs[d]
        lane = ln(d, h)
        gcol = gc_col[d, c][:, lane:lane + 1]
        bcol = gall[d, c][:, GDN_AB // 2 + lane:GDN_AB // 2 + lane + 1]
        diff = gcol - gc_row[d, c][lane:lane + 1, :]
        e_strict = jnp.where(dr["strict"], jnp.exp(jnp.where(dr["strict"], diff, 0.0)), 0.0)
        e_incl = jnp.where(dr["incl"], jnp.exp(jnp.where(dr["incl"], diff, 0.0)), 0.0)
        eg = jnp.exp(gcol)
        glast[p] = gcol[dr["last"]:dr["last"] + 1, :]
        a_list.append(bcol * kk[p] * e_strict)
        v = dr["refs"][2][0, rows[c], cols[h]]
        rhs_list.append(jnp.concatenate([(bcol * eg) * k[p], bcol * v], axis=1))
        kd[p] = (k[p] * jnp.exp(glast[p] - gcol)).astype(BF16)
        qd[p] = (q[p] * eg).astype(BF16)
        aqk[p] = (qk[p] * e_incl).astype(BF16)
    sol = dict(zip(probs, _solve_unit_triangular(a_list, rhs_list, bd_mask)))

    chains = [(d, h) for d in range(2) for h in heads]
    for t in range(nchunk):
        cur = {(d, h): (d, dirs[d]["order"][t], h) for d, h in chains}
        s = {dh: s_ref[dh[0], dh[1]] for dh in chains}
        sb = {dh: s[dh].astype(BF16) for dh in chains}
        ws = {dh: _dot(sol[cur[dh]][:, :GDN_DK].astype(BF16), sb[dh]) for dh in chains}
        qs = {dh: _dot(qd[cur[dh]], sb[dh]) for dh in chains}
        u = {dh: (sol[cur[dh]][:, GDN_DK:] - ws[dh]).astype(BF16) for dh in chains}
        au = {dh: _dot(aqk[cur[dh]], u[dh]) for dh in chains}
        ku = {dh: _dot_tn(kd[cur[dh]], u[dh]) for dh in chains}
        for dh in chains:
            d, c, h = cur[dh]
            s_ref[d, h] = jnp.exp(glast[cur[dh]]) * s[dh] + ku[dh]
            dirs[d]["refs"][5][0, rows[c], cols[h]] = qs[dh] + au[dh]

    @pl.when(i == pl.num_programs(1) - 1)
    def _():
        sfin_ref[0] = s_ref[...]


def _gdn_scan_call(q, k, v, gb, gbt, s0, sc):
    b, n, _ = q.shape
    nsup = n // sc
    fwd = lambda bi, i: (bi, i, 0)
    bwd = lambda bi, i: (bi, nsup - 1 - i, 0)
    st = lambda bi, i: (bi, 0, 0, 0, 0)
    state = (1, 2, GDN_HEADS, GDN_DK, GDN_DV)

    def in_specs(row, tr):
        return [pl.BlockSpec((1, sc, GDN_QK), row), pl.BlockSpec((1, sc, GDN_QK), row),
                pl.BlockSpec((1, sc, GDN_V), row), pl.BlockSpec((1, sc, LANES), row),
                pl.BlockSpec((1, GDN_AB, sc), tr)]

    return pl.pallas_call(
        _gdn_scan_kernel,
        grid=(b, nsup),
        in_specs=(in_specs(fwd, lambda bi, i: (bi, 0, i))
                  + in_specs(bwd, lambda bi, i: (bi, 0, nsup - 1 - i))
                  + [pl.BlockSpec(state, st)]),
        out_specs=[pl.BlockSpec((1, sc, GDN_V), fwd), pl.BlockSpec((1, sc, GDN_V), bwd),
                   pl.BlockSpec(state, st)],
        out_shape=[jax.ShapeDtypeStruct((b, n, GDN_V), F32), jax.ShapeDtypeStruct((b, n, GDN_V), F32),
                   jax.ShapeDtypeStruct((b,) + state[1:], F32)],
        scratch_shapes=[pltpu.VMEM(state[1:], F32)],
        compiler_params=_cparams(("parallel", "arbitrary")),
        name="gdn_scan",
    )(q, k, v, gb, gbt, q, k, v, gb, gbt, s0)


def _outproj_kernel(oda_ref, of_ref, ob_ref, gate_ref, x_ref, gt1_ref, gng_ref, wout_ref,
                    n2g_ref, sh2_ref, sc2_ref, wr_ref, xnew_ref, ht_ref, aff_ref, afft_ref):
    og = of_ref[0] + ob_ref[0]
    gate = gate_ref[0]
    ys = []
    for h in range(GDN_HEADS):
        cols = slice(h * GDN_DV, (h + 1) * GDN_DV)
        t = og[:, cols]
        y = t * lax.rsqrt(jnp.mean(t * t, axis=-1, keepdims=True) + EPS) * gng_ref[...]
        ys.append((y * _silu(gate[:, cols])).astype(BF16))
    y_gdn = jnp.concatenate(ys, axis=1)
    proj = _dot(oda_ref[0], wout_ref[0:DA_V, :]) + _dot(y_gdn, wout_ref[DA_V:, :])
    xn = x_ref[0] + gt1_ref[0] * proj
    xnew_ref[0] = xn
    ms = jnp.mean(xn * xn, axis=-1, keepdims=True)
    hm = (xn * lax.rsqrt(ms + EPS) * n2g_ref[...]) * (1.0 + sc2_ref[0]) + sh2_ref[0]
    hmt = hm.T.astype(BF16)
    tb = ht_ref.shape[3]
    for c in range(ht_ref.shape[1]):
        ht_ref[0, c] = hmt[:, c * tb:(c + 1) * tb]
    logits = _dot_x3(hm, wr_ref[...])
    lane = lax.broadcasted_iota(jnp.int32, logits.shape, 1)
    logits = jnp.where(lane < N_EXPERTS, logits, NEG_BIG)
    e = jnp.exp(logits - jnp.max(logits, axis=-1, keepdims=True))
    aff = e / jnp.sum(e, axis=-1, keepdims=True)
    aff_ref[0] = aff
    afft_ref[0] = aff.T[:N_EXPERTS, :]


def _outproj_call(o_da, o_f, o_b, gate, x, gt1, gdn_norm_g, w_out_bf, norm2_g, sh2, sc2, wr_pad, tm, tb):
    b, n, d = x.shape
    assert tm % tb == 0
    row = lambda bi, i: (bi, i, 0)
    vec = lambda bi, i: (bi, 0, 0)
    const2 = lambda bi, i: (0, 0)
    return pl.pallas_call(
        _outproj_kernel,
        grid=(b, n // tm),
        in_specs=[pl.BlockSpec((1, tm, DA_V), row), pl.BlockSpec((1, tm, GDN_V), row),
                  pl.BlockSpec((1, tm, GDN_V), row), pl.BlockSpec((1, tm, GDN_V), row),
                  pl.BlockSpec((1, tm, d), row), pl.BlockSpec((1, 1, d), vec),
                  pl.BlockSpec((1, GDN_DV), const2), pl.BlockSpec((DA_V + GDN_V, d), const2),
                  pl.BlockSpec((1, d), const2), pl.BlockSpec((1, 1, d), vec), pl.BlockSpec((1, 1, d), vec),
                  pl.BlockSpec((d, LANES), const2)],
        out_specs=[pl.BlockSpec((1, tm, d), row),
                   pl.BlockSpec((1, tm // tb, d, tb), lambda bi, i: (bi, i, 0, 0)),
                   pl.BlockSpec((1, tm, LANES), row),
                   pl.BlockSpec((1, N_EXPERTS, tm), lambda bi, i: (bi, 0, i))],
        out_shape=[jax.ShapeDtypeStruct((b, n, d), F32), jax.ShapeDtypeStruct((b, n // tb, d, tb), BF16),
                   jax.ShapeDtypeStruct((b, n, LANES), F32),
                   jax.ShapeDtypeStruct((b, N_EXPERTS, n), F32)],
        compiler_params=_cparams(("parallel", "parallel")),
        name="outproj",
    )(o_da, o_f, o_b, gate, x, gt1, gdn_norm_g, w_out_bf, norm2_g, sh2, sc2, wr_pad)


GATHER_GROUP = 8
COMBINE_ALIGN = 16
PREFIX_BLK = 256
BISECT_STEPS = 64


def _select_kernel(afft_ref, aff_ref, post_ref, pos_ref, st_ref, rng_ref, *, cap, tb, tj):
    at = afft_ref[0]
    n = at.shape[1]

    def count_ge(t):
        return jnp.sum(jnp.where(at >= t, 1.0, 0.0), axis=1, keepdims=True)

    def bisect(_, bounds):
        lo, hi = bounds
        mid = 0.5 * (lo + hi)
        ok = count_ge(mid) >= cap
        return jnp.where(ok, mid, lo), jnp.where(ok, hi, mid)

    _, hi = lax.fori_loop(0, BISECT_STEPS, bisect,
                          (jnp.zeros((N_EXPERTS, 1), F32), jnp.full((N_EXPERTS, 1), 2.0, F32)))

    def below(h):
        return jnp.max(jnp.where(at < h, at, -1.0), axis=1, keepdims=True)

    def short(state):
        return jnp.sum(jnp.where(count_ge(state[0]) < cap, 1.0, 0.0)) > 0.0

    def step_down(state):
        t, h = state
        lacking = count_ge(t) < cap
        h = jnp.where(lacking, t, h)
        return jnp.where(lacking, below(h), t), h

    thr, _ = lax.while_loop(short, step_down, (below(hi), hi))
    need = cap - jnp.sum(jnp.where(at > thr, 1.0, 0.0), axis=1, keepdims=True)

    pi = lax.broadcasted_iota(jnp.int32, (PREFIX_BLK, PREFIX_BLK), 0)
    pj = lax.broadcasted_iota(jnp.int32, (PREFIX_BLK, PREFIX_BLK), 1)
    upper = jnp.where(pi <= pj, 1.0, 0.0).astype(BF16)
    lower = jnp.where(pi >= pj, 1.0, 0.0).astype(BF16)

    def prefix_lanes(m):
        carry = jnp.zeros((m.shape[0], 1), F32)
        outs = []
        for c in range(m.shape[1] // PREFIX_BLK):
            mc = m[:, c * PREFIX_BLK:(c + 1) * PREFIX_BLK]
            loc = _dot(mc.astype(BF16), upper)
            outs.append(loc - mc + carry)
            carry = carry + loc[:, PREFIX_BLK - 1:PREFIX_BLK]
        return jnp.concatenate(outs, axis=1)

    def prefix_rows(m):
        carry = jnp.zeros((1, m.shape[1]), F32)
        outs = []
        for c in range(m.shape[0] // PREFIX_BLK):
            mc = m[c * PREFIX_BLK:(c + 1) * PREFIX_BLK, :]
            loc = _dot(lower, mc.astype(BF16))
            outs.append(loc - mc + carry)
            carry = carry + loc[PREFIX_BLK - 1:PREFIX_BLK, :]
        return jnp.concatenate(outs, axis=0)

    eq_t = jnp.where(at == thr, 1.0, 0.0)
    sel_t = jnp.where(at > thr, 1.0, jnp.where(prefix_lanes(eq_t) < need, eq_t, 0.0))
    post_ref[0] = jnp.where(sel_t > 0.0, prefix_lanes(sel_t), -1.0)

    ti = lax.broadcasted_iota(jnp.int32, (n, LANES), 0)
    bi = lax.broadcasted_iota(jnp.int32, (n, LANES), 1)
    before = jnp.where(ti < bi * tb, 1.0, 0.0).astype(BF16)
    st = _dot(sel_t.astype(BF16), before)
    st_ref[0] = st.astype(jnp.int32)

    through = jnp.where(ti < (bi + 1) * tb, 1.0, 0.0).astype(BF16)
    st_end = _dot(sel_t.astype(BF16), through)
    blk_lane = lax.broadcasted_iota(jnp.int32, (N_EXPERTS, LANES), 1)
    is_blk = blk_lane < n // tb
    rng = jnp.zeros((N_EXPERTS, LANES), F32)
    ntile = cap // tj
    for j in range(ntile):
        first = jnp.sum(jnp.where(is_blk & (st_end <= j * tj), 1.0, 0.0), axis=1, keepdims=True)
        last = jnp.sum(jnp.where(is_blk & (st < (j + 1) * tj), 1.0, 0.0), axis=1, keepdims=True) - 1.0
        rng = rng + jnp.where(blk_lane == j, first, 0.0) + jnp.where(blk_lane == ntile + j, last, 0.0)
    rng_ref[0] = rng.astype(jnp.int32)

    a = aff_ref[0]
    er = lax.broadcasted_iota(jnp.int32, (N_EXPERTS, LANES), 0)
    ec = lax.broadcasted_iota(jnp.int32, (N_EXPERTS, LANES), 1)
    diag = er == ec
    thr_row = jnp.sum(jnp.where(diag, thr, 0.0), axis=0, keepdims=True)
    need_row = jnp.sum(jnp.where(diag, need, 0.0), axis=0, keepdims=True)
    valid = lax.broadcasted_iota(jnp.int32, a.shape, 1) < N_EXPERTS
    eq = jnp.where(valid & (a == thr_row), 1.0, 0.0)
    sel = jnp.where(valid & (a > thr_row), 1.0, jnp.where(prefix_rows(eq) < need_row, eq, 0.0))
    pos_ref[0] = jnp.where(sel > 0.0, prefix_rows(sel), -1.0)


def _select_call(afft, aff, cap, tb, tj):
    b, e, n = afft.shape
    assert n // tb + 1 <= LANES and 2 * (cap // tj) <= LANES
    return pl.pallas_call(
        functools.partial(_select_kernel, cap=cap, tb=tb, tj=tj),
        grid=(b,),
        in_specs=[pl.BlockSpec((1, e, n), lambda bi: (bi, 0, 0)),
                  pl.BlockSpec((1, n, LANES), lambda bi: (bi, 0, 0))],
        out_specs=[pl.BlockSpec((1, e, n), lambda bi: (bi, 0, 0)),
                   pl.BlockSpec((1, n, LANES), lambda bi: (bi, 0, 0)),
                   pl.BlockSpec((1, e, LANES), lambda bi: (bi, 0, 0)),
                   pl.BlockSpec((1, e, LANES), lambda bi: (bi, 0, 0))],
        out_shape=[jax.ShapeDtypeStruct((b, e, n), F32), jax.ShapeDtypeStruct((b, n, LANES), F32),
                   jax.ShapeDtypeStruct((b, e, LANES), jnp.int32),
                   jax.ShapeDtypeStruct((b, e, LANES), jnp.int32)],
        compiler_params=_cparams(("parallel",)),
        name="select",
    )(afft, aff)


def _moe_ffn_kernel(rng_ref, ht_ref, post_ref, afft_ref, wg_ref, wu_ref, wd_ref, ye_ref,
                    acc_ref, gacc_ref, wgb_ref, wub_ref, wdb_ref, *, tj, tb):
    b = pl.program_id(0)
    e = pl.program_id(1)
    cap = ye_ref.shape[2]
    ntile = cap // tj
    slot = lax.broadcasted_iota(jnp.int32, (tj, tb), 0).astype(F32)
    wgb_ref[...] = wg_ref[0].astype(BF16)
    wub_ref[...] = wu_ref[0].astype(BF16)
    wdb_ref[...] = wd_ref[0].astype(BF16)

    for j in range(cap // tj):
        lo = j * tj
        acc_ref[...] = jnp.zeros(acc_ref.shape, F32)
        gacc_ref[...] = jnp.zeros(gacc_ref.shape, F32)

        first = rng_ref[b, e, j]
        last = rng_ref[b, e, ntile + j]

        def gather_blocks(start, count):
            part = jnp.zeros(acc_ref.shape, F32)
            gate = jnp.zeros(gacc_ref.shape, F32)
            for u in range(count):
                bi = start + u
                prow = post_ref[0, 0, pl.ds(bi, 1), :]
                hit = prow == (slot + float(lo))
                onehot = jnp.where(hit, 1.0, 0.0).astype(BF16)
                part = part + _dot_nt(ht_ref[0, bi], onehot)
                arow = afft_ref[0, 0, pl.ds(bi, 1), :]
                gate = gate + jnp.sum(jnp.where(hit, arow, 0.0), axis=1, keepdims=True)
            acc_ref[...] += part
            gacc_ref[...] += gate

        def group_body(g, carry):
            gather_blocks(first + g * GATHER_GROUP, GATHER_GROUP)
            return carry

        nblocks = last - first + 1
        full = nblocks // GATHER_GROUP
        lax.fori_loop(0, full, group_body, 0)
        done = first + full * GATHER_GROUP
        size = GATHER_GROUP // 2
        while size >= 1:
            @pl.when((nblocks & size) != 0)
            def _(done=done, size=size):
                gather_blocks(done, size)
            done = done + (nblocks & size)
            size //= 2
        xe = acc_ref[...].T.astype(BF16)
        hid = _silu(_dot(xe, wgb_ref[...])) * _dot(xe, wub_ref[...])
        ye = _dot(hid.astype(BF16), wdb_ref[...]) * gacc_ref[...]
        ye_ref[0, 0, lo:lo + tj, :] = ye.astype(ye_ref.dtype)


def _moe_ffn_call(tile_rng, ht4, post4, afft4, wg, wu, wd, cap, tj):
    b, nblk, d, tb = ht4.shape
    e, _, f = wg.shape
    wspec = lambda shp: pl.BlockSpec((1,) + shp, lambda bi, ei: (ei, 0, 0))
    return pl.pallas_call(
        functools.partial(_moe_ffn_kernel, tj=tj, tb=tb),
        grid=(b, e),
        in_specs=[pl.BlockSpec(memory_space=pltpu.SMEM),
                  pl.BlockSpec((1, nblk, d, tb), lambda bi, ei: (bi, 0, 0, 0), pipeline_mode=pl.Buffered(1)),
                  pl.BlockSpec((1, 1, nblk, tb), lambda bi, ei: (bi, ei, 0, 0)),
                  pl.BlockSpec((1, 1, nblk, tb), lambda bi, ei: (bi, ei, 0, 0)),
                  wspec((d, f)), wspec((d, f)), wspec((f, d))],
        out_specs=pl.BlockSpec((1, 1, cap, d), lambda bi, ei: (bi, ei, 0, 0)),
        out_shape=jax.ShapeDtypeStruct((b, e, cap, d), BF16),
        scratch_shapes=[pltpu.VMEM((d, tj), F32), pltpu.VMEM((tj, 1), F32),
                        pltpu.VMEM((d, f), BF16), pltpu.VMEM((d, f), BF16), pltpu.VMEM((f, d), BF16)],
        compiler_params=_cparams(("parallel", "arbitrary")),
        name="moe_ffn",
    )(tile_rng, ht4, post4, afft4, wg, wu, wd)


def _combine_kernel(st_ref, pos_ref, ye_ref, x_ref, gt2_ref, g_ref, o_ref, rest_ref, *, tb):
    b = pl.program_id(0)
    blk = pl.program_id(1)
    cap = ye_ref.shape[2]
    win = tb
    slot = lax.broadcasted_iota(jnp.int32, (tb, win), 1).astype(F32)
    pos = pos_ref[0]

    def window_base(e):
        return jnp.minimum(st_ref[b, e, blk] // COMBINE_ALIGN, (cap - win) // COMBINE_ALIGN) * COMBINE_ALIGN

    def window(e, start):
        return ye_ref[0, e, pl.ds(pl.multiple_of(start, COMBINE_ALIGN), win), :]

    moe = jnp.zeros((tb, ye_ref.shape[3]), F32)
    for e in range(N_EXPERTS):
        base = window_base(e)
        onehot = jnp.where(pos[:, e:e + 1] == slot + base.astype(F32), 1.0, 0.0).astype(BF16)
        moe = moe + _dot(onehot, window(e, base))

    rest_ref[...] = jnp.zeros(rest_ref.shape, F32)
    for e in range(N_EXPERTS):
        done = window_base(e) + win

        @pl.when(st_ref[b, e, blk + 1] > done)
        def _():
            start = jnp.minimum(done, cap - win)
            pcol = pos[:, e:e + 1]
            hit = (pcol == slot + start.astype(F32)) & (pcol >= done.astype(F32))
            rest_ref[...] += _dot(jnp.where(hit, 1.0, 0.0).astype(BF16), window(e, start))

    y = x_ref[0] + gt2_ref[0] * (moe + rest_ref[...])
    ms = jnp.mean(y * y, axis=-1, keepdims=True)
    o_ref[0] = y * lax.rsqrt(ms + EPS) * g_ref[...]


def _combine_call(starts, pos, ye, x_new, gt2, final_g, tb):
    b, e, cap, d = ye.shape
    n = pos.shape[1]
    assert cap % COMBINE_ALIGN == 0 and tb % COMBINE_ALIGN == 0 and cap >= tb
    row = lambda bi, i: (bi, i, 0)
    return pl.pallas_call(
        functools.partial(_combine_kernel, tb=tb),
        grid=(b, n // tb),
        in_specs=[pl.BlockSpec(memory_space=pltpu.SMEM),
                  pl.BlockSpec((1, tb, LANES), row),
                  pl.BlockSpec((1, e, cap, d), lambda bi, i: (bi, 0, 0, 0), pipeline_mode=pl.Buffered(1)),
                  pl.BlockSpec((1, tb, d), row),
                  pl.BlockSpec((1, 1, d), lambda bi, i: (bi, 0, 0)),
                  pl.BlockSpec((1, d), lambda bi, i: (0, 0))],
        out_specs=pl.BlockSpec((1, tb, d), row),
        out_shape=jax.ShapeDtypeStruct((b, n, d), F32),
        scratch_shapes=[pltpu.VMEM((tb, d), F32)],
        compiler_params=_cparams(("parallel", "arbitrary")),
        name="combine",
    )(starts, pos, ye, x_new, gt2, final_g)


def _rope_tables(n):
    t = np.arange(n)
    rows = (t // GRID_W).astype(np.float64)
    cols = (t % GRID_W).astype(np.float64)
    inv_freq = np.power(ROPE_THETA, -np.arange(0, ROPE_AXIS_DIM, 2, dtype=np.float64) / ROPE_AXIS_DIM)
    ang_row = rows[:, None] * inv_freq[None, :]
    ang_col = cols[:, None] * inv_freq[None, :]

    def axis_tables(ang):
        c = np.cos(ang).astype(np.float32)
        s = np.sin(ang).astype(np.float32)
        return np.concatenate([c, c], axis=1), np.concatenate([-s, s], axis=1)

    cr, sr = axis_tables(ang_row)
    cc, sc = axis_tables(ang_col)
    cos64 = np.concatenate([cr, cc], axis=1)
    sin64 = np.concatenate([sr, sc], axis=1)
    return (jnp.asarray(np.concatenate([cos64, cos64], axis=1)),
            jnp.asarray(np.concatenate([sin64, sin64], axis=1)))


def _attn_chunk(nk):
    return max(t for t in range(VT_BLK, 1024 + 1, VT_BLK) if nk % t == 0)


def _pad_lanes(v):
    return jnp.pad(v.reshape(1, -1), ((0, 0), (0, LANES - v.size)))


def kernel(x, c, ctx, c_ctx, w_mod, b_mod, norm1_g, w_in, conv_w, a_log, dt_bias, gdn_norm_g,
           lam_q1, lam_k1, lam_q2, lam_k2, da_subln_g, w_out, norm2_g,
           w_router, w_gate, w_up, w_down, final_g):
    b, n, d = x.shape
    nc = ctx.shape[1]
    layer = 0

    cvec = jnp.concatenate([c, c_ctx[None, :], jnp.zeros((8 - b - 1, d), F32)], axis=0)
    mod = _mod_call(cvec, w_mod[layer], b_mod[layer])
    sh1, sc1, gt1, sh2, sc2, gt2 = [mod[:b, i * d:(i + 1) * d].reshape(b, 1, d) for i in range(6)]
    sh1c = jnp.broadcast_to(mod[b:b + 1, 0:d].reshape(1, 1, d), (b, 1, d))
    sc1c = jnp.broadcast_to(mod[b:b + 1, d:2 * d].reshape(1, 1, d), (b, 1, d))

    w_in_bf = jnp.pad(w_in[layer].astype(BF16), ((0, 0), (0, IN_COLS_PAD - w_in.shape[2])))
    g1 = norm1_g[layer].reshape(1, d)
    cos_l, sin_l = _rope_tables(n)
    cos_c, sin_c = jnp.ones((nc, LANES), F32), jnp.zeros((nc, LANES), F32)
    conv_w8 = jnp.pad(conv_w[layer], ((0, 8 - conv_w.shape[1]), (0, 0)))
    alog_row = _pad_lanes(a_log[layer])
    dtb_row = _pad_lanes(dt_bias[layer])
    gdn_args = (conv_w8, alog_row, dtb_row)
    q, k, vt, ql, kl, vl, gate, gbl, gbtl = _inproj_call(
        x, g1, sh1, sc1, cos_l, sin_l, w_in_bf, *gdn_args, tm=512)
    _, kc, vct, qc, kcg, vcg, _, gbc, gbtc = _inproj_call(
        ctx, g1, sh1c, sc1c, cos_c, sin_c, w_in_bf, *gdn_args, tm=nc)

    lam_vecs = jnp.stack([lam_q1[layer], lam_k1[layer], lam_q2[layer], lam_k2[layer]], axis=0)
    k_all = jnp.concatenate([kc, k], axis=1)
    vt_all = jnp.concatenate([vct, vt], axis=2)
    o_da = _attn_call(q, k_all, vt_all, lam_vecs, da_subln_g[layer].reshape(1, DA_DV),
                      tq=2048, tk=_attn_chunk(nc + n))

    zeros_state = jnp.zeros((b, 2, GDN_HEADS, GDN_DK, GDN_DV), F32)
    _, _, s_ctx = _gdn_scan_call(qc, kcg, vcg, gbc, gbtc, zeros_state, sc=nc)
    o_fwd, o_bwd, _ = _gdn_scan_call(ql, kl, vl, gbl, gbtl, s_ctx, sc=256)
    o_dirs = [o_fwd, o_bwd]

    wr_pad = jnp.pad(w_router[layer], ((0, 0), (0, LANES - N_EXPERTS)))
    cap = CAP_FACTOR * n // N_EXPERTS
    tb = 256
    tj = 256
    x_new, ht4, aff, afft = _outproj_call(
        o_da, o_dirs[0], o_dirs[1], gate, x, gt1, gdn_norm_g[layer].reshape(1, GDN_DV),
        w_out[layer].astype(BF16), norm2_g[layer].reshape(1, d), sh2, sc2, wr_pad, tm=512, tb=tb)

    post, pos, starts, tile_rng = _select_call(afft, aff, cap, tb, tj)
    nblk = n // tb
    ye = _moe_ffn_call(tile_rng, ht4, post.reshape(b, N_EXPERTS, nblk, tb),
                       afft.reshape(b, N_EXPERTS, nblk, tb),
                       w_gate[layer], w_up[layer], w_down[layer], cap, tj)
    return _combine_call(starts, pos, ye, x_new, gt2, final_g.reshape(1, d), tb)
```

```python
import functools
import math

import jax
import jax.numpy as jnp
import numpy as np
from jax import lax
from jax.experimental import pallas as pl
from jax.experimental.pallas import tpu as pltpu

F32 = jnp.float32
BF16 = jnp.bfloat16

D_MODEL = 1024
GRID_W = 64
EPS = 1e-6
DA_HEADS = 4
DA_D = 64
DA_DV = 2 * DA_D
ROPE_AXIS_DIM = DA_D // 2
ROPE_THETA = 10000.0
GDN_HEADS = 4
GDN_DK = 128
GDN_DV = 128
CHUNK = 64
N_EXPERTS = 16
CAP_FACTOR = 2
LAM_INIT = 0.8 - 0.6 * math.exp(-0.3 * 0)

DA_QK = DA_HEADS * 2 * DA_D
DA_V = DA_HEADS * DA_DV
GDN_QK = GDN_HEADS * GDN_DK
GDN_V = GDN_HEADS * GDN_DV
GDN_QKV = 2 * GDN_QK + GDN_V
GDN_AB = 2 * 2 * GDN_HEADS
COL_Q, COL_K, COL_V = 0, DA_QK, 2 * DA_QK
COL_QKV = 2 * DA_QK + DA_V
COL_GATE = COL_QKV + GDN_QKV
COL_AB = COL_GATE + GDN_V
LANES = 128
IN_COLS_PAD = COL_AB + LANES
MXU_WIDTH = 256
ATTN_COL_TILE = MXU_WIDTH
VT_ROWS = DA_DV + 8
VT_BLK = MXU_WIDTH

VMEM_LIMIT = 56 * 1024 * 1024
NEG_BIG = -1e30
LOG2_E = math.log2(math.e)


def _cparams(sem):
    return pltpu.CompilerParams(dimension_semantics=sem, vmem_limit_bytes=VMEM_LIMIT)


def _sigmoid(x):
    return 1.0 / (1.0 + jnp.exp(-x))


def _silu(x):
    return x * _sigmoid(x)


def _split3(a):
    a1 = a.astype(BF16)
    r1 = a - a1.astype(F32)
    a2 = r1.astype(BF16)
    a3 = (r1 - a2.astype(F32)).astype(BF16)
    return a1, a2, a3


def _dot(a, b):
    return jnp.dot(a, b, preferred_element_type=F32)


def _dot_x3(a, b):
    a1, a2, _ = _split3(a)
    b1, b2, _ = _split3(b)
    return _dot(a1, b1) + (_dot(a1, b2) + _dot(a2, b1))


def _dot_nt(a, b):
    return lax.dot_general(a, b, (((1,), (1,)), ((), ())), preferred_element_type=F32)


def _dot_tn(a, b):
    return lax.dot_general(a, b, (((0,), (0,)), ((), ())), preferred_element_type=F32)


def _mod_kernel(c_ref, w_ref, b_ref, o_ref):
    s = _silu(c_ref[...])
    o_ref[...] = jnp.dot(s, w_ref[...], precision=lax.Precision.HIGHEST,
                         preferred_element_type=F32) + b_ref[...]


def _mod_call(cvec, w_mod, b_mod):
    d, n = w_mod.shape
    tn = 1024
    return pl.pallas_call(
        _mod_kernel,
        grid=(n // tn,),
        in_specs=[pl.BlockSpec((8, d), lambda j: (0, 0)),
                  pl.BlockSpec((d, tn), lambda j: (0, j)),
                  pl.BlockSpec((1, tn), lambda j: (0, j))],
        out_specs=pl.BlockSpec((8, tn), lambda j: (0, j)),
        out_shape=jax.ShapeDtypeStruct((8, n), F32),
        compiler_params=_cparams(("arbitrary",)),
        name="mod",
    )(cvec, w_mod, b_mod.reshape(1, n))


def _inproj_kernel(x_ref, xp_ref, xn_ref, g_ref, sh_ref, sc_ref, cos_ref, sin_ref, w_ref,
                   cw_ref, al_ref, dtb_ref,
                   q_ref, k_ref, vt_ref, gq_ref, gk_ref, gv_ref, gate_ref, gb_ref, gbt_ref):
    i = pl.program_id(1)
    nblk = pl.num_programs(1)
    tm = x_ref.shape[1]

    def modulated(xb):
        ms = jnp.mean(xb * xb, axis=-1, keepdims=True)
        hm = (xb * lax.rsqrt(ms + EPS) * g_ref[...]) * (1.0 + sc_ref[0]) + sh_ref[0]
        return hm.astype(BF16)

    sb = VT_BLK
    nsb = tm // sb
    lane = lax.broadcasted_iota(jnp.int32, (sb, LANES), 1)
    first_half = (lane % ROPE_AXIS_DIM) < (ROPE_AXIS_DIM // 2)
    extra = lax.broadcasted_iota(jnp.int32, (DA_HEADS, VT_ROWS - DA_DV, sb), 1)
    ones_row = jnp.where(extra == 0, 1.0, 0.0).astype(BF16)
    row = lax.broadcasted_iota(jnp.int32, (sb, 1), 0)
    cw = cw_ref[...]

    halo = _dot(modulated(jnp.concatenate([xp_ref[0], xn_ref[0]], axis=0)),
                w_ref[:, COL_QKV:COL_GATE])
    tile_prev = jnp.where(i > 0, halo[7:8, :], 0.0)
    tile_next = jnp.where(i < nblk - 1, halo[8:9, :], 0.0)

    def project(c):
        hb = modulated(x_ref[0, c * sb:(c + 1) * sb, :])
        cuts = (COL_Q, COL_K, COL_V, COL_QKV, COL_GATE, COL_AB, IN_COLS_PAD)
        return [_dot(hb, w_ref[:, lo:hi]) for lo, hi in zip(cuts[:-1], cuts[1:])]

    def rope(t, cos, sin):
        outs = []
        for j in range(t.shape[1] // LANES):
            s = t[:, LANES * j:LANES * (j + 1)]
            partner = jnp.where(first_half,
                                pltpu.roll(s, LANES - ROPE_AXIS_DIM // 2, 1),
                                pltpu.roll(s, ROPE_AXIS_DIM // 2, 1))
            outs.append(s * cos + partner * sin)
        return jnp.concatenate(outs, axis=1)

    def l2n(t):
        return t * lax.rsqrt(jnp.sum(t * t, axis=-1, keepdims=True) + EPS)

    def finish(c, cur, prev, nxt):
        rows = slice(c * sb, (c + 1) * sb)
        q, k, v, qkv, gate, ab = cur
        cos = cos_ref[rows, :]
        sin = sin_ref[rows, :]
        q_ref[0, rows, :] = (rope(q, cos, sin) * (DA_D ** -0.5 * LOG2_E)).astype(BF16)
        k_ref[0, rows, :] = rope(k, cos, sin).astype(BF16)
        vt_ref[0, :, c, 0:DA_DV, :] = v.T.astype(BF16).reshape(DA_HEADS, DA_DV, sb)
        vt_ref[0, :, c, DA_DV:VT_ROWS, :] = ones_row
        gate_ref[0, rows, :] = gate

        qkv_m1 = jnp.where(row == 0, prev, pltpu.roll(qkv, 1, 0))
        qkv_p1 = jnp.where(row == sb - 1, nxt, pltpu.roll(qkv, sb - 1, 0))
        s = _silu(qkv_m1 * cw[0:1] + qkv * cw[1:2] + qkv_p1 * cw[2:3])
        for h in range(GDN_HEADS):
            lo, hi = h * GDN_DK, (h + 1) * GDN_DK
            gq_ref[0, rows, lo:hi] = l2n(s[:, lo:hi]) * GDN_DK ** -0.5
            gk_ref[0, rows, lo:hi] = l2n(s[:, GDN_QK + lo:GDN_QK + hi])
        gv_ref[0, rows, :] = s[:, 2 * GDN_QK:]

        z = ab + dtb_ref[...]
        softplus = jnp.maximum(z, 0.0) + jnp.log1p(jnp.exp(-jnp.abs(z)))
        gval = -jnp.exp(al_ref[...]) * softplus
        gb = jnp.where(lane < GDN_AB // 2, gval, jnp.where(lane < GDN_AB, _sigmoid(ab), 0.0))
        gb_ref[0, rows, :] = gb
        gbt_ref[0, :, rows] = gb.T[:GDN_AB, :]

    res = {0: project(0)}
    for c in range(nsb):
        if c + 1 < nsb:
            res[c + 1] = project(c + 1)
        prev = res[c - 1][3][sb - 1:sb, :] if c > 0 else tile_prev
        nxt = res[c + 1][3][0:1, :] if c + 1 < nsb else tile_next
        finish(c, res[c], prev, nxt)
        res.pop(c - 1, None)


def _inproj_call(x, norm_g, shift, scale, cos, sin, w_bf, conv_w8, alog_row, dtb_row, tm):
    b, n, d = x.shape
    nb8 = n // 8
    step8 = tm // 8
    row = lambda bi, i: (bi, i, 0)
    vec = lambda bi, i: (bi, 0, 0)
    const = lambda bi, i: (0, 0)
    rows_out = [(DA_QK, BF16), (DA_QK, BF16), None, (GDN_QK, F32), (GDN_QK, F32), (GDN_V, F32),
                (GDN_V, F32), (LANES, F32)]
    out_specs = [pl.BlockSpec((1, tm, o[0]), row) if o else
                 pl.BlockSpec((1, DA_HEADS, tm // VT_BLK, VT_ROWS, VT_BLK), lambda bi, i: (bi, 0, i, 0, 0))
                 for o in rows_out]
    out_shape = [jax.ShapeDtypeStruct((b, n, o[0]), o[1]) if o else
                 jax.ShapeDtypeStruct((b, DA_HEADS, n // VT_BLK, VT_ROWS, VT_BLK), BF16) for o in rows_out]
    out_specs.append(pl.BlockSpec((1, GDN_AB, tm), lambda bi, i: (bi, 0, i)))
    out_shape.append(jax.ShapeDtypeStruct((b, GDN_AB, n), F32))
    return pl.pallas_call(
        _inproj_kernel,
        grid=(b, n // tm),
        in_specs=[pl.BlockSpec((1, tm, d), row),
                  pl.BlockSpec((1, 8, d), lambda bi, i: (bi, jnp.maximum(i * step8 - 1, 0), 0)),
                  pl.BlockSpec((1, 8, d), lambda bi, i: (bi, jnp.minimum((i + 1) * step8, nb8 - 1), 0)),
                  pl.BlockSpec((1, d), const),
                  pl.BlockSpec((1, 1, d), vec),
                  pl.BlockSpec((1, 1, d), vec),
                  pl.BlockSpec((tm, LANES), lambda bi, i: (i, 0)),
                  pl.BlockSpec((tm, LANES), lambda bi, i: (i, 0)),
                  pl.BlockSpec((d, IN_COLS_PAD), const, pipeline_mode=pl.Buffered(1)),
                  pl.BlockSpec((8, GDN_QKV), const),
                  pl.BlockSpec((1, LANES), const),
                  pl.BlockSpec((1, LANES), const)],
        out_specs=out_specs,
        out_shape=out_shape,
        compiler_params=_cparams(("parallel", "parallel")),
        name="inproj",
    )(x, x, x, norm_g, shift, scale, cos, sin, w_bf, conv_w8, alog_row, dtb_row)


def _attn_kernel(q_ref, k_ref, vt_ref, lam_ref, g_ref, o_ref,
                 q2_ref, acc_ref, m_ref, s_ref, p_ref, a_ref, *, tk):
    q = q_ref[0]
    tq = q.shape[0]
    lane = lax.broadcasted_iota(jnp.int32, (tq, LANES), 1)
    zero = jnp.zeros_like(q)
    q2_ref[0:tq, :] = jnp.where(lane < DA_D, q, zero)
    q2_ref[tq:2 * tq, :] = jnp.where(lane >= DA_D, q, zero)
    m_ref[...] = jnp.full(m_ref.shape, NEG_BIG, F32)
    acc_ref[...] = jnp.zeros(acc_ref.shape, F32)

    nct = 2 * tq // ATTN_COL_TILE
    col = lambda ct: slice(ct * ATTN_COL_TILE, (ct + 1) * ATTN_COL_TILE)

    def scores(kk, ct):
        return _dot_nt(kk, q2_ref[col(ct), :])

    def softmax_update(ct, s):
        cols = col(ct)
        m_old = m_ref[:, cols]
        m_new = jnp.maximum(m_old, jnp.max(s, axis=0, keepdims=True))
        alpha = jnp.exp2(m_old - m_new)
        p = jnp.exp2(s - m_new)
        m_ref[:, cols] = m_new
        return p.astype(BF16), alpha

    vblk = vt_ref.shape[4]
    nsub = tk // vblk

    def value_update(ct, j, p, alpha):
        cols = col(ct)
        pv = _dot(vt_ref[0, 0, j * nsub], p[0:vblk, :])
        for c in range(1, nsub):
            pv = pv + _dot(vt_ref[0, 0, j * nsub + c], p[c * vblk:(c + 1) * vblk, :])
        acc_ref[:, cols] = alpha * acc_ref[:, cols] + pv

    nchunk = k_ref.shape[1] // tk
    last = nct - 1
    s_ref[...] = scores(k_ref[0, 0:tk, :], 0)
    p_ref[...] = jnp.zeros(p_ref.shape, BF16)
    a_ref[...] = jnp.ones(a_ref.shape, F32)

    def body(j, carry):
        kk = k_ref[0, pl.ds(pl.multiple_of(j * tk, vblk), tk), :]
        s_cur = s_ref[...]
        pending = (last, jnp.maximum(j - 1, 0), p_ref[...], a_ref[...])
        for ct in range(nct):
            if ct < last:
                s_ahead = scores(kk, ct + 1)
            else:
                jn = jnp.minimum(j + 1, nchunk - 1)
                s_ahead = scores(k_ref[0, pl.ds(pl.multiple_of(jn * tk, vblk), tk), :], 0)
            p, alpha = softmax_update(ct, s_cur)
            value_update(*pending)
            pending = (ct, j, p, alpha)
            s_cur = s_ahead
        s_ref[...] = s_cur
        p_ref[...] = pending[2]
        a_ref[...] = pending[3]
        return carry

    lax.fori_loop(0, nchunk, body, 0)
    value_update(last, nchunk - 1, p_ref[...], a_ref[...])

    lv = lam_ref[...]
    lam = (jnp.exp(jnp.sum(lv[0:1] * lv[1:2], axis=1, keepdims=True))
           - jnp.exp(jnp.sum(lv[2:3] * lv[3:4], axis=1, keepdims=True)) + LAM_INIT)
    den = acc_ref[DA_DV:DA_DV + 1, :]
    ot = (acc_ref[0:DA_DV, 0:tq] / den[:, 0:tq]
          - lam * (acc_ref[0:DA_DV, tq:2 * tq] / den[:, tq:2 * tq]))
    o = ot.T
    ms = jnp.mean(o * o, axis=-1, keepdims=True)
    o_ref[0] = ((o * lax.rsqrt(ms + EPS) * g_ref[...]) * (1.0 - LAM_INIT)).astype(o_ref.dtype)


def _attn_call(q, k_all, vt_all, lam_vecs, subln_g, tq, tk):
    b, n, _ = q.shape
    nk = k_all.shape[1]
    _, _, nvt, _, vblk = vt_all.shape
    assert nk == nvt * vblk and tk % vblk == 0 and nk % tk == 0
    return pl.pallas_call(
        functools.partial(_attn_kernel, tk=tk),
        grid=(b, DA_HEADS, n // tq),
        in_specs=[pl.BlockSpec((1, tq, LANES), lambda bi, h, i: (bi, i, h)),
                  pl.BlockSpec((1, nk, LANES), lambda bi, h, i: (bi, 0, h)),
                  pl.BlockSpec((1, 1, nvt, VT_ROWS, vblk), lambda bi, h, i: (bi, h, 0, 0, 0)),
                  pl.BlockSpec((4, DA_D), lambda bi, h, i: (0, 0)),
                  pl.BlockSpec((1, DA_DV), lambda bi, h, i: (0, 0))],
        out_specs=pl.BlockSpec((1, tq, LANES), lambda bi, h, i: (bi, i, h)),
        out_shape=jax.ShapeDtypeStruct((b, n, DA_V), BF16),
        scratch_shapes=[pltpu.VMEM((2 * tq, LANES), BF16), pltpu.VMEM((VT_ROWS, 2 * tq), F32),
                        pltpu.VMEM((1, 2 * tq), F32),
                        pltpu.VMEM((tk, ATTN_COL_TILE), F32), pltpu.VMEM((tk, ATTN_COL_TILE), BF16),
                        pltpu.VMEM((1, ATTN_COL_TILE), F32)],
        compiler_params=_cparams(("parallel", "parallel", "parallel")),
        name="attn",
    )(q, k_all, vt_all, lam_vecs, subln_g)


def _solve_unit_triangular(a_list, rhs_list, bd_mask):
    def mm(xs, ys):
        return [_dot(x.astype(BF16), y.astype(BF16)) for x, y in zip(xs, ys)]

    d = [jnp.where(bd_mask, a, 0.0) for a in a_list]
    n = [a - di for a, di in zip(a_list, d)]
    d2 = mm(d, d)
    dd2 = mm(d, d2)
    p = [(-di + d2i) - t for di, d2i, t in zip(d, d2, dd2)]
    d4 = mm(d2, d2)
    pd4 = mm(p, d4)
    p = [pi + d4i + t for pi, d4i, t in zip(p, d4, pd4)]
    d8 = mm(d4, d4)
    pd8 = mm(p, d8)
    p = [pi + d8i + t for pi, d8i, t in zip(p, d8, pd8)]
    m = [ni + t for ni, t in zip(n, mm(p, n))]
    z = [ri + t for ri, t in zip(rhs_list, mm(p, rhs_list))]
    m2 = mm(m, m)
    z = [zi + t for zi, t in zip(z, mm(m2, z))]
    return [zi - t for zi, t in zip(z, mm(m, z))]


def _gdn_scan_kernel(qf_ref, kf_ref, vf_ref, gbf_ref, gbtf_ref, qb_ref, kb_ref, vb_ref, gbb_ref, gbtb_ref,
                     s0_ref, of_ref, ob_ref, sfin_ref, s_ref):
    i = pl.program_id(1)

    @pl.when(i == 0)
    def _():
        s_ref[...] = s0_ref[0]

    c64 = CHUNK
    nchunk = qf_ref.shape[1] // c64
    ri = lax.broadcasted_iota(jnp.int32, (c64, c64), 0)
    ci = lax.broadcasted_iota(jnp.int32, (c64, c64), 1)
    bd_mask = (ri // 16) == (ci // 16)
    lower_incl, upper_incl = ri >= ci, ri <= ci
    heads = range(GDN_HEADS)
    rows = [slice(c * c64, (c + 1) * c64) for c in range(nchunk)]
    cols = [slice(h * GDN_DK, (h + 1) * GDN_DK) for h in heads]

    dirs = []
    for d, refs in enumerate(((qf_ref, kf_ref, vf_ref, gbf_ref, gbtf_ref, of_ref),
                              (qb_ref, kb_ref, vb_ref, gbb_ref, gbtb_ref, ob_ref))):
        rev = d == 1
        incl = upper_incl if rev else lower_incl
        dirs.append(dict(
            refs=refs, incl=incl, strict=(ri < ci) if rev else (ri > ci),
            tri_c=jnp.where(incl, 1.0, 0.0).astype(BF16),
            tri_r=jnp.where(lower_incl if rev else upper_incl, 1.0, 0.0).astype(BF16),
            last=0 if rev else c64 - 1,
            order=list(range(nchunk - 1, -1, -1) if rev else range(nchunk))))

    gall, gc_col, gc_row = {}, {}, {}
    for d, dr in enumerate(dirs):
        gb_ref, gbt_ref = dr["refs"][3], dr["refs"][4]
        for c in range(nchunk):
            gall[d, c] = gb_ref[0, rows[c], :]
            g1, g2, g3 = _split3(gall[d, c])
            tc, tr = dr["tri_c"], dr["tri_r"]
            gc_col[d, c] = _dot(tc, g1) + _dot(tc, g2) + _dot(tc, g3)
            t1, t2, t3 = _split3(gbt_ref[0, :, rows[c]])
            gc_row[d, c] = _dot(t1, tr) + _dot(t2, tr) + _dot(t3, tr)

    probs = [(d, c, h) for d in range(2) for c in range(nchunk) for h in heads]
    ln = lambda d, h: d * GDN_HEADS + h
    k = {(d, c, h): dirs[d]["refs"][1][0, rows[c], cols[h]] for d, c, h in probs}
    q = {(d, c, h): dirs[d]["refs"][0][0, rows[c], cols[h]] for d, c, h in probs}
    kbf = {p: k[p].astype(BF16) for p in probs}
    kk = {p: _dot_nt(kbf[p], kbf[p]) for p in probs}
    qk = {p: _dot_nt(q[p].astype(BF16), kbf[p]) for p in probs}
    a_list, rhs_list = [], []
    kd, qd, aqk, glast = {}, {}, {}, {}
    for p in probs:
        d, c, h = p
        dr = dirs[d]
        lane = ln(d, h)
        gcol = gc_col[d, c][:, lane:lane + 1]
        bcol = gall[d, c][:, GDN_AB // 2 + lane:GDN_AB // 2 + lane + 1]
        diff = gcol - gc_row[d, c][lane:lane + 1, :]
        e_strict = jnp.where(dr["strict"], jnp.exp(jnp.where(dr["strict"], diff, 0.0)), 0.0)
        e_incl = jnp.where(dr["incl"], jnp.exp(jnp.where(dr["incl"], diff, 0.0)), 0.0)
        eg = jnp.exp(gcol)
        glast[p] = gcol[dr["last"]:dr["last"] + 1, :]
        a_list.append(bcol * kk[p] * e_strict)
        v = dr["refs"][2][0, rows[c], cols[h]]
        rhs_list.append(jnp.concatenate([(bcol * eg) * k[p], bcol * v], axis=1))
        kd[p] = (k[p] * jnp.exp(glast[p] - gcol)).astype(BF16)
        qd[p] = (q[p] * eg).astype(BF16)
        aqk[p] = (qk[p] * e_incl).astype(BF16)
    sol = dict(zip(probs, _solve_unit_triangular(a_list, rhs_list, bd_mask)))

    chains = [(d, h) for d in range(2) for h in heads]
    for t in range(nchunk):
        cur = {(d, h): (d, dirs[d]["order"][t], h) for d, h in chains}
        s = {dh: s_ref[dh[0], dh[1]] for dh in chains}
        sb = {dh: s[dh].astype(BF16) for dh in chains}
        ws = {dh: _dot(sol[cur[dh]][:, :GDN_DK].astype(BF16), sb[dh]) for dh in chains}
        qs = {dh: _dot(qd[cur[dh]], sb[dh]) for dh in chains}
        u = {dh: (sol[cur[dh]][:, GDN_DK:] - ws[dh]).astype(BF16) for dh in chains}
        au = {dh: _dot(aqk[cur[dh]], u[dh]) for dh in chains}
        ku = {dh: _dot_tn(kd[cur[dh]], u[dh]) for dh in chains}
        for dh in chains:
            d, c, h = cur[dh]
            s_ref[d, h] = jnp.exp(glast[cur[dh]]) * s[dh] + ku[dh]
            dirs[d]["refs"][5][0, rows[c], cols[h]] = qs[dh] + au[dh]

    @pl.when(i == pl.num_programs(1) - 1)
    def _():
        sfin_ref[0] = s_ref[...]


def _gdn_scan_call(q, k, v, gb, gbt, s0, sc):
    b, n, _ = q.shape
    nsup = n // sc
    fwd = lambda bi, i: (bi, i, 0)
    bwd = lambda bi, i: (bi, nsup - 1 - i, 0)
    st = lambda bi, i: (bi, 0, 0, 0, 0)
    state = (1, 2, GDN_HEADS, GDN_DK, GDN_DV)

    def in_specs(row, tr):
        return [pl.BlockSpec((1, sc, GDN_QK), row), pl.BlockSpec((1, sc, GDN_QK), row),
                pl.BlockSpec((1, sc, GDN_V), row), pl.BlockSpec((1, sc, LANES), row),
                pl.BlockSpec((1, GDN_AB, sc), tr)]

    return pl.pallas_call(
        _gdn_scan_kernel,
        grid=(b, nsup),
        in_specs=(in_specs(fwd, lambda bi, i: (bi, 0, i))
                  + in_specs(bwd, lambda bi, i: (bi, 0, nsup - 1 - i))
                  + [pl.BlockSpec(state, st)]),
        out_specs=[pl.BlockSpec((1, sc, GDN_V), fwd), pl.BlockSpec((1, sc, GDN_V), bwd),
                   pl.BlockSpec(state, st)],
        out_shape=[jax.ShapeDtypeStruct((b, n, GDN_V), F32), jax.ShapeDtypeStruct((b, n, GDN_V), F32),
                   jax.ShapeDtypeStruct((b,) + state[1:], F32)],
        scratch_shapes=[pltpu.VMEM(state[1:], F32)],
        compiler_params=_cparams(("parallel", "arbitrary")),
        name="gdn_scan",
    )(q, k, v, gb, gbt, q, k, v, gb, gbt, s0)


def _outproj_kernel(oda_ref, of_ref, ob_ref, gate_ref, x_ref, gt1_ref, gng_ref, wout_ref,
                    n2g_ref, sh2_ref, sc2_ref, wr_ref, xnew_ref, ht_ref, aff_ref, afft_ref):
    tb = ht_ref.shape[3]
    w1, w2, _ = _split3(wr_ref[...])

    def mix_and_project(c):
        rows = slice(c * tb, (c + 1) * tb)
        og = of_ref[0, rows, :] + ob_ref[0, rows, :]
        gate = gate_ref[0, rows, :]
        ys = []
        for h in range(GDN_HEADS):
            cols = slice(h * GDN_DV, (h + 1) * GDN_DV)
            t = og[:, cols]
            y = t * lax.rsqrt(jnp.mean(t * t, axis=-1, keepdims=True) + EPS) * gng_ref[...]
            ys.append((y * _silu(gate[:, cols])).astype(BF16))
        y_gdn = jnp.concatenate(ys, axis=1)
        proj = _dot(oda_ref[0, rows, :], wout_ref[0:DA_V, :]) + _dot(y_gdn, wout_ref[DA_V:, :])
        return x_ref[0, rows, :] + gt1_ref[0] * proj

    def norm_and_route(c, xn):
        rows = slice(c * tb, (c + 1) * tb)
        xnew_ref[0, rows, :] = xn
        ms = jnp.mean(xn * xn, axis=-1, keepdims=True)
        hm = (xn * lax.rsqrt(ms + EPS) * n2g_ref[...]) * (1.0 + sc2_ref[0]) + sh2_ref[0]
        ht_ref[0, c] = hm.T.astype(BF16)
        h1, h2, _ = _split3(hm)
        logits = _dot(h1, w1) + (_dot(h1, w2) + _dot(h2, w1))
        lane = lax.broadcasted_iota(jnp.int32, logits.shape, 1)
        logits = jnp.where(lane < N_EXPERTS, logits, NEG_BIG)
        e = jnp.exp(logits - jnp.max(logits, axis=-1, keepdims=True))
        aff = e / jnp.sum(e, axis=-1, keepdims=True)
        aff_ref[0, rows, :] = aff
        afft_ref[0, :, rows] = aff.T[:N_EXPERTS, :]

    nblk = ht_ref.shape[1]
    xn = mix_and_project(0)
    for c in range(nblk):
        xn_next = mix_and_project(c + 1) if c + 1 < nblk else None
        norm_and_route(c, xn)
        xn = xn_next


def _outproj_call(o_da, o_f, o_b, gate, x, gt1, gdn_norm_g, w_out_bf, norm2_g, sh2, sc2, wr_pad, tm, tb):
    b, n, d = x.shape
    assert tm % tb == 0
    row = lambda bi, i: (bi, i, 0)
    vec = lambda bi, i: (bi, 0, 0)
    const2 = lambda bi, i: (0, 0)
    return pl.pallas_call(
        _outproj_kernel,
        grid=(b, n // tm),
        in_specs=[pl.BlockSpec((1, tm, DA_V), row), pl.BlockSpec((1, tm, GDN_V), row),
                  pl.BlockSpec((1, tm, GDN_V), row), pl.BlockSpec((1, tm, GDN_V), row),
                  pl.BlockSpec((1, tm, d), row), pl.BlockSpec((1, 1, d), vec),
                  pl.BlockSpec((1, GDN_DV), const2), pl.BlockSpec((DA_V + GDN_V, d), const2),
                  pl.BlockSpec((1, d), const2), pl.BlockSpec((1, 1, d), vec), pl.BlockSpec((1, 1, d), vec),
                  pl.BlockSpec((d, LANES), const2)],
        out_specs=[pl.BlockSpec((1, tm, d), row),
                   pl.BlockSpec((1, tm // tb, d, tb), lambda bi, i: (bi, i, 0, 0)),
                   pl.BlockSpec((1, tm, LANES), row),
                   pl.BlockSpec((1, N_EXPERTS, tm), lambda bi, i: (bi, 0, i))],
        out_shape=[jax.ShapeDtypeStruct((b, n, d), F32), jax.ShapeDtypeStruct((b, n // tb, d, tb), BF16),
                   jax.ShapeDtypeStruct((b, n, LANES), F32),
                   jax.ShapeDtypeStruct((b, N_EXPERTS, n), F32)],
        compiler_params=_cparams(("parallel", "parallel")),
        name="outproj",
    )(o_da, o_f, o_b, gate, x, gt1, gdn_norm_g, w_out_bf, norm2_g, sh2, sc2, wr_pad)


GATHER_GROUP = 8
COMBINE_ALIGN = 16
PREFIX_BLK = 256
BISECT_STEPS = 64


def _select_kernel(afft_ref, aff_ref, post_ref, pos_ref, st_ref, rng_ref, *, cap, tb, tj):
    at = afft_ref[0]
    n = at.shape[1]

    def count_ge(t):
        return jnp.sum(jnp.where(at >= t, 1.0, 0.0), axis=1, keepdims=True)

    def bisect(_, bounds):
        lo, hi = bounds
        mid = 0.5 * (lo + hi)
        ok = count_ge(mid) >= cap
        return jnp.where(ok, mid, lo), jnp.where(ok, hi, mid)

    _, hi = lax.fori_loop(0, BISECT_STEPS, bisect,
                          (jnp.zeros((N_EXPERTS, 1), F32), jnp.full((N_EXPERTS, 1), 2.0, F32)))

    def below(h):
        return jnp.max(jnp.where(at < h, at, -1.0), axis=1, keepdims=True)

    def short(state):
        return jnp.sum(jnp.where(count_ge(state[0]) < cap, 1.0, 0.0)) > 0.0

    def step_down(state):
        t, h = state
        lacking = count_ge(t) < cap
        h = jnp.where(lacking, t, h)
        return jnp.where(lacking, below(h), t), h

    thr, _ = lax.while_loop(short, step_down, (below(hi), hi))
    need = cap - jnp.sum(jnp.where(at > thr, 1.0, 0.0), axis=1, keepdims=True)

    pi = lax.broadcasted_iota(jnp.int32, (PREFIX_BLK, PREFIX_BLK), 0)
    pj = lax.broadcasted_iota(jnp.int32, (PREFIX_BLK, PREFIX_BLK), 1)
    upper = jnp.where(pi <= pj, 1.0, 0.0).astype(BF16)
    lower = jnp.where(pi >= pj, 1.0, 0.0).astype(BF16)

    def prefix_lanes(m):
        carry = jnp.zeros((m.shape[0], 1), F32)
        outs = []
        for c in range(m.shape[1] // PREFIX_BLK):
            mc = m[:, c * PREFIX_BLK:(c + 1) * PREFIX_BLK]
            loc = _dot(mc.astype(BF16), upper)
            outs.append(loc - mc + carry)
            carry = carry + loc[:, PREFIX_BLK - 1:PREFIX_BLK]
        return jnp.concatenate(outs, axis=1)

    def prefix_rows(m):
        carry = jnp.zeros((1, m.shape[1]), F32)
        outs = []
        for c in range(m.shape[0] // PREFIX_BLK):
            mc = m[c * PREFIX_BLK:(c + 1) * PREFIX_BLK, :]
            loc = _dot(lower, mc.astype(BF16))
            outs.append(loc - mc + carry)
            carry = carry + loc[PREFIX_BLK - 1:PREFIX_BLK, :]
        return jnp.concatenate(outs, axis=0)

    eq_t = jnp.where(at == thr, 1.0, 0.0)
    sel_t = jnp.where(at > thr, 1.0, jnp.where(prefix_lanes(eq_t) < need, eq_t, 0.0))
    post_ref[0] = jnp.where(sel_t > 0.0, prefix_lanes(sel_t), -1.0)

    ti = lax.broadcasted_iota(jnp.int32, (n, LANES), 0)
    bi = lax.broadcasted_iota(jnp.int32, (n, LANES), 1)
    before = jnp.where(ti < bi * tb, 1.0, 0.0).astype(BF16)
    st = _dot(sel_t.astype(BF16), before)
    st_ref[0] = st.astype(jnp.int32)

    through = jnp.where(ti < (bi + 1) * tb, 1.0, 0.0).astype(BF16)
    st_end = _dot(sel_t.astype(BF16), through)
    blk_lane = lax.broadcasted_iota(jnp.int32, (N_EXPERTS, LANES), 1)
    is_blk = blk_lane < n // tb
    rng = jnp.zeros((N_EXPERTS, LANES), F32)
    ntile = cap // tj
    for j in range(ntile):
        first = jnp.sum(jnp.where(is_blk & (st_end <= j * tj), 1.0, 0.0), axis=1, keepdims=True)
        last = jnp.sum(jnp.where(is_blk & (st < (j + 1) * tj), 1.0, 0.0), axis=1, keepdims=True) - 1.0
        rng = rng + jnp.where(blk_lane == j, first, 0.0) + jnp.where(blk_lane == ntile + j, last, 0.0)
    rng_ref[0] = rng.astype(jnp.int32)

    a = aff_ref[0]
    er = lax.broadcasted_iota(jnp.int32, (N_EXPERTS, LANES), 0)
    ec = lax.broadcasted_iota(jnp.int32, (N_EXPERTS, LANES), 1)
    diag = er == ec
    thr_row = jnp.sum(jnp.where(diag, thr, 0.0), axis=0, keepdims=True)
    need_row = jnp.sum(jnp.where(diag, need, 0.0), axis=0, keepdims=True)
    valid = lax.broadcasted_iota(jnp.int32, a.shape, 1) < N_EXPERTS
    eq = jnp.where(valid & (a == thr_row), 1.0, 0.0)
    sel = jnp.where(valid & (a > thr_row), 1.0, jnp.where(prefix_rows(eq) < need_row, eq, 0.0))
    pos_ref[0] = jnp.where(sel > 0.0, prefix_rows(sel), -1.0)


def _select_call(afft, aff, cap, tb, tj):
    b, e, n = afft.shape
    assert n // tb + 1 <= LANES and 2 * (cap // tj) <= LANES
    return pl.pallas_call(
        functools.partial(_select_kernel, cap=cap, tb=tb, tj=tj),
        grid=(b,),
        in_specs=[pl.BlockSpec((1, e, n), lambda bi: (bi, 0, 0)),
                  pl.BlockSpec((1, n, LANES), lambda bi: (bi, 0, 0))],
        out_specs=[pl.BlockSpec((1, e, n), lambda bi: (bi, 0, 0)),
                   pl.BlockSpec((1, n, LANES), lambda bi: (bi, 0, 0)),
                   pl.BlockSpec((1, e, LANES), lambda bi: (bi, 0, 0)),
                   pl.BlockSpec((1, e, LANES), lambda bi: (bi, 0, 0))],
        out_shape=[jax.ShapeDtypeStruct((b, e, n), F32), jax.ShapeDtypeStruct((b, n, LANES), F32),
                   jax.ShapeDtypeStruct((b, e, LANES), jnp.int32),
                   jax.ShapeDtypeStruct((b, e, LANES), jnp.int32)],
        compiler_params=_cparams(("parallel",)),
        name="select",
    )(afft, aff)


def _moe_ffn_kernel(rng_ref, ht_ref, post_ref, afft_ref, wg_ref, wu_ref, wd_ref, ye_ref,
                    acc_ref, gacc_ref, wgb_ref, wub_ref, wdb_ref, *, tj, tb):
    b = pl.program_id(0)
    e = pl.program_id(1)
    cap = ye_ref.shape[2]
    ntile = cap // tj
    slot = lax.broadcasted_iota(jnp.int32, (tj, tb), 0).astype(F32)
    wgb_ref[...] = wg_ref[0].astype(BF16)
    wub_ref[...] = wu_ref[0].astype(BF16)
    wdb_ref[...] = wd_ref[0].astype(BF16)

    for j in range(cap // tj):
        lo = j * tj
        acc_ref[...] = jnp.zeros(acc_ref.shape, F32)
        gacc_ref[...] = jnp.zeros(gacc_ref.shape, F32)

        first = rng_ref[b, e, j]
        last = rng_ref[b, e, ntile + j]

        def gather_blocks(start, count):
            part = jnp.zeros(acc_ref.shape, F32)
            gate = jnp.zeros(gacc_ref.shape, F32)
            for u in range(count):
                bi = start + u
                prow = post_ref[0, 0, pl.ds(bi, 1), :]
                hit = prow == (slot + float(lo))
                onehot = jnp.where(hit, 1.0, 0.0).astype(BF16)
                part = part + _dot_nt(ht_ref[0, bi], onehot)
                arow = afft_ref[0, 0, pl.ds(bi, 1), :]
                gate = gate + jnp.sum(jnp.where(hit, arow, 0.0), axis=1, keepdims=True)
            acc_ref[...] += part
            gacc_ref[...] += gate

        def group_body(g, carry):
            gather_blocks(first + g * GATHER_GROUP, GATHER_GROUP)
            return carry

        nblocks = last - first + 1
        full = nblocks // GATHER_GROUP
        lax.fori_loop(0, full, group_body, 0)
        done = first + full * GATHER_GROUP
        size = GATHER_GROUP // 2
        while size >= 1:
            @pl.when((nblocks & size) != 0)
            def _(done=done, size=size):
                gather_blocks(done, size)
            done = done + (nblocks & size)
            size //= 2
        xe = acc_ref[...].T.astype(BF16)
        hid = _silu(_dot(xe, wgb_ref[...])) * _dot(xe, wub_ref[...])
        ye = _dot(hid.astype(BF16), wdb_ref[...]) * gacc_ref[...]
        ye_ref[0, 0, lo:lo + tj, :] = ye.astype(ye_ref.dtype)


def _moe_ffn_call(tile_rng, ht4, post4, afft4, wg, wu, wd, cap, tj):
    b, nblk, d, tb = ht4.shape
    e, _, f = wg.shape
    wspec = lambda shp: pl.BlockSpec((1,) + shp, lambda bi, ei: (ei, 0, 0))
    return pl.pallas_call(
        functools.partial(_moe_ffn_kernel, tj=tj, tb=tb),
        grid=(b, e),
        in_specs=[pl.BlockSpec(memory_space=pltpu.SMEM),
                  pl.BlockSpec((1, nblk, d, tb), lambda bi, ei: (bi, 0, 0, 0), pipeline_mode=pl.Buffered(1)),
                  pl.BlockSpec((1, 1, nblk, tb), lambda bi, ei: (bi, ei, 0, 0)),
                  pl.BlockSpec((1, 1, nblk, tb), lambda bi, ei: (bi, ei, 0, 0)),
                  wspec((d, f)), wspec((d, f)), wspec((f, d))],
        out_specs=pl.BlockSpec((1, 1, cap, d), lambda bi, ei: (bi, ei, 0, 0)),
        out_shape=jax.ShapeDtypeStruct((b, e, cap, d), BF16),
        scratch_shapes=[pltpu.VMEM((d, tj), F32), pltpu.VMEM((tj, 1), F32),
                        pltpu.VMEM((d, f), BF16), pltpu.VMEM((d, f), BF16), pltpu.VMEM((f, d), BF16)],
        compiler_params=_cparams(("parallel", "arbitrary")),
        name="moe_ffn",
    )(tile_rng, ht4, post4, afft4, wg, wu, wd)


def _combine_kernel(st_ref, pos_ref, ye_ref, x_ref, gt2_ref, g_ref, o_ref, rest_ref, *, tb):
    b = pl.program_id(0)
    blk = pl.program_id(1)
    cap = ye_ref.shape[2]
    win = tb
    slot = lax.broadcasted_iota(jnp.int32, (tb, win), 1).astype(F32)
    pos = pos_ref[0]

    def window_base(e):
        return jnp.minimum(st_ref[b, e, blk] // COMBINE_ALIGN, (cap - win) // COMBINE_ALIGN) * COMBINE_ALIGN

    def window(e, start):
        return ye_ref[0, e, pl.ds(pl.multiple_of(start, COMBINE_ALIGN), win), :]

    moe = jnp.zeros((tb, ye_ref.shape[3]), F32)
    for e in range(N_EXPERTS):
        base = window_base(e)
        onehot = jnp.where(pos[:, e:e + 1] == slot + base.astype(F32), 1.0, 0.0).astype(BF16)
        moe = moe + _dot(onehot, window(e, base))

    rest_ref[...] = jnp.zeros(rest_ref.shape, F32)
    for e in range(N_EXPERTS):
        done = window_base(e) + win

        @pl.when(st_ref[b, e, blk + 1] > done)
        def _():
            start = jnp.minimum(done, cap - win)
            pcol = pos[:, e:e + 1]
            hit = (pcol == slot + start.astype(F32)) & (pcol >= done.astype(F32))
            rest_ref[...] += _dot(jnp.where(hit, 1.0, 0.0).astype(BF16), window(e, start))

    y = x_ref[0] + gt2_ref[0] * (moe + rest_ref[...])
    ms = jnp.mean(y * y, axis=-1, keepdims=True)
    o_ref[0] = y * lax.rsqrt(ms + EPS) * g_ref[...]


def _combine_call(starts, pos, ye, x_new, gt2, final_g, tb):
    b, e, cap, d = ye.shape
    n = pos.shape[1]
    assert cap % COMBINE_ALIGN == 0 and tb % COMBINE_ALIGN == 0 and cap >= tb
    row = lambda bi, i: (bi, i, 0)
    return pl.pallas_call(
        functools.partial(_combine_kernel, tb=tb),
        grid=(b, n // tb),
        in_specs=[pl.BlockSpec(memory_space=pltpu.SMEM),
                  pl.BlockSpec((1, tb, LANES), row),
                  pl.BlockSpec((1, e, cap, d), lambda bi, i: (bi, 0, 0, 0), pipeline_mode=pl.Buffered(1)),
                  pl.BlockSpec((1, tb, d), row),
                  pl.BlockSpec((1, 1, d), lambda bi, i: (bi, 0, 0)),
                  pl.BlockSpec((1, d), lambda bi, i: (0, 0))],
        out_specs=pl.BlockSpec((1, tb, d), row),
        out_shape=jax.ShapeDtypeStruct((b, n, d), F32),
        scratch_shapes=[pltpu.VMEM((tb, d), F32)],
        compiler_params=_cparams(("parallel", "arbitrary")),
        name="combine",
    )(starts, pos, ye, x_new, gt2, final_g)


def _rope_tables(n):
    t = np.arange(n)
    rows = (t // GRID_W).astype(np.float64)
    cols = (t % GRID_W).astype(np.float64)
    inv_freq = np.power(ROPE_THETA, -np.arange(0, ROPE_AXIS_DIM, 2, dtype=np.float64) / ROPE_AXIS_DIM)
    ang_row = rows[:, None] * inv_freq[None, :]
    ang_col = cols[:, None] * inv_freq[None, :]

    def axis_tables(ang):
        c = np.cos(ang).astype(np.float32)
        s = np.sin(ang).astype(np.float32)
        return np.concatenate([c, c], axis=1), np.concatenate([-s, s], axis=1)

    cr, sr = axis_tables(ang_row)
    cc, sc = axis_tables(ang_col)
    cos64 = np.concatenate([cr, cc], axis=1)
    sin64 = np.concatenate([sr, sc], axis=1)
    return (jnp.asarray(np.concatenate([cos64, cos64], axis=1)),
            jnp.asarray(np.concatenate([sin64, sin64], axis=1)))


def _attn_chunk(nk):
    return max(t for t in range(VT_BLK, 1024 + 1, VT_BLK) if nk % t == 0)


def _pad_lanes(v):
    return jnp.pad(v.reshape(1, -1), ((0, 0), (0, LANES - v.size)))


def kernel(x, c, ctx, c_ctx, w_mod, b_mod, norm1_g, w_in, conv_w, a_log, dt_bias, gdn_norm_g,
           lam_q1, lam_k1, lam_q2, lam_k2, da_subln_g, w_out, norm2_g,
           w_router, w_gate, w_up, w_down, final_g):
    b, n, d = x.shape
    nc = ctx.shape[1]
    layer = 0

    cvec = jnp.concatenate([c, c_ctx[None, :], jnp.zeros((8 - b - 1, d), F32)], axis=0)
    mod = _mod_call(cvec, w_mod[layer], b_mod[layer])
    sh1, sc1, gt1, sh2, sc2, gt2 = [mod[:b, i * d:(i + 1) * d].reshape(b, 1, d) for i in range(6)]
    sh1c = jnp.broadcast_to(mod[b:b + 1, 0:d].reshape(1, 1, d), (b, 1, d))
    sc1c = jnp.broadcast_to(mod[b:b + 1, d:2 * d].reshape(1, 1, d), (b, 1, d))

    w_in_bf = jnp.pad(w_in[layer].astype(BF16), ((0, 0), (0, IN_COLS_PAD - w_in.shape[2])))
    g1 = norm1_g[layer].reshape(1, d)
    cos_l, sin_l = _rope_tables(n)
    cos_c, sin_c = jnp.ones((nc, LANES), F32), jnp.zeros((nc, LANES), F32)
    conv_w8 = jnp.pad(conv_w[layer], ((0, 8 - conv_w.shape[1]), (0, 0)))
    alog_row = _pad_lanes(a_log[layer])
    dtb_row = _pad_lanes(dt_bias[layer])
    gdn_args = (conv_w8, alog_row, dtb_row)
    q, k, vt, ql, kl, vl, gate, gbl, gbtl = _inproj_call(
        x, g1, sh1, sc1, cos_l, sin_l, w_in_bf, *gdn_args, tm=1024)
    _, kc, vct, qc, kcg, vcg, _, gbc, gbtc = _inproj_call(
        ctx, g1, sh1c, sc1c, cos_c, sin_c, w_in_bf, *gdn_args, tm=nc)

    lam_vecs = jnp.stack([lam_q1[layer], lam_k1[layer], lam_q2[layer], lam_k2[layer]], axis=0)
    k_all = jnp.concatenate([kc, k], axis=1)
    vt_all = jnp.concatenate([vct, vt], axis=2)
    o_da = _attn_call(q, k_all, vt_all, lam_vecs, da_subln_g[layer].reshape(1, DA_DV),
                      tq=2048, tk=_attn_chunk(nc + n))

    zeros_state = jnp.zeros((b, 2, GDN_HEADS, GDN_DK, GDN_DV), F32)
    _, _, s_ctx = _gdn_scan_call(qc, kcg, vcg, gbc, gbtc, zeros_state, sc=nc)
    o_fwd, o_bwd, _ = _gdn_scan_call(ql, kl, vl, gbl, gbtl, s_ctx, sc=256)
    o_dirs = [o_fwd, o_bwd]

    wr_pad = jnp.pad(w_router[layer], ((0, 0), (0, LANES - N_EXPERTS)))
    cap = CAP_FACTOR * n // N_EXPERTS
    tb = 256
    tj = 256
    x_new, ht4, aff, afft = _outproj_call(
        o_da, o_dirs[0], o_dirs[1], gate, x, gt1, gdn_norm_g[layer].reshape(1, GDN_DV),
        w_out[layer].astype(BF16), norm2_g[layer].reshape(1, d), sh2, sc2, wr_pad, tm=1024, tb=tb)

    post, pos, starts, tile_rng = _select_call(afft, aff, cap, tb, tj)
    nblk = n // tb
    ye = _moe_ffn_call(tile_rng, ht4, post.reshape(b, N_EXPERTS, nblk, tb),
                       afft.reshape(b, N_EXPERTS, nblk, tb),
                       w_gate[layer], w_up[layer], w_down[layer], cap, tj)
    return _combine_call(starts, pos, ye, x_new, gt2, final_g.reshape(1, d), tb)
```

```python
import functools
import math

import jax
import jax.numpy as jnp
import numpy as np
from jax import lax
from jax.experimental import pallas as pl
from jax.experimental.pallas import tpu as pltpu

F32 = jnp.float32
BF16 = jnp.bfloat16

D_MODEL = 1024
GRID_W = 64
EPS = 1e-6
DA_HEADS = 4
DA_D = 64
DA_DV = 2 * DA_D
ROPE_AXIS_DIM = DA_D // 2
ROPE_THETA = 10000.0
GDN_HEADS = 4
GDN_DK = 128
GDN_DV = 128
CHUNK = 64
N_EXPERTS = 16
CAP_FACTOR = 2
LAM_INIT = 0.8 - 0.6 * math.exp(-0.3 * 0)

DA_QK = DA_HEADS * 2 * DA_D
DA_V = DA_HEADS * DA_DV
GDN_QK = GDN_HEADS * GDN_DK
GDN_V = GDN_HEADS * GDN_DV
GDN_QKV = 2 * GDN_QK + GDN_V
GDN_AB = 2 * 2 * GDN_HEADS
COL_Q, COL_K, COL_V = 0, DA_QK, 2 * DA_QK
COL_QKV = 2 * DA_QK + DA_V
COL_GATE = COL_QKV + GDN_QKV
COL_AB = COL_GATE + GDN_V
LANES = 128
IN_COLS_PAD = COL_AB + LANES
MXU_WIDTH = 256
ATTN_COL_TILE = MXU_WIDTH
VT_ROWS = DA_DV + 8
VT_BLK = MXU_WIDTH

VMEM_LIMIT = 56 * 1024 * 1024
NEG_BIG = -1e30
LOG2_E = math.log2(math.e)


def _cparams(sem):
    return pltpu.CompilerParams(dimension_semantics=sem, vmem_limit_bytes=VMEM_LIMIT)


def _sigmoid(x):
    return 1.0 / (1.0 + jnp.exp(-x))


def _silu(x):
    return x * _sigmoid(x)


def _split3(a):
    a1 = a.astype(BF16)
    r1 = a - a1.astype(F32)
    a2 = r1.astype(BF16)
    a3 = (r1 - a2.astype(F32)).astype(BF16)
    return a1, a2, a3


def _dot(a, b):
    return jnp.dot(a, b, preferred_element_type=F32)


def _dot_x3(a, b):
    a1, a2, _ = _split3(a)
    b1, b2, _ = _split3(b)
    return _dot(a1, b1) + (_dot(a1, b2) + _dot(a2, b1))


def _dot_nt(a, b):
    return lax.dot_general(a, b, (((1,), (1,)), ((), ())), preferred_element_type=F32)


def _dot_tn(a, b):
    return lax.dot_general(a, b, (((0,), (0,)), ((), ())), preferred_element_type=F32)


def _mod_kernel(c_ref, w_ref, b_ref, o_ref):
    s = _silu(c_ref[...])
    o_ref[...] = jnp.dot(s, w_ref[...], precision=lax.Precision.HIGHEST,
                         preferred_element_type=F32) + b_ref[...]


def _mod_call(cvec, w_mod, b_mod):
    d, n = w_mod.shape
    tn = 1024
    return pl.pallas_call(
        _mod_kernel,
        grid=(n // tn,),
        in_specs=[pl.BlockSpec((8, d), lambda j: (0, 0)),
                  pl.BlockSpec((d, tn), lambda j: (0, j)),
                  pl.BlockSpec((1, tn), lambda j: (0, j))],
        out_specs=pl.BlockSpec((8, tn), lambda j: (0, j)),
        out_shape=jax.ShapeDtypeStruct((8, n), F32),
        compiler_params=_cparams(("arbitrary",)),
        name="mod",
    )(cvec, w_mod, b_mod.reshape(1, n))


def _inproj_kernel(x_ref, xp_ref, xn_ref, g_ref, sh_ref, sc_ref, cos_ref, sin_ref, w_ref,
                   cw_ref, al_ref, dtb_ref,
                   q_ref, k_ref, vt_ref, gq_ref, gk_ref, gv_ref, gate_ref, gb_ref, gbt_ref):
    i = pl.program_id(1)
    nblk = pl.num_programs(1)
    tm = x_ref.shape[1]

    def modulated(xb):
        ms = jnp.mean(xb * xb, axis=-1, keepdims=True)
        hm = (xb * lax.rsqrt(ms + EPS) * g_ref[...]) * (1.0 + sc_ref[0]) + sh_ref[0]
        return hm.astype(BF16)

    sb = VT_BLK
    nsb = tm // sb
    lane = lax.broadcasted_iota(jnp.int32, (sb, LANES), 1)
    first_half = (lane % ROPE_AXIS_DIM) < (ROPE_AXIS_DIM // 2)
    extra = lax.broadcasted_iota(jnp.int32, (DA_HEADS, VT_ROWS - DA_DV, sb), 1)
    ones_row = jnp.where(extra == 0, 1.0, 0.0).astype(BF16)
    row = lax.broadcasted_iota(jnp.int32, (sb, 1), 0)
    cw = cw_ref[...]

    halo = _dot(modulated(jnp.concatenate([xp_ref[0], xn_ref[0]], axis=0)),
                w_ref[:, COL_QKV:COL_GATE])
    tile_prev = jnp.where(i > 0, halo[7:8, :], 0.0)
    tile_next = jnp.where(i < nblk - 1, halo[8:9, :], 0.0)

    def project(c):
        hb = modulated(x_ref[0, c * sb:(c + 1) * sb, :])
        cuts = (COL_Q, COL_K, COL_V, COL_QKV, COL_GATE, COL_AB, IN_COLS_PAD)
        return [_dot(hb, w_ref[:, lo:hi]) for lo, hi in zip(cuts[:-1], cuts[1:])]

    def rope(t, cos, sin):
        outs = []
        for j in range(t.shape[1] // LANES):
            s = t[:, LANES * j:LANES * (j + 1)]
            partner = jnp.where(first_half,
                                pltpu.roll(s, LANES - ROPE_AXIS_DIM // 2, 1),
                                pltpu.roll(s, ROPE_AXIS_DIM // 2, 1))
            outs.append(s * cos + partner * sin)
        return jnp.concatenate(outs, axis=1)

    def l2n(t):
        return t * lax.rsqrt(jnp.sum(t * t, axis=-1, keepdims=True) + EPS)

    def finish(c, cur, prev, nxt):
        rows = slice(c * sb, (c + 1) * sb)
        q, k, v, qkv, gate, ab = cur
        cos = cos_ref[rows, :]
        sin = sin_ref[rows, :]
        q_ref[0, rows, :] = (rope(q, cos, sin) * (DA_D ** -0.5 * LOG2_E)).astype(BF16)
        k_ref[0, rows, :] = rope(k, cos, sin).astype(BF16)
        vt_ref[0, :, c, 0:DA_DV, :] = v.T.astype(BF16).reshape(DA_HEADS, DA_DV, sb)
        vt_ref[0, :, c, DA_DV:VT_ROWS, :] = ones_row
        gate_ref[0, rows, :] = gate

        qkv_m1 = jnp.where(row == 0, prev, pltpu.roll(qkv, 1, 0))
        qkv_p1 = jnp.where(row == sb - 1, nxt, pltpu.roll(qkv, sb - 1, 0))
        s = _silu(qkv_m1 * cw[0:1] + qkv * cw[1:2] + qkv_p1 * cw[2:3])
        for h in range(GDN_HEADS):
            lo, hi = h * GDN_DK, (h + 1) * GDN_DK
            gq_ref[0, rows, lo:hi] = l2n(s[:, lo:hi]) * GDN_DK ** -0.5
            gk_ref[0, rows, lo:hi] = l2n(s[:, GDN_QK + lo:GDN_QK + hi])
        gv_ref[0, rows, :] = s[:, 2 * GDN_QK:]

        z = ab + dtb_ref[...]
        softplus = jnp.maximum(z, 0.0) + jnp.log1p(jnp.exp(-jnp.abs(z)))
        gval = -jnp.exp(al_ref[...]) * softplus
        gb = jnp.where(lane < GDN_AB // 2, gval, jnp.where(lane < GDN_AB, _sigmoid(ab), 0.0))
        gb_ref[0, rows, :] = gb
        gbt_ref[0, :, rows] = gb.T[:GDN_AB, :]

    res = {0: project(0)}
    for c in range(nsb):
        if c + 1 < nsb:
            res[c + 1] = project(c + 1)
        prev = res[c - 1][3][sb - 1:sb, :] if c > 0 else tile_prev
        nxt = res[c + 1][3][0:1, :] if c + 1 < nsb else tile_next
        finish(c, res[c], prev, nxt)
        res.pop(c - 1, None)


def _inproj_call(x, norm_g, shift, scale, cos, sin, w_bf, conv_w8, alog_row, dtb_row, tm):
    b, n, d = x.shape
    nb8 = n // 8
    step8 = tm // 8
    row = lambda bi, i: (bi, i, 0)
    vec = lambda bi, i: (bi, 0, 0)
    const = lambda bi, i: (0, 0)
    rows_out = [(DA_QK, BF16), (DA_QK, BF16), None, (GDN_QK, F32), (GDN_QK, F32), (GDN_V, F32),
                (GDN_V, F32), (LANES, F32)]
    out_specs = [pl.BlockSpec((1, tm, o[0]), row) if o else
                 pl.BlockSpec((1, DA_HEADS, tm // VT_BLK, VT_ROWS, VT_BLK), lambda bi, i: (bi, 0, i, 0, 0))
                 for o in rows_out]
    out_shape = [jax.ShapeDtypeStruct((b, n, o[0]), o[1]) if o else
                 jax.ShapeDtypeStruct((b, DA_HEADS, n // VT_BLK, VT_ROWS, VT_BLK), BF16) for o in rows_out]
    out_specs.append(pl.BlockSpec((1, GDN_AB, tm), lambda bi, i: (bi, 0, i)))
    out_shape.append(jax.ShapeDtypeStruct((b, GDN_AB, n), F32))
    return pl.pallas_call(
        _inproj_kernel,
        grid=(b, n // tm),
        in_specs=[pl.BlockSpec((1, tm, d), row),
                  pl.BlockSpec((1, 8, d), lambda bi, i: (bi, jnp.maximum(i * step8 - 1, 0), 0)),
                  pl.BlockSpec((1, 8, d), lambda bi, i: (bi, jnp.minimum((i + 1) * step8, nb8 - 1), 0)),
                  pl.BlockSpec((1, d), const),
                  pl.BlockSpec((1, 1, d), vec),
                  pl.BlockSpec((1, 1, d), vec),
                  pl.BlockSpec((tm, LANES), lambda bi, i: (i, 0)),
                  pl.BlockSpec((tm, LANES), lambda bi, i: (i, 0)),
                  pl.BlockSpec((d, IN_COLS_PAD), const, pipeline_mode=pl.Buffered(1)),
                  pl.BlockSpec((8, GDN_QKV), const),
                  pl.BlockSpec((1, LANES), const),
                  pl.BlockSpec((1, LANES), const)],
        out_specs=out_specs,
        out_shape=out_shape,
        compiler_params=_cparams(("parallel", "parallel")),
        name="inproj",
    )(x, x, x, norm_g, shift, scale, cos, sin, w_bf, conv_w8, alog_row, dtb_row)


def _attn_kernel(q_ref, k_ref, vt_ref, lam_ref, g_ref, o_ref,
                 q2_ref, acc_ref, m_ref, s_ref, p_ref, a_ref, *, tk):
    q = q_ref[0]
    tq = q.shape[0]
    lane = lax.broadcasted_iota(jnp.int32, (tq, LANES), 1)
    zero = jnp.zeros_like(q)
    q2_ref[0:tq, :] = jnp.where(lane < DA_D, q, zero)
    q2_ref[tq:2 * tq, :] = jnp.where(lane >= DA_D, q, zero)
    m_ref[...] = jnp.full(m_ref.shape, NEG_BIG, F32)
    acc_ref[...] = jnp.zeros(acc_ref.shape, F32)

    nct = 2 * tq // ATTN_COL_TILE
    col = lambda ct: slice(ct * ATTN_COL_TILE, (ct + 1) * ATTN_COL_TILE)

    def scores(kk, ct):
        return _dot_nt(kk, q2_ref[col(ct), :])

    def softmax_update(ct, s):
        cols = col(ct)
        m_old = m_ref[:, cols]
        m_new = jnp.maximum(m_old, jnp.max(s, axis=0, keepdims=True))
        alpha = jnp.exp2(m_old - m_new)
        p = jnp.exp2(s - m_new)
        m_ref[:, cols] = m_new
        return p.astype(BF16), alpha

    vblk = vt_ref.shape[4]
    nsub = tk // vblk

    def value_update(ct, j, p, alpha):
        cols = col(ct)
        pv = _dot(vt_ref[0, 0, j * nsub], p[0:vblk, :])
        for c in range(1, nsub):
            pv = pv + _dot(vt_ref[0, 0, j * nsub + c], p[c * vblk:(c + 1) * vblk, :])
        acc_ref[:, cols] = alpha * acc_ref[:, cols] + pv

    nchunk = k_ref.shape[1] // tk
    last = nct - 1
    s_ref[...] = scores(k_ref[0, 0:tk, :], 0)
    p_ref[...] = jnp.zeros(p_ref.shape, BF16)
    a_ref[...] = jnp.ones(a_ref.shape, F32)

    def body(j, carry):
        kk = k_ref[0, pl.ds(pl.multiple_of(j * tk, vblk), tk), :]
        s_cur = s_ref[...]
        pending = (last, jnp.maximum(j - 1, 0), p_ref[...], a_ref[...])
        for ct in range(nct):
            if ct < last:
                s_ahead = scores(kk, ct + 1)
            else:
                jn = jnp.minimum(j + 1, nchunk - 1)
                s_ahead = scores(k_ref[0, pl.ds(pl.multiple_of(jn * tk, vblk), tk), :], 0)
            p, alpha = softmax_update(ct, s_cur)
            value_update(*pending)
            pending = (ct, j, p, alpha)
            s_cur = s_ahead
        s_ref[...] = s_cur
        p_ref[...] = pending[2]
        a_ref[...] = pending[3]
        return carry

    lax.fori_loop(0, nchunk, body, 0)
    value_update(last, nchunk - 1, p_ref[...], a_ref[...])

    lv = lam_ref[...]
    lam = (jnp.exp(jnp.sum(lv[0:1] * lv[1:2], axis=1, keepdims=True))
           - jnp.exp(jnp.sum(lv[2:3] * lv[3:4], axis=1, keepdims=True)) + LAM_INIT)
    den = acc_ref[DA_DV:DA_DV + 1, :]
    ot = (acc_ref[0:DA_DV, 0:tq] / den[:, 0:tq]
          - lam * (acc_ref[0:DA_DV, tq:2 * tq] / den[:, tq:2 * tq]))
    o = ot.T
    ms = jnp.mean(o * o, axis=-1, keepdims=True)
    o_ref[0] = ((o * lax.rsqrt(ms + EPS) * g_ref[...]) * (1.0 - LAM_INIT)).astype(o_ref.dtype)


def _attn_call(q, k_all, vt_all, lam_vecs, subln_g, tq, tk):
    b, n, _ = q.shape
    nk = k_all.shape[1]
    _, _, nvt, _, vblk = vt_all.shape
    assert nk == nvt * vblk and tk % vblk == 0 and nk % tk == 0
    return pl.pallas_call(
        functools.partial(_attn_kernel, tk=tk),
        grid=(b, DA_HEADS, n // tq),
        in_specs=[pl.BlockSpec((1, tq, LANES), lambda bi, h, i: (bi, i, h)),
                  pl.BlockSpec((1, nk, LANES), lambda bi, h, i: (bi, 0, h)),
                  pl.BlockSpec((1, 1, nvt, VT_ROWS, vblk), lambda bi, h, i: (bi, h, 0, 0, 0)),
                  pl.BlockSpec((4, DA_D), lambda bi, h, i: (0, 0)),
                  pl.BlockSpec((1, DA_DV), lambda bi, h, i: (0, 0))],
        out_specs=pl.BlockSpec((1, tq, LANES), lambda bi, h, i: (bi, i, h)),
        out_shape=jax.ShapeDtypeStruct((b, n, DA_V), BF16),
        scratch_shapes=[pltpu.VMEM((2 * tq, LANES), BF16), pltpu.VMEM((VT_ROWS, 2 * tq), F32),
                        pltpu.VMEM((1, 2 * tq), F32),
                        pltpu.VMEM((tk, ATTN_COL_TILE), F32), pltpu.VMEM((tk, ATTN_COL_TILE), BF16),
                        pltpu.VMEM((1, ATTN_COL_TILE), F32)],
        compiler_params=_cparams(("parallel", "parallel", "parallel")),
        name="attn",
    )(q, k_all, vt_all, lam_vecs, subln_g)


def _solve_unit_triangular(a_list, rhs_list, bd_mask):
    def mm(xs, ys):
        return [_dot(x.astype(BF16), y.astype(BF16)) for x, y in zip(xs, ys)]

    d = [jnp.where(bd_mask, a, 0.0) for a in a_list]
    n = [a - di for a, di in zip(a_list, d)]
    d2 = mm(d, d)
    dd2 = mm(d, d2)
    p = [(-di + d2i) - t for di, d2i, t in zip(d, d2, dd2)]
    d4 = mm(d2, d2)
    pd4 = mm(p, d4)
    p = [pi + d4i + t for pi, d4i, t in zip(p, d4, pd4)]
    d8 = mm(d4, d4)
    pd8 = mm(p, d8)
    p = [pi + d8i + t for pi, d8i, t in zip(p, d8, pd8)]
    m = [ni + t for ni, t in zip(n, mm(p, n))]
    z = [ri + t for ri, t in zip(rhs_list, mm(p, rhs_list))]
    m2 = mm(m, m)
    z = [zi + t for zi, t in zip(z, mm(m2, z))]
    return [zi - t for zi, t in zip(z, mm(m, z))]


def _gdn_scan_kernel(qf_ref, kf_ref, vf_ref, gbf_ref, gbtf_ref, qb_ref, kb_ref, vb_ref, gbb_ref, gbtb_ref,
                     s0_ref, of_ref, ob_ref, sfin_ref, s_ref):
    i = pl.program_id(1)

    @pl.when(i == 0)
    def _():
        s_ref[...] = s0_ref[0]

    c64 = CHUNK
    nchunk = qf_ref.shape[1] // c64
    ri = lax.broadcasted_iota(jnp.int32, (c64, c64), 0)
    ci = lax.broadcasted_iota(jnp.int32, (c64, c64), 1)
    bd_mask = (ri // 16) == (ci // 16)
    lower_incl, upper_incl = ri >= ci, ri <= ci
    heads = range(GDN_HEADS)
    rows = [slice(c * c64, (c + 1) * c64) for c in range(nchunk)]
    cols = [slice(h * GDN_DK, (h + 1) * GDN_DK) for h in heads]

    dirs = []
    for d, refs in enumerate(((qf_ref, kf_ref, vf_ref, gbf_ref, gbtf_ref, of_ref),
                              (qb_ref, kb_ref, vb_ref, gbb_ref, gbtb_ref, ob_ref))):
        rev = d == 1
        incl = upper_incl if rev else lower_incl
        dirs.append(dict(
            refs=refs, incl=incl, strict=(ri < ci) if rev else (ri > ci),
            tri_c=jnp.where(incl, 1.0, 0.0).astype(BF16),
            tri_r=jnp.where(lower_incl if rev else upper_incl, 1.0, 0.0).astype(BF16),
            last=0 if rev else c64 - 1,
            order=list(range(nchunk - 1, -1, -1) if rev else range(nchunk))))

    gall, gc_col, gc_row = {}, {}, {}
    for d, dr in enumerate(dirs):
        gb_ref, gbt_ref = dr["refs"][3], dr["refs"][4]
        for c in range(nchunk):
            gall[d, c] = gb_ref[0, rows[c], :]
            g1, g2, g3 = _split3(gall[d, c])
            tc, tr = dr["tri_c"], dr["tri_r"]
            gc_col[d, c] = _dot(tc, g1) + _dot(tc, g2) + _dot(tc, g3)
            t1, t2, t3 = _split3(gbt_ref[0, :, rows[c]])
            gc_row[d, c] = _dot(t1, tr) + _dot(t2, tr) + _dot(t3, tr)

    probs = [(d, c, h) for d in range(2) for c in range(nchunk) for h in heads]
    ln = lambda d, h: d * GDN_HEADS + h
    k = {(d, c, h): dirs[d]["refs"][1][0, rows[c], cols[h]] for d, c, h in probs}
    q = {(d, c, h): dirs[d]["refs"][0][0, rows[c], cols[h]] for d, c, h in probs}
    kbf = {p: k[p].astype(BF16) for p in probs}
    kk = {p: _dot_nt(kbf[p], kbf[p]) for p in probs}
    qk = {p: _dot_nt(q[p].astype(BF16), kbf[p]) for p in probs}
    a_list, rhs_list = [], []
    kd, qd, aqk, glast = {}, {}, {}, {}
    for p in probs:
        d, c, h = p
        dr = dirs[d]
        lane = ln(d, h)
        gcol = gc_col[d, c][:, lane:lane + 1]
        bcol = gall[d, c][:, GDN_AB // 2 + lane:GDN_AB // 2 + lane + 1]
        diff = gcol - gc_row[d, c][lane:lane + 1, :]
        e_strict = jnp.where(dr["strict"], jnp.exp(jnp.where(dr["strict"], diff, 0.0)), 0.0)
        e_incl = jnp.where(dr["incl"], jnp.exp(jnp.where(dr["incl"], diff, 0.0)), 0.0)
        eg = jnp.exp(gcol)
        glast[p] = gcol[dr["last"]:dr["last"] + 1, :]
        a_list.append(bcol * kk[p] * e_strict)
        v = dr["refs"][2][0, rows[c], cols[h]]
        rhs_list.append(jnp.concatenate([(bcol * eg) * k[p], bcol * v], axis=1))
        kd[p] = (k[p] * jnp.exp(glast[p] - gcol)).astype(BF16)
        qd[p] = (q[p] * eg).astype(BF16)
        aqk[p] = (qk[p] * e_incl).astype(BF16)
    sol = dict(zip(probs, _solve_unit_triangular(a_list, rhs_list, bd_mask)))

    chains = [(d, h) for d in range(2) for h in heads]
    for t in range(nchunk):
        cur = {(d, h): (d, dirs[d]["order"][t], h) for d, h in chains}
        s = {dh: s_ref[dh[0], dh[1]] for dh in chains}
        sb = {dh: s[dh].astype(BF16) for dh in chains}
        ws = {dh: _dot(sol[cur[dh]][:, :GDN_DK].astype(BF16), sb[dh]) for dh in chains}
        qs = {dh: _dot(qd[cur[dh]], sb[dh]) for dh in chains}
        u = {dh: (sol[cur[dh]][:, GDN_DK:] - ws[dh]).astype(BF16) for dh in chains}
        au = {dh: _dot(aqk[cur[dh]], u[dh]) for dh in chains}
        ku = {dh: _dot_tn(kd[cur[dh]], u[dh]) for dh in chains}
        for dh in chains:
            d, c, h = cur[dh]
            s_ref[d, h] = jnp.exp(glast[cur[dh]]) * s[dh] + ku[dh]
            dirs[d]["refs"][5][0, rows[c], cols[h]] = qs[dh] + au[dh]

    @pl.when(i == pl.num_programs(1) - 1)
    def _():
        sfin_ref[0] = s_ref[...]


def _gdn_scan_call(q, k, v, gb, gbt, s0, sc):
    b, n, _ = q.shape
    nsup = n // sc
    fwd = lambda bi, i: (bi, i, 0)
    bwd = lambda bi, i: (bi, nsup - 1 - i, 0)
    st = lambda bi, i: (bi, 0, 0, 0, 0)
    state = (1, 2, GDN_HEADS, GDN_DK, GDN_DV)

    def in_specs(row, tr):
        return [pl.BlockSpec((1, sc, GDN_QK), row), pl.BlockSpec((1, sc, GDN_QK), row),
                pl.BlockSpec((1, sc, GDN_V), row), pl.BlockSpec((1, sc, LANES), row),
                pl.BlockSpec((1, GDN_AB, sc), tr)]

    return pl.pallas_call(
        _gdn_scan_kernel,
        grid=(b, nsup),
        in_specs=(in_specs(fwd, lambda bi, i: (bi, 0, i))
                  + in_specs(bwd, lambda bi, i: (bi, 0, nsup - 1 - i))
                  + [pl.BlockSpec(state, st)]),
        out_specs=[pl.BlockSpec((1, sc, GDN_V), fwd), pl.BlockSpec((1, sc, GDN_V), bwd),
                   pl.BlockSpec(state, st)],
        out_shape=[jax.ShapeDtypeStruct((b, n, GDN_V), F32), jax.ShapeDtypeStruct((b, n, GDN_V), F32),
                   jax.ShapeDtypeStruct((b,) + state[1:], F32)],
        scratch_shapes=[pltpu.VMEM(state[1:], F32)],
        compiler_params=_cparams(("parallel", "arbitrary")),
        name="gdn_scan",
    )(q, k, v, gb, gbt, q, k, v, gb, gbt, s0)


def _outproj_kernel(oda_ref, of_ref, ob_ref, gate_ref, x_ref, gt1_ref, gng_ref, wout_ref,
                    n2g_ref, sh2_ref, sc2_ref, wr_ref, xnew_ref, ht_ref, aff_ref, afft_ref):
    tb = ht_ref.shape[3]
    w1, w2, _ = _split3(wr_ref[...])

    def mix_and_project(c):
        rows = slice(c * tb, (c + 1) * tb)
        og = of_ref[0, rows, :] + ob_ref[0, rows, :]
        gate = gate_ref[0, rows, :]
        ys = []
        for h in range(GDN_HEADS):
            cols = slice(h * GDN_DV, (h + 1) * GDN_DV)
            t = og[:, cols]
            y = t * lax.rsqrt(jnp.mean(t * t, axis=-1, keepdims=True) + EPS) * gng_ref[...]
            ys.append((y * _silu(gate[:, cols])).astype(BF16))
        y_gdn = jnp.concatenate(ys, axis=1)
        proj = _dot(oda_ref[0, rows, :], wout_ref[0:DA_V, :]) + _dot(y_gdn, wout_ref[DA_V:, :])
        return x_ref[0, rows, :] + gt1_ref[0] * proj

    def norm_and_route(c, xn):
        rows = slice(c * tb, (c + 1) * tb)
        xnew_ref[0, rows, :] = xn
        ms = jnp.mean(xn * xn, axis=-1, keepdims=True)
        hm = (xn * lax.rsqrt(ms + EPS) * n2g_ref[...]) * (1.0 + sc2_ref[0]) + sh2_ref[0]
        ht_ref[0, c] = hm.T.astype(BF16)
        h1, h2, _ = _split3(hm)
        logits = _dot(h1, w1) + (_dot(h1, w2) + _dot(h2, w1))
        lane = lax.broadcasted_iota(jnp.int32, logits.shape, 1)
        logits = jnp.where(lane < N_EXPERTS, logits, NEG_BIG)
        e = jnp.exp(logits - jnp.max(logits, axis=-1, keepdims=True))
        aff = e / jnp.sum(e, axis=-1, keepdims=True)
        aff_ref[0, rows, :] = aff
        afft_ref[0, :, rows] = aff.T[:N_EXPERTS, :]

    nblk = ht_ref.shape[1]
    xn = mix_and_project(0)
    for c in range(nblk):
        xn_next = mix_and_project(c + 1) if c + 1 < nblk else None
        norm_and_route(c, xn)
        xn = xn_next


def _outproj_call(o_da, o_f, o_b, gate, x, gt1, gdn_norm_g, w_out_bf, norm2_g, sh2, sc2, wr_pad, tm, tb):
    b, n, d = x.shape
    assert tm % tb == 0
    row = lambda bi, i: (bi, i, 0)
    vec = lambda bi, i: (bi, 0, 0)
    const2 = lambda bi, i: (0, 0)
    return pl.pallas_call(
        _outproj_kernel,
        grid=(b, n // tm),
        in_specs=[pl.BlockSpec((1, tm, DA_V), row), pl.BlockSpec((1, tm, GDN_V), row),
                  pl.BlockSpec((1, tm, GDN_V), row), pl.BlockSpec((1, tm, GDN_V), row),
                  pl.BlockSpec((1, tm, d), row), pl.BlockSpec((1, 1, d), vec),
                  pl.BlockSpec((1, GDN_DV), const2), pl.BlockSpec((DA_V + GDN_V, d), const2),
                  pl.BlockSpec((1, d), const2), pl.BlockSpec((1, 1, d), vec), pl.BlockSpec((1, 1, d), vec),
                  pl.BlockSpec((d, LANES), const2)],
        out_specs=[pl.BlockSpec((1, tm, d), row),
                   pl.BlockSpec((1, tm // tb, d, tb), lambda bi, i: (bi, i, 0, 0)),
                   pl.BlockSpec((1, tm, LANES), row),
                   pl.BlockSpec((1, N_EXPERTS, tm), lambda bi, i: (bi, 0, i))],
        out_shape=[jax.ShapeDtypeStruct((b, n, d), F32), jax.ShapeDtypeStruct((b, n // tb, d, tb), BF16),
                   jax.ShapeDtypeStruct((b, n, LANES), F32),
                   jax.ShapeDtypeStruct((b, N_EXPERTS, n), F32)],
        compiler_params=_cparams(("parallel", "parallel")),
        name="outproj",
    )(o_da, o_f, o_b, gate, x, gt1, gdn_norm_g, w_out_bf, norm2_g, sh2, sc2, wr_pad)


GATHER_GROUP = 8
COMBINE_ALIGN = 16
PREFIX_BLK = 256
BISECT_STEPS = 40


def _select_kernel(afft_ref, aff_ref, post_ref, pos_ref, st_ref, rng_ref, *, cap, tb, tj):
    at = afft_ref[0]
    n = at.shape[1]

    def count_ge(t):
        return jnp.sum(jnp.where(at >= t, 1.0, 0.0), axis=1, keepdims=True)

    def bisect(_, bounds):
        lo, hi = bounds
        mid = 0.5 * (lo + hi)
        ok = count_ge(mid) >= cap
        return jnp.where(ok, mid, lo), jnp.where(ok, hi, mid)

    _, hi = lax.fori_loop(0, BISECT_STEPS, bisect,
                          (jnp.zeros((N_EXPERTS, 1), F32), jnp.full((N_EXPERTS, 1), 2.0, F32)))

    def below(h):
        return jnp.max(jnp.where(at < h, at, -1.0), axis=1, keepdims=True)

    def short(state):
        return jnp.sum(jnp.where(count_ge(state[0]) < cap, 1.0, 0.0)) > 0.0

    def step_down(state):
        t, h = state
        lacking = count_ge(t) < cap
        h = jnp.where(lacking, t, h)
        return jnp.where(lacking, below(h), t), h

    thr, _ = lax.while_loop(short, step_down, (below(hi), hi))
    need = cap - jnp.sum(jnp.where(at > thr, 1.0, 0.0), axis=1, keepdims=True)

    pi = lax.broadcasted_iota(jnp.int32, (PREFIX_BLK, PREFIX_BLK), 0)
    pj = lax.broadcasted_iota(jnp.int32, (PREFIX_BLK, PREFIX_BLK), 1)
    upper = jnp.where(pi <= pj, 1.0, 0.0).astype(BF16)
    lower = jnp.where(pi >= pj, 1.0, 0.0).astype(BF16)

    def prefix_lanes(m):
        carry = jnp.zeros((m.shape[0], 1), F32)
        outs = []
        for c in range(m.shape[1] // PREFIX_BLK):
            mc = m[:, c * PREFIX_BLK:(c + 1) * PREFIX_BLK]
            loc = _dot(mc.astype(BF16), upper)
            outs.append(loc - mc + carry)
            carry = carry + loc[:, PREFIX_BLK - 1:PREFIX_BLK]
        return jnp.concatenate(outs, axis=1)

    def prefix_rows(m):
        carry = jnp.zeros((1, m.shape[1]), F32)
        outs = []
        for c in range(m.shape[0] // PREFIX_BLK):
            mc = m[c * PREFIX_BLK:(c + 1) * PREFIX_BLK, :]
            loc = _dot(lower, mc.astype(BF16))
            outs.append(loc - mc + carry)
            carry = carry + loc[PREFIX_BLK - 1:PREFIX_BLK, :]
        return jnp.concatenate(outs, axis=0)

    eq_t = jnp.where(at == thr, 1.0, 0.0)
    sel_t = jnp.where(at > thr, 1.0, jnp.where(prefix_lanes(eq_t) < need, eq_t, 0.0))
    post_ref[0] = jnp.where(sel_t > 0.0, prefix_lanes(sel_t), -1.0)

    ti = lax.broadcasted_iota(jnp.int32, (n, LANES), 0)
    bi = lax.broadcasted_iota(jnp.int32, (n, LANES), 1)
    before = jnp.where(ti < bi * tb, 1.0, 0.0).astype(BF16)
    st = _dot(sel_t.astype(BF16), before)
    st_ref[0] = st.astype(jnp.int32)

    through = jnp.where(ti < (bi + 1) * tb, 1.0, 0.0).astype(BF16)
    st_end = _dot(sel_t.astype(BF16), through)
    blk_lane = lax.broadcasted_iota(jnp.int32, (N_EXPERTS, LANES), 1)
    is_blk = blk_lane < n // tb
    rng = jnp.zeros((N_EXPERTS, LANES), F32)
    ntile = cap // tj
    for j in range(ntile):
        first = jnp.sum(jnp.where(is_blk & (st_end <= j * tj), 1.0, 0.0), axis=1, keepdims=True)
        last = jnp.sum(jnp.where(is_blk & (st < (j + 1) * tj), 1.0, 0.0), axis=1, keepdims=True) - 1.0
        rng = rng + jnp.where(blk_lane == j, first, 0.0) + jnp.where(blk_lane == ntile + j, last, 0.0)
    rng_ref[0] = rng.astype(jnp.int32)

    a = aff_ref[0]
    er = lax.broadcasted_iota(jnp.int32, (N_EXPERTS, LANES), 0)
    ec = lax.broadcasted_iota(jnp.int32, (N_EXPERTS, LANES), 1)
    diag = er == ec
    thr_row = jnp.sum(jnp.where(diag, thr, 0.0), axis=0, keepdims=True)
    need_row = jnp.sum(jnp.where(diag, need, 0.0), axis=0, keepdims=True)
    valid = lax.broadcasted_iota(jnp.int32, a.shape, 1) < N_EXPERTS
    eq = jnp.where(valid & (a == thr_row), 1.0, 0.0)
    sel = jnp.where(valid & (a > thr_row), 1.0, jnp.where(prefix_rows(eq) < need_row, eq, 0.0))
    pos_ref[0] = jnp.where(sel > 0.0, prefix_rows(sel), -1.0)


def _select_call(afft, aff, cap, tb, tj):
    b, e, n = afft.shape
    assert n // tb + 1 <= LANES and 2 * (cap // tj) <= LANES
    return pl.pallas_call(
        functools.partial(_select_kernel, cap=cap, tb=tb, tj=tj),
        grid=(b,),
        in_specs=[pl.BlockSpec((1, e, n), lambda bi: (bi, 0, 0)),
                  pl.BlockSpec((1, n, LANES), lambda bi: (bi, 0, 0))],
        out_specs=[pl.BlockSpec((1, e, n), lambda bi: (bi, 0, 0)),
                   pl.BlockSpec((1, n, LANES), lambda bi: (bi, 0, 0)),
                   pl.BlockSpec((1, e, LANES), lambda bi: (bi, 0, 0)),
                   pl.BlockSpec((1, e, LANES), lambda bi: (bi, 0, 0))],
        out_shape=[jax.ShapeDtypeStruct((b, e, n), F32), jax.ShapeDtypeStruct((b, n, LANES), F32),
                   jax.ShapeDtypeStruct((b, e, LANES), jnp.int32),
                   jax.ShapeDtypeStruct((b, e, LANES), jnp.int32)],
        compiler_params=_cparams(("parallel",)),
        name="select",
    )(afft, aff)


def _moe_ffn_kernel(rng_ref, ht_ref, post_ref, afft_ref, wg_ref, wu_ref, wd_ref, ye_ref,
                    acc_ref, gacc_ref, wgb_ref, wub_ref, wdb_ref, *, tj, tb):
    b = pl.program_id(0)
    e = pl.program_id(1)
    cap = ye_ref.shape[2]
    ntile = cap // tj
    slot = lax.broadcasted_iota(jnp.int32, (tj, tb), 0).astype(F32)

    for j in range(cap // tj):
        lo = j * tj
        acc_ref[...] = jnp.zeros(acc_ref.shape, F32)
        gacc_ref[...] = jnp.zeros(gacc_ref.shape, F32)

        first = rng_ref[b, e, j]
        last = rng_ref[b, e, ntile + j]

        def gather_blocks(start, count):
            part = jnp.zeros(acc_ref.shape, F32)
            gate = jnp.zeros(gacc_ref.shape, F32)
            for u in range(count):
                bi = start + u
                prow = post_ref[0, 0, pl.ds(bi, 1), :]
                hit = prow == (slot + float(lo))
                onehot = jnp.where(hit, 1.0, 0.0).astype(BF16)
                part = part + _dot_nt(ht_ref[0, bi], onehot)
                arow = afft_ref[0, 0, pl.ds(bi, 1), :]
                gate = gate + jnp.sum(jnp.where(hit, arow, 0.0), axis=1, keepdims=True)
            acc_ref[...] += part
            gacc_ref[...] += gate

        def group_body(g, carry):
            gather_blocks(first + g * GATHER_GROUP, GATHER_GROUP)
            return carry

        nblocks = last - first + 1
        full = nblocks // GATHER_GROUP
        lax.fori_loop(0, full, group_body, 0)
        done = first + full * GATHER_GROUP
        size = GATHER_GROUP // 2
        while size >= 1:
            @pl.when((nblocks & size) != 0)
            def _(done=done, size=size):
                gather_blocks(done, size)
            done = done + (nblocks & size)
            size //= 2
        xe = acc_ref[...].T.astype(BF16)
        if j == 0:
            wgb_ref[...] = wg_ref[0].astype(BF16)
            wub_ref[...] = wu_ref[0].astype(BF16)
            wdb_ref[...] = wd_ref[0].astype(BF16)
        hid = _silu(_dot(xe, wgb_ref[...])) * _dot(xe, wub_ref[...])
        ye = _dot(hid.astype(BF16), wdb_ref[...]) * gacc_ref[...]
        ye_ref[0, 0, lo:lo + tj, :] = ye.astype(ye_ref.dtype)


def _moe_ffn_call(tile_rng, ht4, post4, afft4, wg, wu, wd, cap, tj):
    b, nblk, d, tb = ht4.shape
    e, _, f = wg.shape
    wspec = lambda shp: pl.BlockSpec((1,) + shp, lambda bi, ei: (ei, 0, 0))
    return pl.pallas_call(
        functools.partial(_moe_ffn_kernel, tj=tj, tb=tb),
        grid=(b, e),
        in_specs=[pl.BlockSpec(memory_space=pltpu.SMEM),
                  pl.BlockSpec((1, nblk, d, tb), lambda bi, ei: (bi, 0, 0, 0), pipeline_mode=pl.Buffered(1)),
                  pl.BlockSpec((1, 1, nblk, tb), lambda bi, ei: (bi, ei, 0, 0)),
                  pl.BlockSpec((1, 1, nblk, tb), lambda bi, ei: (bi, ei, 0, 0)),
                  wspec((d, f)), wspec((d, f)), wspec((f, d))],
        out_specs=pl.BlockSpec((1, 1, cap, d), lambda bi, ei: (bi, ei, 0, 0)),
        out_shape=jax.ShapeDtypeStruct((b, e, cap, d), BF16),
        scratch_shapes=[pltpu.VMEM((d, tj), F32), pltpu.VMEM((tj, 1), F32),
                        pltpu.VMEM((d, f), BF16), pltpu.VMEM((d, f), BF16), pltpu.VMEM((f, d), BF16)],
        compiler_params=_cparams(("parallel", "arbitrary")),
        name="moe_ffn",
    )(tile_rng, ht4, post4, afft4, wg, wu, wd)


def _combine_kernel(st_ref, pos_ref, ye_ref, x_ref, gt2_ref, g_ref, o_ref, rest_ref, *, tb):
    b = pl.program_id(0)
    blk = pl.program_id(1)
    cap = ye_ref.shape[2]
    win = tb
    slot = lax.broadcasted_iota(jnp.int32, (tb, win), 1).astype(F32)
    pos = pos_ref[0]

    def window_base(e):
        return jnp.minimum(st_ref[b, e, blk] // COMBINE_ALIGN, (cap - win) // COMBINE_ALIGN) * COMBINE_ALIGN

    def window(e, start):
        return ye_ref[0, e, pl.ds(pl.multiple_of(start, COMBINE_ALIGN), win), :]

    moe = jnp.zeros((tb, ye_ref.shape[3]), F32)
    for e in range(N_EXPERTS):
        base = window_base(e)
        onehot = jnp.where(pos[:, e:e + 1] == slot + base.astype(F32), 1.0, 0.0).astype(BF16)
        moe = moe + _dot(onehot, window(e, base))

    rest_ref[...] = jnp.zeros(rest_ref.shape, F32)
    for e in range(N_EXPERTS):
        done = window_base(e) + win

        @pl.when(st_ref[b, e, blk + 1] > done)
        def _():
            start = jnp.minimum(done, cap - win)
            pcol = pos[:, e:e + 1]
            hit = (pcol == slot + start.astype(F32)) & (pcol >= done.astype(F32))
            rest_ref[...] += _dot(jnp.where(hit, 1.0, 0.0).astype(BF16), window(e, start))

    y = x_ref[0] + gt2_ref[0] * (moe + rest_ref[...])
    ms = jnp.mean(y * y, axis=-1, keepdims=True)
    o_ref[0] = y * lax.rsqrt(ms + EPS) * g_ref[...]


def _combine_call(starts, pos, ye, x_new, gt2, final_g, tb):
    b, e, cap, d = ye.shape
    n = pos.shape[1]
    assert cap % COMBINE_ALIGN == 0 and tb % COMBINE_ALIGN == 0 and cap >= tb
    row = lambda bi, i: (bi, i, 0)
    return pl.pallas_call(
        functools.partial(_combine_kernel, tb=tb),
        grid=(b, n // tb),
        in_specs=[pl.BlockSpec(memory_space=pltpu.SMEM),
                  pl.BlockSpec((1, tb, LANES), row),
                  pl.BlockSpec((1, e, cap, d), lambda bi, i: (bi, 0, 0, 0), pipeline_mode=pl.Buffered(1)),
                  pl.BlockSpec((1, tb, d), row),
                  pl.BlockSpec((1, 1, d), lambda bi, i: (bi, 0, 0)),
                  pl.BlockSpec((1, d), lambda bi, i: (0, 0))],
        out_specs=pl.BlockSpec((1, tb, d), row),
        out_shape=jax.ShapeDtypeStruct((b, n, d), F32),
        scratch_shapes=[pltpu.VMEM((tb, d), F32)],
        compiler_params=_cparams(("parallel", "arbitrary")),
        name="combine",
    )(starts, pos, ye, x_new, gt2, final_g)


def _rope_tables(n):
    t = np.arange(n)
    rows = (t // GRID_W).astype(np.float64)
    cols = (t % GRID_W).astype(np.float64)
    inv_freq = np.power(ROPE_THETA, -np.arange(0, ROPE_AXIS_DIM, 2, dtype=np.float64) / ROPE_AXIS_DIM)
    ang_row = rows[:, None] * inv_freq[None, :]
    ang_col = cols[:, None] * inv_freq[None, :]

    def axis_tables(ang):
        c = np.cos(ang).astype(np.float32)
        s = np.sin(ang).astype(np.float32)
        return np.concatenate([c, c], axis=1), np.concatenate([-s, s], axis=1)

    cr, sr = axis_tables(ang_row)
    cc, sc = axis_tables(ang_col)
    cos64 = np.concatenate([cr, cc], axis=1)
    sin64 = np.concatenate([sr, sc], axis=1)
    return (jnp.asarray(np.concatenate([cos64, cos64], axis=1)),
            jnp.asarray(np.concatenate([sin64, sin64], axis=1)))


def _attn_chunk(nk):
    return max(t for t in range(VT_BLK, 1024 + 1, VT_BLK) if nk % t == 0)


def _pad_lanes(v):
    return jnp.pad(v.reshape(1, -1), ((0, 0), (0, LANES - v.size)))


def kernel(x, c, ctx, c_ctx, w_mod, b_mod, norm1_g, w_in, conv_w, a_log, dt_bias, gdn_norm_g,
           lam_q1, lam_k1, lam_q2, lam_k2, da_subln_g, w_out, norm2_g,
           w_router, w_gate, w_up, w_down, final_g):
    b, n, d = x.shape
    nc = ctx.shape[1]
    layer = 0

    cvec = jnp.concatenate([c, c_ctx[None, :], jnp.zeros((8 - b - 1, d), F32)], axis=0)
    mod = _mod_call(cvec, w_mod[layer], b_mod[layer])
    sh1, sc1, gt1, sh2, sc2, gt2 = [mod[:b, i * d:(i + 1) * d].reshape(b, 1, d) for i in range(6)]
    sh1c = jnp.broadcast_to(mod[b:b + 1, 0:d].reshape(1, 1, d), (b, 1, d))
    sc1c = jnp.broadcast_to(mod[b:b + 1, d:2 * d].reshape(1, 1, d), (b, 1, d))

    w_in_bf = jnp.pad(w_in[layer].astype(BF16), ((0, 0), (0, IN_COLS_PAD - w_in.shape[2])))
    g1 = norm1_g[layer].reshape(1, d)
    cos_l, sin_l = _rope_tables(n)
    cos_c, sin_c = jnp.ones((nc, LANES), F32), jnp.zeros((nc, LANES), F32)
    conv_w8 = jnp.pad(conv_w[layer], ((0, 8 - conv_w.shape[1]), (0, 0)))
    alog_row = _pad_lanes(a_log[layer])
    dtb_row = _pad_lanes(dt_bias[layer])
    gdn_args = (conv_w8, alog_row, dtb_row)
    q, k, vt, ql, kl, vl, gate, gbl, gbtl = _inproj_call(
        x, g1, sh1, sc1, cos_l, sin_l, w_in_bf, *gdn_args, tm=1024)
    _, kc, vct, qc, kcg, vcg, _, gbc, gbtc = _inproj_call(
        ctx, g1, sh1c, sc1c, cos_c, sin_c, w_in_bf, *gdn_args, tm=nc)

    lam_vecs = jnp.stack([lam_q1[layer], lam_k1[layer], lam_q2[layer], lam_k2[layer]], axis=0)
    k_all = jnp.concatenate([kc, k], axis=1)
    vt_all = jnp.concatenate([vct, vt], axis=2)
    o_da = _attn_call(q, k_all, vt_all, lam_vecs, da_subln_g[layer].reshape(1, DA_DV),
                      tq=4096, tk=_attn_chunk(nc + n))

    zeros_state = jnp.zeros((b, 2, GDN_HEADS, GDN_DK, GDN_DV), F32)
    _, _, s_ctx = _gdn_scan_call(qc, kcg, vcg, gbc, gbtc, zeros_state, sc=nc)
    o_fwd, o_bwd, _ = _gdn_scan_call(ql, kl, vl, gbl, gbtl, s_ctx, sc=256)
    o_dirs = [o_fwd, o_bwd]

    wr_pad = jnp.pad(w_router[layer], ((0, 0), (0, LANES - N_EXPERTS)))
    cap = CAP_FACTOR * n // N_EXPERTS
    tb = 256
    tj = 256
    x_new, ht4, aff, afft = _outproj_call(
        o_da, o_dirs[0], o_dirs[1], gate, x, gt1, gdn_norm_g[layer].reshape(1, GDN_DV),
        w_out[layer].astype(BF16), norm2_g[layer].reshape(1, d), sh2, sc2, wr_pad, tm=1024, tb=tb)

    post, pos, starts, tile_rng = _select_call(afft, aff, cap, tb, tj)
    nblk = n // tb
    ye = _moe_ffn_call(tile_rng, ht4, post.reshape(b, N_EXPERTS, nblk, tb),
                       afft.reshape(b, N_EXPERTS, nblk, tb),
                       w_gate[layer], w_up[layer], w_down[layer], cap, tj)
    return _combine_call(starts, pos, ye, x_new, gt2, final_g.reshape(1, d), tb)
```

```python
import functools
import math

import jax
import jax.numpy as jnp
import numpy as np
from jax import lax
from jax.experimental import pallas as pl
from jax.experimental.pallas import tpu as pltpu

F32 = jnp.float32
BF16 = jnp.bfloat16

D_MODEL = 1024
GRID_W = 64
EPS = 1e-6
DA_HEADS = 4
DA_D = 64
DA_DV = 2 * DA_D
ROPE_AXIS_DIM = DA_D // 2
ROPE_THETA = 10000.0
GDN_HEADS = 4
GDN_DK = 128
GDN_DV = 128
CHUNK = 64
N_EXPERTS = 16
CAP_FACTOR = 2
LAM_INIT = 0.8 - 0.6 * math.exp(-0.3 * 0)

DA_QK = DA_HEADS * 2 * DA_D
DA_V = DA_HEADS * DA_DV
GDN_QK = GDN_HEADS * GDN_DK
GDN_V = GDN_HEADS * GDN_DV
GDN_QKV = 2 * GDN_QK + GDN_V
GDN_AB = 2 * 2 * GDN_HEADS
COL_Q, COL_K, COL_V = 0, DA_QK, 2 * DA_QK
COL_QKV = 2 * DA_QK + DA_V
COL_GATE = COL_QKV + GDN_QKV
COL_AB = COL_GATE + GDN_V
LANES = 128
IN_COLS_PAD = COL_AB + LANES
MXU_WIDTH = 256
ATTN_COL_TILE = MXU_WIDTH
VT_ROWS = DA_DV + 8
VT_BLK = MXU_WIDTH

VMEM_LIMIT = 56 * 1024 * 1024
NEG_BIG = -1e30
LOG2_E = math.log2(math.e)


def _cparams(sem):
    return pltpu.CompilerParams(dimension_semantics=sem, vmem_limit_bytes=VMEM_LIMIT)


def _sigmoid(x):
    return 1.0 / (1.0 + jnp.exp(-x))


def _silu(x):
    return x * _sigmoid(x)


def _split3(a):
    a1 = a.astype(BF16)
    r1 = a - a1.astype(F32)
    a2 = r1.astype(BF16)
    a3 = (r1 - a2.astype(F32)).astype(BF16)
    return a1, a2, a3


def _dot(a, b):
    return jnp.dot(a, b, preferred_element_type=F32)


def _dot_x3(a, b):
    a1, a2, _ = _split3(a)
    b1, b2, _ = _split3(b)
    return _dot(a1, b1) + (_dot(a1, b2) + _dot(a2, b1))


def _dot_nt(a, b):
    return lax.dot_general(a, b, (((1,), (1,)), ((), ())), preferred_element_type=F32)


def _dot_tn(a, b):
    return lax.dot_general(a, b, (((0,), (0,)), ((), ())), preferred_element_type=F32)


def _mod_kernel(c_ref, w_ref, b_ref, o_ref):
    s = _silu(c_ref[...])
    o_ref[...] = jnp.dot(s, w_ref[...], precision=lax.Precision.HIGHEST,
                         preferred_element_type=F32) + b_ref[...]


def _mod_call(cvec, w_mod, b_mod):
    d, n = w_mod.shape
    tn = 1024
    return pl.pallas_call(
        _mod_kernel,
        grid=(n // tn,),
        in_specs=[pl.BlockSpec((8, d), lambda j: (0, 0)),
                  pl.BlockSpec((d, tn), lambda j: (0, j)),
                  pl.BlockSpec((1, tn), lambda j: (0, j))],
        out_specs=pl.BlockSpec((8, tn), lambda j: (0, j)),
        out_shape=jax.ShapeDtypeStruct((8, n), F32),
        compiler_params=_cparams(("arbitrary",)),
        name="mod",
    )(cvec, w_mod, b_mod.reshape(1, n))


def _inproj_kernel(x_ref, xp_ref, xn_ref, g_ref, sh_ref, sc_ref, cos_ref, sin_ref, w_ref,
                   cw_ref, al_ref, dtb_ref,
                   q_ref, k_ref, vt_ref, gq_ref, gk_ref, gv_ref, gate_ref, gb_ref, gbt_ref):
    i = pl.program_id(1)
    nblk = pl.num_programs(1)
    tm = x_ref.shape[1]

    def modulated(xb):
        ms = jnp.mean(xb * xb, axis=-1, keepdims=True)
        hm = (xb * lax.rsqrt(ms + EPS) * g_ref[...]) * (1.0 + sc_ref[0]) + sh_ref[0]
        return hm.astype(BF16)

    sb = VT_BLK
    nsb = tm // sb
    lane = lax.broadcasted_iota(jnp.int32, (sb, LANES), 1)
    first_half = (lane % ROPE_AXIS_DIM) < (ROPE_AXIS_DIM // 2)
    extra = lax.broadcasted_iota(jnp.int32, (DA_HEADS, VT_ROWS - DA_DV, sb), 1)
    ones_row = jnp.where(extra == 0, 1.0, 0.0).astype(BF16)
    row = lax.broadcasted_iota(jnp.int32, (sb, 1), 0)
    cw = cw_ref[...]

    halo = _dot(modulated(jnp.concatenate([xp_ref[0], xn_ref[0]], axis=0)),
                w_ref[:, COL_QKV:COL_GATE])
    tile_prev = jnp.where(i > 0, halo[7:8, :], 0.0)
    tile_next = jnp.where(i < nblk - 1, halo[8:9, :], 0.0)

    def project(c):
        hb = modulated(x_ref[0, c * sb:(c + 1) * sb, :])
        cuts = (COL_Q, COL_K, COL_V, COL_QKV, COL_GATE, COL_AB, IN_COLS_PAD)
        return [_dot(hb, w_ref[:, lo:hi]) for lo, hi in zip(cuts[:-1], cuts[1:])]

    def rope(t, cos, sin):
        outs = []
        for j in range(t.shape[1] // LANES):
            s = t[:, LANES * j:LANES * (j + 1)]
            partner = jnp.where(first_half,
                                pltpu.roll(s, LANES - ROPE_AXIS_DIM // 2, 1),
                                pltpu.roll(s, ROPE_AXIS_DIM // 2, 1))
            outs.append(s * cos + partner * sin)
        return jnp.concatenate(outs, axis=1)

    def l2n(t):
        return t * lax.rsqrt(jnp.sum(t * t, axis=-1, keepdims=True) + EPS)

    def finish(c, cur, prev, nxt):
        rows = slice(c * sb, (c + 1) * sb)
        q, k, v, qkv, gate, ab = cur
        cos = cos_ref[rows, :]
        sin = sin_ref[rows, :]
        q_ref[0, rows, :] = (rope(q, cos, sin) * (DA_D ** -0.5 * LOG2_E)).astype(BF16)
        k_ref[0, rows, :] = rope(k, cos, sin).astype(BF16)
        vt_ref[0, :, c, 0:DA_DV, :] = v.T.astype(BF16).reshape(DA_HEADS, DA_DV, sb)
        vt_ref[0, :, c, DA_DV:VT_ROWS, :] = ones_row
        gate_ref[0, rows, :] = gate

        qkv_m1 = jnp.where(row == 0, prev, pltpu.roll(qkv, 1, 0))
        qkv_p1 = jnp.where(row == sb - 1, nxt, pltpu.roll(qkv, sb - 1, 0))
        s = _silu(qkv_m1 * cw[0:1] + qkv * cw[1:2] + qkv_p1 * cw[2:3])
        for h in range(GDN_HEADS):
            lo, hi = h * GDN_DK, (h + 1) * GDN_DK
            gq_ref[0, rows, lo:hi] = l2n(s[:, lo:hi]) * GDN_DK ** -0.5
            gk_ref[0, rows, lo:hi] = l2n(s[:, GDN_QK + lo:GDN_QK + hi])
        gv_ref[0, rows, :] = s[:, 2 * GDN_QK:]

        z = ab + dtb_ref[...]
        softplus = jnp.maximum(z, 0.0) + jnp.log1p(jnp.exp(-jnp.abs(z)))
        gval = -jnp.exp(al_ref[...]) * softplus
        gb = jnp.where(lane < GDN_AB // 2, gval, jnp.where(lane < GDN_AB, _sigmoid(ab), 0.0))
        gb_ref[0, rows, :] = gb
        gbt_ref[0, :, rows] = gb.T[:GDN_AB, :]

    res = {0: project(0)}
    for c in range(nsb):
        if c + 1 < nsb:
            res[c + 1] = project(c + 1)
        prev = res[c - 1][3][sb - 1:sb, :] if c > 0 else tile_prev
        nxt = res[c + 1][3][0:1, :] if c + 1 < nsb else tile_next
        finish(c, res[c], prev, nxt)
        res.pop(c - 1, None)


def _inproj_call(x, norm_g, shift, scale, cos, sin, w_bf, conv_w8, alog_row, dtb_row, tm):
    b, n, d = x.shape
    nb8 = n // 8
    step8 = tm // 8
    row = lambda bi, i: (bi, i, 0)
    vec = lambda bi, i: (bi, 0, 0)
    const = lambda bi, i: (0, 0)
    rows_out = [(DA_QK, BF16), (DA_QK, BF16), None, (GDN_QK, F32), (GDN_QK, F32), (GDN_V, F32),
                (GDN_V, F32), (LANES, F32)]
    out_specs = [pl.BlockSpec((1, tm, o[0]), row) if o else
                 pl.BlockSpec((1, DA_HEADS, tm // VT_BLK, VT_ROWS, VT_BLK), lambda bi, i: (bi, 0, i, 0, 0))
                 for o in rows_out]
    out_shape = [jax.ShapeDtypeStruct((b, n, o[0]), o[1]) if o else
                 jax.ShapeDtypeStruct((b, DA_HEADS, n // VT_BLK, VT_ROWS, VT_BLK), BF16) for o in rows_out]
    out_specs.append(pl.BlockSpec((1, GDN_AB, tm), lambda bi, i: (bi, 0, i)))
    out_shape.append(jax.ShapeDtypeStruct((b, GDN_AB, n), F32))
    return pl.pallas_call(
        _inproj_kernel,
        grid=(b, n // tm),
        in_specs=[pl.BlockSpec((1, tm, d), row),
                  pl.BlockSpec((1, 8, d), lambda bi, i: (bi, jnp.maximum(i * step8 - 1, 0), 0)),
                  pl.BlockSpec((1, 8, d), lambda bi, i: (bi, jnp.minimum((i + 1) * step8, nb8 - 1), 0)),
                  pl.BlockSpec((1, d), const),
                  pl.BlockSpec((1, 1, d), vec),
                  pl.BlockSpec((1, 1, d), vec),
                  pl.BlockSpec((tm, LANES), lambda bi, i: (i, 0)),
                  pl.BlockSpec((tm, LANES), lambda bi, i: (i, 0)),
                  pl.BlockSpec((d, IN_COLS_PAD), const, pipeline_mode=pl.Buffered(1)),
                  pl.BlockSpec((8, GDN_QKV), const),
                  pl.BlockSpec((1, LANES), const),
                  pl.BlockSpec((1, LANES), const)],
        out_specs=out_specs,
        out_shape=out_shape,
        compiler_params=_cparams(("parallel", "parallel")),
        name="inproj",
    )(x, x, x, norm_g, shift, scale, cos, sin, w_bf, conv_w8, alog_row, dtb_row)


def _attn_kernel(q_ref, k_ref, vt_ref, lam_ref, g_ref, o_ref,
                 q2_ref, acc_ref, m_ref, s_ref, p_ref, a_ref, *, tk):
    q = q_ref[0]
    tq = q.shape[0]
    lane = lax.broadcasted_iota(jnp.int32, (tq, LANES), 1)
    zero = jnp.zeros_like(q)
    q2_ref[0:tq, :] = jnp.where(lane < DA_D, q, zero)
    q2_ref[tq:2 * tq, :] = jnp.where(lane >= DA_D, q, zero)
    m_ref[...] = jnp.full(m_ref.shape, NEG_BIG, F32)
    acc_ref[...] = jnp.zeros(acc_ref.shape, F32)

    nct = 2 * tq // ATTN_COL_TILE
    col = lambda ct: slice(ct * ATTN_COL_TILE, (ct + 1) * ATTN_COL_TILE)

    def scores(kk, ct):
        return _dot_nt(kk, q2_ref[col(ct), :])

    def softmax_update(ct, s):
        cols = col(ct)
        m_old = m_ref[:, cols]
        m_new = jnp.maximum(m_old, jnp.max(s, axis=0, keepdims=True))
        alpha = jnp.exp2(m_old - m_new)
        p = jnp.exp2(s - m_new)
        m_ref[:, cols] = m_new
        return p.astype(BF16), alpha

    vblk = vt_ref.shape[4]
    nsub = tk // vblk

    def value_update(ct, j, p, alpha):
        cols = col(ct)
        pv = _dot(vt_ref[0, 0, j * nsub], p[0:vblk, :])
        for c in range(1, nsub):
            pv = pv + _dot(vt_ref[0, 0, j * nsub + c], p[c * vblk:(c + 1) * vblk, :])
        acc_ref[:, cols] = alpha * acc_ref[:, cols] + pv

    nchunk = k_ref.shape[1] // tk
    last = nct - 1
    s_ref[...] = scores(k_ref[0, 0:tk, :], 0)
    p_ref[...] = jnp.zeros(p_ref.shape, BF16)
    a_ref[...] = jnp.ones(a_ref.shape, F32)

    def body(j, carry):
        kk = k_ref[0, pl.ds(pl.multiple_of(j * tk, vblk), tk), :]
        s_cur = s_ref[...]
        pending = (last, jnp.maximum(j - 1, 0), p_ref[...], a_ref[...])
        for ct in range(nct):
            if ct < last:
                s_ahead = scores(kk, ct + 1)
            else:
                jn = jnp.minimum(j + 1, nchunk - 1)
                s_ahead = scores(k_ref[0, pl.ds(pl.multiple_of(jn * tk, vblk), tk), :], 0)
            p, alpha = softmax_update(ct, s_cur)
            value_update(*pending)
            pending = (ct, j, p, alpha)
            s_cur = s_ahead
        s_ref[...] = s_cur
        p_ref[...] = pending[2]
        a_ref[...] = pending[3]
        return carry

    lax.fori_loop(0, nchunk, body, 0)
    value_update(last, nchunk - 1, p_ref[...], a_ref[...])

    lv = lam_ref[...]
    lam = (jnp.exp(jnp.sum(lv[0:1] * lv[1:2], axis=1, keepdims=True))
           - jnp.exp(jnp.sum(lv[2:3] * lv[3:4], axis=1, keepdims=True)) + LAM_INIT)
    den = acc_ref[DA_DV:DA_DV + 1, :]
    ot = (acc_ref[0:DA_DV, 0:tq] / den[:, 0:tq]
          - lam * (acc_ref[0:DA_DV, tq:2 * tq] / den[:, tq:2 * tq]))
    o = ot.T
    ms = jnp.mean(o * o, axis=-1, keepdims=True)
    o_ref[0] = ((o * lax.rsqrt(ms + EPS) * g_ref[...]) * (1.0 - LAM_INIT)).astype(o_ref.dtype)


def _attn_call(q, k_all, vt_all, lam_vecs, subln_g, tq, tk):
    b, n, _ = q.shape
    nk = k_all.shape[1]
    _, _, nvt, _, vblk = vt_all.shape
    assert nk == nvt * vblk and tk % vblk == 0 and nk % tk == 0
    return pl.pallas_call(
        functools.partial(_attn_kernel, tk=tk),
        grid=(b, DA_HEADS, n // tq),
        in_specs=[pl.BlockSpec((1, tq, LANES), lambda bi, h, i: (bi, i, h)),
                  pl.BlockSpec((1, nk, LANES), lambda bi, h, i: (bi, 0, h)),
                  pl.BlockSpec((1, 1, nvt, VT_ROWS, vblk), lambda bi, h, i: (bi, h, 0, 0, 0)),
                  pl.BlockSpec((4, DA_D), lambda bi, h, i: (0, 0)),
                  pl.BlockSpec((1, DA_DV), lambda bi, h, i: (0, 0))],
        out_specs=pl.BlockSpec((1, tq, LANES), lambda bi, h, i: (bi, i, h)),
        out_shape=jax.ShapeDtypeStruct((b, n, DA_V), BF16),
        scratch_shapes=[pltpu.VMEM((2 * tq, LANES), BF16), pltpu.VMEM((VT_ROWS, 2 * tq), F32),
                        pltpu.VMEM((1, 2 * tq), F32),
                        pltpu.VMEM((tk, ATTN_COL_TILE), F32), pltpu.VMEM((tk, ATTN_COL_TILE), BF16),
                        pltpu.VMEM((1, ATTN_COL_TILE), F32)],
        compiler_params=_cparams(("parallel", "parallel", "parallel")),
        name="attn",
    )(q, k_all, vt_all, lam_vecs, subln_g)


def _solve_unit_triangular(a_list, rhs_list, bd_mask):
    def mm(xs, ys):
        return [_dot(x.astype(BF16), y.astype(BF16)) for x, y in zip(xs, ys)]

    d = [jnp.where(bd_mask, a, 0.0) for a in a_list]
    n = [a - di for a, di in zip(a_list, d)]
    d2 = mm(d, d)
    dd2 = mm(d, d2)
    p = [(-di + d2i) - t for di, d2i, t in zip(d, d2, dd2)]
    d4 = mm(d2, d2)
    pd4 = mm(p, d4)
    p = [pi + d4i + t for pi, d4i, t in zip(p, d4, pd4)]
    d8 = mm(d4, d4)
    pd8 = mm(p, d8)
    p = [pi + d8i + t for pi, d8i, t in zip(p, d8, pd8)]
    m = [ni + t for ni, t in zip(n, mm(p, n))]
    z = [ri + t for ri, t in zip(rhs_list, mm(p, rhs_list))]
    m2 = mm(m, m)
    z = [zi + t for zi, t in zip(z, mm(m2, z))]
    return [zi - t for zi, t in zip(z, mm(m, z))]


def _gdn_scan_kernel(qf_ref, kf_ref, vf_ref, gbf_ref, gbtf_ref, qb_ref, kb_ref, vb_ref, gbb_ref, gbtb_ref,
                     s0_ref, of_ref, ob_ref, sfin_ref, s_ref):
    i = pl.program_id(1)

    @pl.when(i == 0)
    def _():
        s_ref[...] = s0_ref[0]

    c64 = CHUNK
    nchunk = qf_ref.shape[1] // c64
    ri = lax.broadcasted_iota(jnp.int32, (c64, c64), 0)
    ci = lax.broadcasted_iota(jnp.int32, (c64, c64), 1)
    bd_mask = (ri // 16) == (ci // 16)
    lower_incl, upper_incl = ri >= ci, ri <= ci
    heads = range(GDN_HEADS)
    rows = [slice(c * c64, (c + 1) * c64) for c in range(nchunk)]
    cols = [slice(h * GDN_DK, (h + 1) * GDN_DK) for h in heads]

    dirs = []
    for d, refs in enumerate(((qf_ref, kf_ref, vf_ref, gbf_ref, gbtf_ref, of_ref),
                              (qb_ref, kb_ref, vb_ref, gbb_ref, gbtb_ref, ob_ref))):
        rev = d == 1
        incl = upper_incl if rev else lower_incl
        dirs.append(dict(
            refs=refs, incl=incl, strict=(ri < ci) if rev else (ri > ci),
            tri_c=jnp.where(incl, 1.0, 0.0).astype(BF16),
            tri_r=jnp.where(lower_incl if rev else upper_incl, 1.0, 0.0).astype(BF16),
            last=0 if rev else c64 - 1,
            order=list(range(nchunk - 1, -1, -1) if rev else range(nchunk))))

    gall, gc_col, gc_row = {}, {}, {}
    for d, dr in enumerate(dirs):
        gb_ref, gbt_ref = dr["refs"][3], dr["refs"][4]
        for c in range(nchunk):
            gall[d, c] = gb_ref[0, rows[c], :]
            g1, g2, g3 = _split3(gall[d, c])
            tc, tr = dr["tri_c"], dr["tri_r"]
            gc_col[d, c] = _dot(tc, g1) + _dot(tc, g2) + _dot(tc, g3)
            t1, t2, t3 = _split3(gbt_ref[0, :, rows[c]])
            gc_row[d, c] = _dot(t1, tr) + _dot(t2, tr) + _dot(t3, tr)

    probs = [(d, c, h) for d in range(2) for c in range(nchunk) for h in heads]
    ln = lambda d, h: d * GDN_HEADS + h
    k = {(d, c, h): dirs[d]["refs"][1][0, rows[c], cols[h]] for d, c, h in probs}
    q = {(d, c, h): dirs[d]["refs"][0][0, rows[c], cols[h]] for d, c, h in probs}
    kbf = {p: k[p].astype(BF16) for p in probs}
    kk = {p: _dot_nt(kbf[p], kbf[p]) for p in probs}
    qk = {p: _dot_nt(q[p].astype(BF16), kbf[p]) for p in probs}
    a_list, rhs_list = [], []
    kd, qd, aqk, glast = {}, {}, {}, {}
    for p in probs:
        d, c, h = p
        dr = dirs[d]
        lane = ln(d, h)
        gcol = gc_col[d, c][:, lane:lane + 1]
        bcol = gall[d, c][:, GDN_AB // 2 + lane:GDN_AB // 2 + lane + 1]
        diff = gcol - gc_row[d, c][lane:lane + 1, :]
        e_strict = jnp.where(dr["strict"], jnp.exp(jnp.where(dr["strict"], diff, 0.0)), 0.0)
        e_incl = jnp.where(dr["incl"], jnp.exp(jnp.where(dr["incl"], diff, 0.0)), 0.0)
        eg = jnp.exp(gcol)
        glast[p] = gcol[dr["last"]:dr["last"] + 1, :]
        a_list.append(bcol * kk[p] * e_strict)
        v = dr["refs"][2][0, rows[c], cols[h]]
        rhs_list.append(jnp.concatenate([(bcol * eg) * k[p], bcol * v], axis=1))
        kd[p] = (k[p] * jnp.exp(glast[p] - gcol)).astype(BF16)
        qd[p] = (q[p] * eg).astype(BF16)
        aqk[p] = (qk[p] * e_incl).astype(BF16)
    sol = dict(zip(probs, _solve_unit_triangular(a_list, rhs_list, bd_mask)))

    chains = [(d, h) for d in range(2) for h in heads]
    for t in range(nchunk):
        cur = {(d, h): (d, dirs[d]["order"][t], h) for d, h in chains}
        s = {dh: s_ref[dh[0], dh[1]] for dh in chains}
        sb = {dh: s[dh].astype(BF16) for dh in chains}
        ws = {dh: _dot(sol[cur[dh]][:, :GDN_DK].astype(BF16), sb[dh]) for dh in chains}
        qs = {dh: _dot(qd[cur[dh]], sb[dh]) for dh in chains}
        u = {dh: (sol[cur[dh]][:, GDN_DK:] - ws[dh]).astype(BF16) for dh in chains}
        au = {dh: _dot(aqk[cur[dh]], u[dh]) for dh in chains}
        ku = {dh: _dot_tn(kd[cur[dh]], u[dh]) for dh in chains}
        for dh in chains:
            d, c, h = cur[dh]
            s_ref[d, h] = jnp.exp(glast[cur[dh]]) * s[dh] + ku[dh]
            dirs[d]["refs"][5][0, rows[c], cols[h]] = qs[dh] + au[dh]

    @pl.when(i == pl.num_programs(1) - 1)
    def _():
        sfin_ref[0] = s_ref[...]


def _gdn_scan_call(q, k, v, gb, gbt, s0, sc):
    b, n, _ = q.shape
    nsup = n // sc
    fwd = lambda bi, i: (bi, i, 0)
    bwd = lambda bi, i: (bi, nsup - 1 - i, 0)
    st = lambda bi, i: (bi, 0, 0, 0, 0)
    state = (1, 2, GDN_HEADS, GDN_DK, GDN_DV)

    def in_specs(row, tr):
        return [pl.BlockSpec((1, sc, GDN_QK), row), pl.BlockSpec((1, sc, GDN_QK), row),
                pl.BlockSpec((1, sc, GDN_V), row), pl.BlockSpec((1, sc, LANES), row),
                pl.BlockSpec((1, GDN_AB, sc), tr)]

    return pl.pallas_call(
        _gdn_scan_kernel,
        grid=(b, nsup),
        in_specs=(in_specs(fwd, lambda bi, i: (bi, 0, i))
                  + in_specs(bwd, lambda bi, i: (bi, 0, nsup - 1 - i))
                  + [pl.BlockSpec(state, st)]),
        out_specs=[pl.BlockSpec((1, sc, GDN_V), fwd), pl.BlockSpec((1, sc, GDN_V), bwd),
                   pl.BlockSpec(state, st)],
        out_shape=[jax.ShapeDtypeStruct((b, n, GDN_V), F32), jax.ShapeDtypeStruct((b, n, GDN_V), F32),
                   jax.ShapeDtypeStruct((b,) + state[1:], F32)],
        scratch_shapes=[pltpu.VMEM(state[1:], F32)],
        compiler_params=_cparams(("parallel", "arbitrary")),
        name="gdn_scan",
    )(q, k, v, gb, gbt, q, k, v, gb, gbt, s0)


def _outproj_kernel(oda_ref, of_ref, ob_ref, gate_ref, x_ref, gt1_ref, gng_ref, wout_ref,
                    n2g_ref, sh2_ref, sc2_ref, wr_ref, xnew_ref, ht_ref, aff_ref, afft_ref):
    tb = ht_ref.shape[3]
    w1, w2, _ = _split3(wr_ref[...])

    def mix_and_project(c):
        rows = slice(c * tb, (c + 1) * tb)
        og = of_ref[0, rows, :] + ob_ref[0, rows, :]
        gate = gate_ref[0, rows, :]
        ys = []
        for h in range(GDN_HEADS):
            cols = slice(h * GDN_DV, (h + 1) * GDN_DV)
            t = og[:, cols]
            y = t * lax.rsqrt(jnp.mean(t * t, axis=-1, keepdims=True) + EPS) * gng_ref[...]
            ys.append((y * _silu(gate[:, cols])).astype(BF16))
        y_gdn = jnp.concatenate(ys, axis=1)
        proj = _dot(oda_ref[0, rows, :], wout_ref[0:DA_V, :]) + _dot(y_gdn, wout_ref[DA_V:, :])
        return x_ref[0, rows, :] + gt1_ref[0] * proj

    def norm_and_route(c, xn):
        rows = slice(c * tb, (c + 1) * tb)
        xnew_ref[0, rows, :] = xn
        ms = jnp.mean(xn * xn, axis=-1, keepdims=True)
        hm = (xn * lax.rsqrt(ms + EPS) * n2g_ref[...]) * (1.0 + sc2_ref[0]) + sh2_ref[0]
        ht_ref[0, c] = hm.T.astype(BF16)
        h1, h2, _ = _split3(hm)
        logits = _dot(h1, w1) + (_dot(h1, w2) + _dot(h2, w1))
        lane = lax.broadcasted_iota(jnp.int32, logits.shape, 1)
        logits = jnp.where(lane < N_EXPERTS, logits, NEG_BIG)
        e = jnp.exp(logits - jnp.max(logits, axis=-1, keepdims=True))
        aff = e / jnp.sum(e, axis=-1, keepdims=True)
        aff_ref[0, rows, :] = aff
        afft_ref[0, :, rows] = aff.T[:N_EXPERTS, :]

    nblk = ht_ref.shape[1]
    xn = mix_and_project(0)
    for c in range(nblk):
        xn_next = mix_and_project(c + 1) if c + 1 < nblk else None
        norm_and_route(c, xn)
        xn = xn_next


def _outproj_call(o_da, o_f, o_b, gate, x, gt1, gdn_norm_g, w_out_bf, norm2_g, sh2, sc2, wr_pad, tm, tb):
    b, n, d = x.shape
    assert tm % tb == 0
    row = lambda bi, i: (bi, i, 0)
    vec = lambda bi, i: (bi, 0, 0)
    const2 = lambda bi, i: (0, 0)
    return pl.pallas_call(
        _outproj_kernel,
        grid=(b, n // tm),
        in_specs=[pl.BlockSpec((1, tm, DA_V), row), pl.BlockSpec((1, tm, GDN_V), row),
                  pl.BlockSpec((1, tm, GDN_V), row), pl.BlockSpec((1, tm, GDN_V), row),
                  pl.BlockSpec((1, tm, d), row), pl.BlockSpec((1, 1, d), vec),
                  pl.BlockSpec((1, GDN_DV), const2), pl.BlockSpec((DA_V + GDN_V, d), const2),
                  pl.BlockSpec((1, d), const2), pl.BlockSpec((1, 1, d), vec), pl.BlockSpec((1, 1, d), vec),
                  pl.BlockSpec((d, LANES), const2)],
        out_specs=[pl.BlockSpec((1, tm, d), row),
                   pl.BlockSpec((1, tm // tb, d, tb), lambda bi, i: (bi, i, 0, 0)),
                   pl.BlockSpec((1, tm, LANES), row),
                   pl.BlockSpec((1, N_EXPERTS, tm), lambda bi, i: (bi, 0, i))],
        out_shape=[jax.ShapeDtypeStruct((b, n, d), F32), jax.ShapeDtypeStruct((b, n // tb, d, tb), BF16),
                   jax.ShapeDtypeStruct((b, n, LANES), F32),
                   jax.ShapeDtypeStruct((b, N_EXPERTS, n), F32)],
        compiler_params=_cparams(("parallel", "parallel")),
        name="outproj",
    )(o_da, o_f, o_b, gate, x, gt1, gdn_norm_g, w_out_bf, norm2_g, sh2, sc2, wr_pad)


GATHER_GROUP = 8
COMBINE_ALIGN = 16
PREFIX_BLK = 256
BISECT_STEPS = 40


def _select_kernel(afft_ref, aff_ref, post_ref, pos_ref, st_ref, rng_ref, *, cap, tb, tj):
    at = afft_ref[0]
    n = at.shape[1]

    def count_ge(t):
        return jnp.sum(jnp.where(at >= t, 1.0, 0.0), axis=1, keepdims=True)

    def bisect(_, bounds):
        lo, hi = bounds
        mid = 0.5 * (lo + hi)
        ok = count_ge(mid) >= cap
        return jnp.where(ok, mid, lo), jnp.where(ok, hi, mid)

    _, hi = lax.fori_loop(0, BISECT_STEPS, bisect,
                          (jnp.zeros((N_EXPERTS, 1), F32), jnp.full((N_EXPERTS, 1), 2.0, F32)))

    def below(h):
        return jnp.max(jnp.where(at < h, at, -1.0), axis=1, keepdims=True)

    def short(state):
        return jnp.sum(jnp.where(count_ge(state[0]) < cap, 1.0, 0.0)) > 0.0

    def step_down(state):
        t, h = state
        lacking = count_ge(t) < cap
        h = jnp.where(lacking, t, h)
        return jnp.where(lacking, below(h), t), h

    thr, _ = lax.while_loop(short, step_down, (below(hi), hi))
    need = cap - jnp.sum(jnp.where(at > thr, 1.0, 0.0), axis=1, keepdims=True)

    pi = lax.broadcasted_iota(jnp.int32, (PREFIX_BLK, PREFIX_BLK), 0)
    pj = lax.broadcasted_iota(jnp.int32, (PREFIX_BLK, PREFIX_BLK), 1)
    upper = jnp.where(pi <= pj, 1.0, 0.0).astype(BF16)
    lower = jnp.where(pi >= pj, 1.0, 0.0).astype(BF16)

    def prefix_lanes(m):
        carry = jnp.zeros((m.shape[0], 1), F32)
        outs = []
        for c in range(m.shape[1] // PREFIX_BLK):
            mc = m[:, c * PREFIX_BLK:(c + 1) * PREFIX_BLK]
            loc = _dot(mc.astype(BF16), upper)
            outs.append(loc - mc + carry)
            carry = carry + loc[:, PREFIX_BLK - 1:PREFIX_BLK]
        return jnp.concatenate(outs, axis=1)

    def prefix_rows(m):
        carry = jnp.zeros((1, m.shape[1]), F32)
        outs = []
        for c in range(m.shape[0] // PREFIX_BLK):
            mc = m[c * PREFIX_BLK:(c + 1) * PREFIX_BLK, :]
            loc = _dot(lower, mc.astype(BF16))
            outs.append(loc - mc + carry)
            carry = carry + loc[PREFIX_BLK - 1:PREFIX_BLK, :]
        return jnp.concatenate(outs, axis=0)

    eq_t = jnp.where(at == thr, 1.0, 0.0)
    sel_t = jnp.where(at > thr, 1.0, jnp.where(prefix_lanes(eq_t) < need, eq_t, 0.0))
    post_ref[0] = jnp.where(sel_t > 0.0, prefix_lanes(sel_t), -1.0)

    ti = lax.broadcasted_iota(jnp.int32, (n, LANES), 0)
    bi = lax.broadcasted_iota(jnp.int32, (n, LANES), 1)
    before = jnp.where(ti < bi * tb, 1.0, 0.0).astype(BF16)
    st = _dot(sel_t.astype(BF16), before)
    st_ref[0] = st.astype(jnp.int32)

    through = jnp.where(ti < (bi + 1) * tb, 1.0, 0.0).astype(BF16)
    st_end = _dot(sel_t.astype(BF16), through)
    blk_lane = lax.broadcasted_iota(jnp.int32, (N_EXPERTS, LANES), 1)
    is_blk = blk_lane < n // tb
    rng = jnp.zeros((N_EXPERTS, LANES), F32)
    ntile = cap // tj
    for j in range(ntile):
        first = jnp.sum(jnp.where(is_blk & (st_end <= j * tj), 1.0, 0.0), axis=1, keepdims=True)
        last = jnp.sum(jnp.where(is_blk & (st < (j + 1) * tj), 1.0, 0.0), axis=1, keepdims=True) - 1.0
        rng = rng + jnp.where(blk_lane == j, first, 0.0) + jnp.where(blk_lane == ntile + j, last, 0.0)
    rng_ref[0] = rng.astype(jnp.int32)

    a = aff_ref[0]
    er = lax.broadcasted_iota(jnp.int32, (N_EXPERTS, LANES), 0)
    ec = lax.broadcasted_iota(jnp.int32, (N_EXPERTS, LANES), 1)
    diag = er == ec
    thr_row = jnp.sum(jnp.where(diag, thr, 0.0), axis=0, keepdims=True)
    need_row = jnp.sum(jnp.where(diag, need, 0.0), axis=0, keepdims=True)
    valid = lax.broadcasted_iota(jnp.int32, a.shape, 1) < N_EXPERTS
    eq = jnp.where(valid & (a == thr_row), 1.0, 0.0)
    sel = jnp.where(valid & (a > thr_row), 1.0, jnp.where(prefix_rows(eq) < need_row, eq, 0.0))
    pos_ref[0] = jnp.where(sel > 0.0, prefix_rows(sel), -1.0)


def _select_call(afft, aff, cap, tb, tj):
    b, e, n = afft.shape
    assert n // tb + 1 <= LANES and 2 * (cap // tj) <= LANES
    return pl.pallas_call(
        functools.partial(_select_kernel, cap=cap, tb=tb, tj=tj),
        grid=(b,),
        in_specs=[pl.BlockSpec((1, e, n), lambda bi: (bi, 0, 0)),
                  pl.BlockSpec((1, n, LANES), lambda bi: (bi, 0, 0))],
        out_specs=[pl.BlockSpec((1, e, n), lambda bi: (bi, 0, 0)),
                   pl.BlockSpec((1, n, LANES), lambda bi: (bi, 0, 0)),
                   pl.BlockSpec((1, e, LANES), lambda bi: (bi, 0, 0)),
                   pl.BlockSpec((1, e, LANES), lambda bi: (bi, 0, 0))],
        out_shape=[jax.ShapeDtypeStruct((b, e, n), F32), jax.ShapeDtypeStruct((b, n, LANES), F32),
                   jax.ShapeDtypeStruct((b, e, LANES), jnp.int32),
                   jax.ShapeDtypeStruct((b, e, LANES), jnp.int32)],
        compiler_params=_cparams(("parallel",)),
        name="select",
    )(afft, aff)


def _moe_ffn_kernel(rng_ref, ht_ref, post_ref, afft_ref, wg_ref, wu_ref, wd_ref, ye_ref,
                    acc_ref, gacc_ref, wgb_ref, wub_ref, wdb_ref, *, tj, tb):
    b = pl.program_id(0)
    e = pl.program_id(1)
    cap = ye_ref.shape[2]
    ntile = cap // tj
    slot = lax.broadcasted_iota(jnp.int32, (tj, tb), 0).astype(F32)

    for j in range(cap // tj):
        lo = j * tj
        acc_ref[...] = jnp.zeros(acc_ref.shape, F32)
        gacc_ref[...] = jnp.zeros(gacc_ref.shape, F32)

        first = rng_ref[b, e, j]
        last = rng_ref[b, e, ntile + j]

        def gather_blocks(start, count):
            part = jnp.zeros(acc_ref.shape, F32)
            gate = jnp.zeros(gacc_ref.shape, F32)
            for u in range(count):
                bi = start + u
                prow = post_ref[0, 0, pl.ds(bi, 1), :]
                hit = prow == (slot + float(lo))
                onehot = jnp.where(hit, 1.0, 0.0).astype(BF16)
                part = part + _dot_nt(ht_ref[0, bi], onehot)
                arow = afft_ref[0, 0, pl.ds(bi, 1), :]
                gate = gate + jnp.sum(jnp.where(hit, arow, 0.0), axis=1, keepdims=True)
            acc_ref[...] += part
            gacc_ref[...] += gate

        def group_body(g, carry):
            gather_blocks(first + g * GATHER_GROUP, GATHER_GROUP)
            return carry

        nblocks = last - first + 1
        full = nblocks // GATHER_GROUP
        lax.fori_loop(0, full, group_body, 0)
        done = first + full * GATHER_GROUP
        size = GATHER_GROUP // 2
        while size >= 1:
            @pl.when((nblocks & size) != 0)
            def _(done=done, size=size):
                gather_blocks(done, size)
            done = done + (nblocks & size)
            size //= 2
        xe = acc_ref[...].T.astype(BF16)
        if j == 0:
            wgb_ref[...] = wg_ref[0].astype(BF16)
            wub_ref[...] = wu_ref[0].astype(BF16)
            wdb_ref[...] = wd_ref[0].astype(BF16)
        hid = _silu(_dot(xe, wgb_ref[...])) * _dot(xe, wub_ref[...])
        ye = _dot(hid.astype(BF16), wdb_ref[...]) * gacc_ref[...]
        ye_ref[0, 0, lo:lo + tj, :] = ye.astype(ye_ref.dtype)


def _moe_ffn_call(tile_rng, ht4, post4, afft4, wg, wu, wd, cap, tj):
    b, nblk, d, tb = ht4.shape
    e, _, f = wg.shape
    wspec = lambda shp: pl.BlockSpec((1,) + shp, lambda bi, ei: (ei, 0, 0))
    return pl.pallas_call(
        functools.partial(_moe_ffn_kernel, tj=tj, tb=tb),
        grid=(b, e),
        in_specs=[pl.BlockSpec(memory_space=pltpu.SMEM),
                  pl.BlockSpec((1, nblk, d, tb), lambda bi, ei: (bi, 0, 0, 0), pipeline_mode=pl.Buffered(1)),
                  pl.BlockSpec((1, 1, nblk, tb), lambda bi, ei: (bi, ei, 0, 0)),
                  pl.BlockSpec((1, 1, nblk, tb), lambda bi, ei: (bi, ei, 0, 0)),
                  wspec((d, f)), wspec((d, f)), wspec((f, d))],
        out_specs=pl.BlockSpec((1, 1, cap, d), lambda bi, ei: (bi, ei, 0, 0)),
        out_shape=jax.ShapeDtypeStruct((b, e, cap, d), BF16),
        scratch_shapes=[pltpu.VMEM((d, tj), F32), pltpu.VMEM((tj, 1), F32),
                        pltpu.VMEM((d, f), BF16), pltpu.VMEM((d, f), BF16), pltpu.VMEM((f, d), BF16)],
        compiler_params=_cparams(("parallel", "arbitrary")),
        name="moe_ffn",
    )(tile_rng, ht4, post4, afft4, wg, wu, wd)


def _combine_kernel(st_ref, pos_ref, ye_ref, x_ref, gt2_ref, g_ref, o_ref, rest_ref, *, tb):
    b = pl.program_id(0)
    blk = pl.program_id(1)
    cap = ye_ref.shape[2]
    win = tb
    slot = lax.broadcasted_iota(jnp.int32, (tb, win), 1).astype(F32)
    pos = pos_ref[0]

    def window_base(e):
        return jnp.minimum(st_ref[b, e, blk] // COMBINE_ALIGN, (cap - win) // COMBINE_ALIGN) * COMBINE_ALIGN

    def window(e, start):
        return ye_ref[0, e, pl.ds(pl.multiple_of(start, COMBINE_ALIGN), win), :]

    moe = jnp.zeros((tb, ye_ref.shape[3]), F32)
    for e in range(N_EXPERTS):
        base = window_base(e)
        onehot = jnp.where(pos[:, e:e + 1] == slot + base.astype(F32), 1.0, 0.0).astype(BF16)
        moe = moe + _dot(onehot, window(e, base))

    rest_ref[...] = jnp.zeros(rest_ref.shape, F32)
    for e in range(N_EXPERTS):
        done = window_base(e) + win

        @pl.when(st_ref[b, e, blk + 1] > done)
        def _():
            start = jnp.minimum(done, cap - win)
            pcol = pos[:, e:e + 1]
            hit = (pcol == slot + start.astype(F32)) & (pcol >= done.astype(F32))
            rest_ref[...] += _dot(jnp.where(hit, 1.0, 0.0).astype(BF16), window(e, start))

    y = x_ref[0] + gt2_ref[0] * (moe + rest_ref[...])
    ms = jnp.mean(y * y, axis=-1, keepdims=True)
    o_ref[0] = y * lax.rsqrt(ms + EPS) * g_ref[...]


def _combine_call(starts, pos, ye, x_new, gt2, final_g, tb):
    b, e, cap, d = ye.shape
    n = pos.shape[1]
    assert cap % COMBINE_ALIGN == 0 and tb % COMBINE_ALIGN == 0 and cap >= tb
    row = lambda bi, i: (bi, i, 0)
    return pl.pallas_call(
        functools.partial(_combine_kernel, tb=tb),
        grid=(b, n // tb),
        in_specs=[pl.BlockSpec(memory_space=pltpu.SMEM),
                  pl.BlockSpec((1, tb, LANES), row),
                  pl.BlockSpec((1, e, cap, d), lambda bi, i: (bi, 0, 0, 0), pipeline_mode=pl.Buffered(1)),
                  pl.BlockSpec((1, tb, d), row),
                  pl.BlockSpec((1, 1, d), lambda bi, i: (bi, 0, 0)),
                  pl.BlockSpec((1, d), lambda bi, i: (0, 0))],
        out_specs=pl.BlockSpec((1, tb, d), row),
        out_shape=jax.ShapeDtypeStruct((b, n, d), F32),
        scratch_shapes=[pltpu.VMEM((tb, d), F32)],
        compiler_params=_cparams(("parallel", "arbitrary")),
        name="combine",
    )(starts, pos, ye, x_new, gt2, final_g)


def _rope_tables(n):
    t = np.arange(n)
    rows = (t // GRID_W).astype(np.float64)
    cols = (t % GRID_W).astype(np.float64)
    inv_freq = np.power(ROPE_THETA, -np.arange(0, ROPE_AXIS_DIM, 2, dtype=np.float64) / ROPE_AXIS_DIM)
    ang_row = rows[:, None] * inv_freq[None, :]
    ang_col = cols[:, None] * inv_freq[None, :]

    def axis_tables(ang):
        c = np.cos(ang).astype(np.float32)
        s = np.sin(ang).astype(np.float32)
        return np.concatenate([c, c], axis=1), np.concatenate([-s, s], axis=1)

    cr, sr = axis_tables(ang_row)
    cc, sc = axis_tables(ang_col)
    cos64 = np.concatenate([cr, cc], axis=1)
    sin64 = np.concatenate([sr, sc], axis=1)
    return (jnp.asarray(np.concatenate([cos64, cos64], axis=1)),
            jnp.asarray(np.concatenate([sin64, sin64], axis=1)))


def _attn_chunk(nk):
    return max(t for t in range(VT_BLK, 1024 + 1, VT_BLK) if nk % t == 0)


def _pad_lanes(v):
    return jnp.pad(v.reshape(1, -1), ((0, 0), (0, LANES - v.size)))


def kernel(x, c, ctx, c_ctx, w_mod, b_mod, norm1_g, w_in, conv_w, a_log, dt_bias, gdn_norm_g,
           lam_q1, lam_k1, lam_q2, lam_k2, da_subln_g, w_out, norm2_g,
           w_router, w_gate, w_up, w_down, final_g):
    b, n, d = x.shape
    nc = ctx.shape[1]
    layer = 0

    cvec = jnp.concatenate([c, c_ctx[None, :], jnp.zeros((8 - b - 1, d), F32)], axis=0)
    mod = _mod_call(cvec, w_mod[layer], b_mod[layer])
    sh1, sc1, gt1, sh2, sc2, gt2 = [mod[:b, i * d:(i + 1) * d].reshape(b, 1, d) for i in range(6)]
    sh1c = jnp.broadcast_to(mod[b:b + 1, 0:d].reshape(1, 1, d), (b, 1, d))
    sc1c = jnp.broadcast_to(mod[b:b + 1, d:2 * d].reshape(1, 1, d), (b, 1, d))

    w_in_bf = jnp.pad(w_in[layer].astype(BF16), ((0, 0), (0, IN_COLS_PAD - w_in.shape[2])))
    g1 = norm1_g[layer].reshape(1, d)
    cos_l, sin_l = _rope_tables(n)
    cos_c, sin_c = jnp.ones((nc, LANES), F32), jnp.zeros((nc, LANES), F32)
    conv_w8 = jnp.pad(conv_w[layer], ((0, 8 - conv_w.shape[1]), (0, 0)))
    alog_row = _pad_lanes(a_log[layer])
    dtb_row = _pad_lanes(dt_bias[layer])
    gdn_args = (conv_w8, alog_row, dtb_row)
    q, k, vt, ql, kl, vl, gate, gbl, gbtl = _inproj_call(
        x, g1, sh1, sc1, cos_l, sin_l, w_in_bf, *gdn_args, tm=1024)
    _, kc, vct, qc, kcg, vcg, _, gbc, gbtc = _inproj_call(
        ctx, g1, sh1c, sc1c, cos_c, sin_c, w_in_bf, *gdn_args, tm=nc)

    lam_vecs = jnp.stack([lam_q1[layer], lam_k1[layer], lam_q2[layer], lam_k2[layer]], axis=0)
    k_all = jnp.concatenate([kc, k], axis=1)
    vt_all = jnp.concatenate([vct, vt], axis=2)
    o_da = _attn_call(q, k_all, vt_all, lam_vecs, da_subln_g[layer].reshape(1, DA_DV),
                      tq=n, tk=_attn_chunk(nc + n))

    zeros_state = jnp.zeros((b, 2, GDN_HEADS, GDN_DK, GDN_DV), F32)
    _, _, s_ctx = _gdn_scan_call(qc, kcg, vcg, gbc, gbtc, zeros_state, sc=nc)
    o_fwd, o_bwd, _ = _gdn_scan_call(ql, kl, vl, gbl, gbtl, s_ctx, sc=256)
    o_dirs = [o_fwd, o_bwd]

    wr_pad = jnp.pad(w_router[layer], ((0, 0), (0, LANES - N_EXPERTS)))
    cap = CAP_FACTOR * n // N_EXPERTS
    tb = 256
    tj = 256
    x_new, ht4, aff, afft = _outproj_call(
        o_da, o_dirs[0], o_dirs[1], gate, x, gt1, gdn_norm_g[layer].reshape(1, GDN_DV),
        w_out[layer].astype(BF16), norm2_g[layer].reshape(1, d), sh2, sc2, wr_pad, tm=1024, tb=tb)

    post, pos, starts, tile_rng = _select_call(afft, aff, cap, tb, tj)
    nblk = n // tb
    ye = _moe_ffn_call(tile_rng, ht4, post.reshape(b, N_EXPERTS, nblk, tb),
                       afft.reshape(b, N_EXPERTS, nblk, tb),
                       w_gate[layer], w_up[layer], w_down[layer], cap, tj)
    return _combine_call(starts, pos, ye, x_new, gt2, final_g.reshape(1, d), tb)
```

```python
import functools
import math

import jax
import jax.numpy as jnp
import numpy as np
from jax import lax
from jax.experimental import pallas as pl
from jax.experimental.pallas import tpu as pltpu

F32 = jnp.float32
BF16 = jnp.bfloat16

D_MODEL = 1024
GRID_W = 64
EPS = 1e-6
DA_HEADS = 4
DA_D = 64
DA_DV = 2 * DA_D
ROPE_AXIS_DIM = DA_D // 2
ROPE_THETA = 10000.0
GDN_HEADS = 4
GDN_DK = 128
GDN_DV = 128
CHUNK = 64
N_EXPERTS = 16
CAP_FACTOR = 2
LAM_INIT = 0.8 - 0.6 * math.exp(-0.3 * 0)

DA_QK = DA_HEADS * 2 * DA_D
DA_V = DA_HEADS * DA_DV
GDN_QK = GDN_HEADS * GDN_DK
GDN_V = GDN_HEADS * GDN_DV
GDN_QKV = 2 * GDN_QK + GDN_V
GDN_AB = 2 * 2 * GDN_HEADS
COL_Q, COL_K, COL_V = 0, DA_QK, 2 * DA_QK
COL_QKV = 2 * DA_QK + DA_V
COL_GATE = COL_QKV + GDN_QKV
COL_AB = COL_GATE + GDN_V
LANES = 128
IN_COLS_PAD = COL_AB + LANES
MXU_WIDTH = 256
ATTN_COL_TILE = MXU_WIDTH
VT_ROWS = DA_DV + 8
VT_BLK = MXU_WIDTH

VMEM_LIMIT = 56 * 1024 * 1024
NEG_BIG = -1e30
LOG2_E = math.log2(math.e)


def _cparams(sem):
    return pltpu.CompilerParams(dimension_semantics=sem, vmem_limit_bytes=VMEM_LIMIT)


def _sigmoid(x):
    return 1.0 / (1.0 + jnp.exp(-x))


def _silu(x):
    return x * _sigmoid(x)


def _split3(a):
    a1 = a.astype(BF16)
    r1 = a - a1.astype(F32)
    a2 = r1.astype(BF16)
    a3 = (r1 - a2.astype(F32)).astype(BF16)
    return a1, a2, a3


def _dot(a, b):
    return jnp.dot(a, b, preferred_element_type=F32)


def _dot_x3(a, b):
    a1, a2, _ = _split3(a)
    b1, b2, _ = _split3(b)
    return _dot(a1, b1) + (_dot(a1, b2) + _dot(a2, b1))


def _dot_nt(a, b):
    return lax.dot_general(a, b, (((1,), (1,)), ((), ())), preferred_element_type=F32)


def _dot_tn(a, b):
    return lax.dot_general(a, b, (((0,), (0,)), ((), ())), preferred_element_type=F32)


def _mod_kernel(c_ref, w_ref, b_ref, o_ref):
    s = _silu(c_ref[...])
    o_ref[...] = jnp.dot(s, w_ref[...], precision=lax.Precision.HIGHEST,
                         preferred_element_type=F32) + b_ref[...]


def _mod_call(cvec, w_mod, b_mod):
    d, n = w_mod.shape
    tn = 1024
    return pl.pallas_call(
        _mod_kernel,
        grid=(n // tn,),
        in_specs=[pl.BlockSpec((8, d), lambda j: (0, 0)),
                  pl.BlockSpec((d, tn), lambda j: (0, j)),
                  pl.BlockSpec((1, tn), lambda j: (0, j))],
        out_specs=pl.BlockSpec((8, tn), lambda j: (0, j)),
        out_shape=jax.ShapeDtypeStruct((8, n), F32),
        compiler_params=_cparams(("arbitrary",)),
        name="mod",
    )(cvec, w_mod, b_mod.reshape(1, n))


def _inproj_kernel(x_ref, xp_ref, xn_ref, g_ref, sh_ref, sc_ref, cos_ref, sin_ref, w_ref,
                   cw_ref, al_ref, dtb_ref,
                   q_ref, k_ref, vt_ref, gq_ref, gk_ref, gv_ref, gate_ref, gb_ref, gbt_ref):
    i = pl.program_id(1)
    nblk = pl.num_programs(1)
    tm = x_ref.shape[1]

    def modulated(xb):
        ms = jnp.mean(xb * xb, axis=-1, keepdims=True)
        hm = (xb * lax.rsqrt(ms + EPS) * g_ref[...]) * (1.0 + sc_ref[0]) + sh_ref[0]
        return hm.astype(BF16)

    sb = VT_BLK
    nsb = tm // sb
    lane = lax.broadcasted_iota(jnp.int32, (sb, LANES), 1)
    first_half = (lane % ROPE_AXIS_DIM) < (ROPE_AXIS_DIM // 2)
    extra = lax.broadcasted_iota(jnp.int32, (DA_HEADS, VT_ROWS - DA_DV, sb), 1)
    ones_row = jnp.where(extra == 0, 1.0, 0.0).astype(BF16)
    row = lax.broadcasted_iota(jnp.int32, (sb, 1), 0)
    cw = cw_ref[...]

    halo = _dot(modulated(jnp.concatenate([xp_ref[0], xn_ref[0]], axis=0)),
                w_ref[:, COL_QKV:COL_GATE])
    tile_prev = jnp.where(i > 0, halo[7:8, :], 0.0)
    tile_next = jnp.where(i < nblk - 1, halo[8:9, :], 0.0)

    def project(c):
        hb = modulated(x_ref[0, c * sb:(c + 1) * sb, :])
        cuts = (COL_Q, COL_K, COL_V, COL_QKV, COL_GATE, COL_AB, IN_COLS_PAD)
        return [_dot(hb, w_ref[:, lo:hi]) for lo, hi in zip(cuts[:-1], cuts[1:])]

    def rope(t, cos, sin):
        outs = []
        for j in range(t.shape[1] // LANES):
            s = t[:, LANES * j:LANES * (j + 1)]
            partner = jnp.where(first_half,
                                pltpu.roll(s, LANES - ROPE_AXIS_DIM // 2, 1),
                                pltpu.roll(s, ROPE_AXIS_DIM // 2, 1))
            outs.append(s * cos + partner * sin)
        return jnp.concatenate(outs, axis=1)

    def l2n(t):
        return t * lax.rsqrt(jnp.sum(t * t, axis=-1, keepdims=True) + EPS)

    def finish(c, cur, prev, nxt):
        rows = slice(c * sb, (c + 1) * sb)
        q, k, v, qkv, gate, ab = cur
        cos = cos_ref[rows, :]
        sin = sin_ref[rows, :]
        q_ref[0, rows, :] = (rope(q, cos, sin) * (DA_D ** -0.5 * LOG2_E)).astype(BF16)
        k_ref[0, rows, :] = rope(k, cos, sin).astype(BF16)
        vt_ref[0, :, c, 0:DA_DV, :] = v.T.astype(BF16).reshape(DA_HEADS, DA_DV, sb)
        vt_ref[0, :, c, DA_DV:VT_ROWS, :] = ones_row
        gate_ref[0, rows, :] = gate

        qkv_m1 = jnp.where(row == 0, prev, pltpu.roll(qkv, 1, 0))
        qkv_p1 = jnp.where(row == sb - 1, nxt, pltpu.roll(qkv, sb - 1, 0))
        s = _silu(qkv_m1 * cw[0:1] + qkv * cw[1:2] + qkv_p1 * cw[2:3])
        for h in range(GDN_HEADS):
            lo, hi = h * GDN_DK, (h + 1) * GDN_DK
            gq_ref[0, rows, lo:hi] = l2n(s[:, lo:hi]) * GDN_DK ** -0.5
            gk_ref[0, rows, lo:hi] = l2n(s[:, GDN_QK + lo:GDN_QK + hi])
        gv_ref[0, rows, :] = s[:, 2 * GDN_QK:]

        z = ab + dtb_ref[...]
        softplus = jnp.maximum(z, 0.0) + jnp.log1p(jnp.exp(-jnp.abs(z)))
        gval = -jnp.exp(al_ref[...]) * softplus
        gb = jnp.where(lane < GDN_AB // 2, gval, jnp.where(lane < GDN_AB, _sigmoid(ab), 0.0))
        gb_ref[0, rows, :] = gb
        gbt_ref[0, :, rows] = gb.T[:GDN_AB, :]

    res = {0: project(0)}
    for c in range(nsb):
        if c + 1 < nsb:
            res[c + 1] = project(c + 1)
        prev = res[c - 1][3][sb - 1:sb, :] if c > 0 else tile_prev
        nxt = res[c + 1][3][0:1, :] if c + 1 < nsb else tile_next
        finish(c, res[c], prev, nxt)
        res.pop(c - 1, None)


def _inproj_call(x, norm_g, shift, scale, cos, sin, w_bf, conv_w8, alog_row, dtb_row, tm):
    b, n, d = x.shape
    nb8 = n // 8
    step8 = tm // 8
    row = lambda bi, i: (bi, i, 0)
    vec = lambda bi, i: (bi, 0, 0)
    const = lambda bi, i: (0, 0)
    rows_out = [(DA_QK, BF16), (DA_QK, BF16), None, (GDN_QK, F32), (GDN_QK, F32), (GDN_V, F32),
                (GDN_V, F32), (LANES, F32)]
    out_specs = [pl.BlockSpec((1, tm, o[0]), row) if o else
                 pl.BlockSpec((1, DA_HEADS, tm // VT_BLK, VT_ROWS, VT_BLK), lambda bi, i: (bi, 0, i, 0, 0))
                 for o in rows_out]
    out_shape = [jax.ShapeDtypeStruct((b, n, o[0]), o[1]) if o else
                 jax.ShapeDtypeStruct((b, DA_HEADS, n // VT_BLK, VT_ROWS, VT_BLK), BF16) for o in rows_out]
    out_specs.append(pl.BlockSpec((1, GDN_AB, tm), lambda bi, i: (bi, 0, i)))
    out_shape.append(jax.ShapeDtypeStruct((b, GDN_AB, n), F32))
    return pl.pallas_call(
        _inproj_kernel,
        grid=(b, n // tm),
        in_specs=[pl.BlockSpec((1, tm, d), row),
                  pl.BlockSpec((1, 8, d), lambda bi, i: (bi, jnp.maximum(i * step8 - 1, 0), 0)),
                  pl.BlockSpec((1, 8, d), lambda bi, i: (bi, jnp.minimum((i + 1) * step8, nb8 - 1), 0)),
                  pl.BlockSpec((1, d), const),
                  pl.BlockSpec((1, 1, d), vec),
                  pl.BlockSpec((1, 1, d), vec),
                  pl.BlockSpec((tm, LANES), lambda bi, i: (i, 0)),
                  pl.BlockSpec((tm, LANES), lambda bi, i: (i, 0)),
                  pl.BlockSpec((d, IN_COLS_PAD), const, pipeline_mode=pl.Buffered(1)),
                  pl.BlockSpec((8, GDN_QKV), const),
                  pl.BlockSpec((1, LANES), const),
                  pl.BlockSpec((1, LANES), const)],
        out_specs=out_specs,
        out_shape=out_shape,
        compiler_params=_cparams(("parallel", "parallel")),
        name="inproj",
    )(x, x, x, norm_g, shift, scale, cos, sin, w_bf, conv_w8, alog_row, dtb_row)


def _attn_kernel(q_ref, k_ref, vt_ref, lam_ref, g_ref, o_ref,
                 q2_ref, acc_ref, m_ref, s_ref, p_ref, a_ref, *, tk):
    q = q_ref[0]
    tq = q.shape[0]
    lane = lax.broadcasted_iota(jnp.int32, (tq, LANES), 1)
    zero = jnp.zeros_like(q)
    q2_ref[0:tq, :] = jnp.where(lane < DA_D, q, zero)
    q2_ref[tq:2 * tq, :] = jnp.where(lane >= DA_D, q, zero)
    m_ref[...] = jnp.full(m_ref.shape, NEG_BIG, F32)
    acc_ref[...] = jnp.zeros(acc_ref.shape, F32)

    nct = 2 * tq // ATTN_COL_TILE
    col = lambda ct: slice(ct * ATTN_COL_TILE, (ct + 1) * ATTN_COL_TILE)

    def scores(kk, ct):
        return _dot_nt(kk, q2_ref[col(ct), :])

    def softmax_update(ct, s):
        cols = col(ct)
        m_old = m_ref[:, cols]
        m_new = jnp.maximum(m_old, jnp.max(s, axis=0, keepdims=True))
        alpha = jnp.exp2(m_old - m_new)
        p = jnp.exp2(s - m_new)
        m_ref[:, cols] = m_new
        return p.astype(BF16), alpha

    vblk = vt_ref.shape[4]
    nsub = tk // vblk

    def value_update(ct, j, p, alpha):
        cols = col(ct)
        pv = _dot(vt_ref[0, 0, j * nsub], p[0:vblk, :])
        for c in range(1, nsub):
            pv = pv + _dot(vt_ref[0, 0, j * nsub + c], p[c * vblk:(c + 1) * vblk, :])
        acc_ref[:, cols] = alpha * acc_ref[:, cols] + pv

    nchunk = k_ref.shape[1] // tk
    last = nct - 1
    s_ref[...] = scores(k_ref[0, 0:tk, :], 0)
    p_ref[...] = jnp.zeros(p_ref.shape, BF16)
    a_ref[...] = jnp.ones(a_ref.shape, F32)

    def body(j, carry):
        kk = k_ref[0, pl.ds(pl.multiple_of(j * tk, vblk), tk), :]
        s_cur = s_ref[...]
        pending = (last, jnp.maximum(j - 1, 0), p_ref[...], a_ref[...])
        for ct in range(nct):
            if ct < last:
                s_ahead = scores(kk, ct + 1)
            else:
                jn = jnp.minimum(j + 1, nchunk - 1)
                s_ahead = scores(k_ref[0, pl.ds(pl.multiple_of(jn * tk, vblk), tk), :], 0)
            p, alpha = softmax_update(ct, s_cur)
            value_update(*pending)
            pending = (ct, j, p, alpha)
            s_cur = s_ahead
        s_ref[...] = s_cur
        p_ref[...] = pending[2]
        a_ref[...] = pending[3]
        return carry

    lax.fori_loop(0, nchunk, body, 0)
    value_update(last, nchunk - 1, p_ref[...], a_ref[...])

    lv = lam_ref[...]
    lam = (jnp.exp(jnp.sum(lv[0:1] * lv[1:2], axis=1, keepdims=True))
           - jnp.exp(jnp.sum(lv[2:3] * lv[3:4], axis=1, keepdims=True)) + LAM_INIT)
    den = acc_ref[DA_DV:DA_DV + 1, :]
    ot = (acc_ref[0:DA_DV, 0:tq] / den[:, 0:tq]
          - lam * (acc_ref[0:DA_DV, tq:2 * tq] / den[:, tq:2 * tq]))
    o = ot.T
    ms = jnp.mean(o * o, axis=-1, keepdims=True)
    o_ref[0] = ((o * lax.rsqrt(ms + EPS) * g_ref[...]) * (1.0 - LAM_INIT)).astype(o_ref.dtype)


def _attn_call(q, k_all, vt_all, lam_vecs, subln_g, tq, tk):
    b, n, _ = q.shape
    nk = k_all.shape[1]
    _, _, nvt, _, vblk = vt_all.shape
    assert nk == nvt * vblk and tk % vblk == 0 and nk % tk == 0
    return pl.pallas_call(
        functools.partial(_attn_kernel, tk=tk),
        grid=(b, DA_HEADS, n // tq),
        in_specs=[pl.BlockSpec((1, tq, LANES), lambda bi, h, i: (bi, i, h)),
                  pl.BlockSpec((1, nk, LANES), lambda bi, h, i: (bi, 0, h)),
                  pl.BlockSpec((1, 1, nvt, VT_ROWS, vblk), lambda bi, h, i: (bi, h, 0, 0, 0)),
                  pl.BlockSpec((4, DA_D), lambda bi, h, i: (0, 0)),
                  pl.BlockSpec((1, DA_DV), lambda bi, h, i: (0, 0))],
        out_specs=pl.BlockSpec((1, tq, LANES), lambda bi, h, i: (bi, i, h)),
        out_shape=jax.ShapeDtypeStruct((b, n, DA_V), BF16),
        scratch_shapes=[pltpu.VMEM((2 * tq, LANES), BF16), pltpu.VMEM((VT_ROWS, 2 * tq), F32),
                        pltpu.VMEM((1, 2 * tq), F32),
                        pltpu.VMEM((tk, ATTN_COL_TILE), F32), pltpu.VMEM((tk, ATTN_COL_TILE), BF16),
                        pltpu.VMEM((1, ATTN_COL_TILE), F32)],
        compiler_params=_cparams(("parallel", "parallel", "parallel")),
        name="attn",
    )(q, k_all, vt_all, lam_vecs, subln_g)


def _solve_unit_triangular(a_list, rhs_list, bd_mask):
    def mm(xs, ys):
        return [_dot(x.astype(BF16), y.astype(BF16)) for x, y in zip(xs, ys)]

    d = [jnp.where(bd_mask, a, 0.0) for a in a_list]
    n = [a - di for a, di in zip(a_list, d)]
    d2 = mm(d, d)
    dd2 = mm(d, d2)
    p = [(-di + d2i) - t for di, d2i, t in zip(d, d2, dd2)]
    d4 = mm(d2, d2)
    pd4 = mm(p, d4)
    p = [pi + d4i + t for pi, d4i, t in zip(p, d4, pd4)]
    d8 = mm(d4, d4)
    pd8 = mm(p, d8)
    p = [pi + d8i + t for pi, d8i, t in zip(p, d8, pd8)]
    m = [ni + t for ni, t in zip(n, mm(p, n))]
    z = [ri + t for ri, t in zip(rhs_list, mm(p, rhs_list))]
    m2 = mm(m, m)
    z = [zi + t for zi, t in zip(z, mm(m2, z))]
    return [zi - t for zi, t in zip(z, mm(m, z))]


def _gdn_scan_kernel(qf_ref, kf_ref, vf_ref, gbf_ref, gbtf_ref, qb_ref, kb_ref, vb_ref, gbb_ref, gbtb_ref,
                     s0_ref, of_ref, ob_ref, sfin_ref, s_ref):
    i = pl.program_id(1)

    @pl.when(i == 0)
    def _():
        s_ref[...] = s0_ref[0]

    c64 = CHUNK
    nchunk = qf_ref.shape[1] // c64
    ri = lax.broadcasted_iota(jnp.int32, (c64, c64), 0)
    ci = lax.broadcasted_iota(jnp.int32, (c64, c64), 1)
    bd_mask = (ri // 16) == (ci // 16)
    lower_incl, upper_incl = ri >= ci, ri <= ci
    heads = range(GDN_HEADS)
    rows = [slice(c * c64, (c + 1) * c64) for c in range(nchunk)]
    cols = [slice(h * GDN_DK, (h + 1) * GDN_DK) for h in heads]

    dirs = []
    for d, refs in enumerate(((qf_ref, kf_ref, vf_ref, gbf_ref, gbtf_ref, of_ref),
                              (qb_ref, kb_ref, vb_ref, gbb_ref, gbtb_ref, ob_ref))):
        rev = d == 1
        incl = upper_incl if rev else lower_incl
        dirs.append(dict(
            refs=refs, incl=incl, strict=(ri < ci) if rev else (ri > ci),
            tri_c=jnp.where(incl, 1.0, 0.0).astype(BF16),
            tri_r=jnp.where(lower_incl if rev else upper_incl, 1.0, 0.0).astype(BF16),
            last=0 if rev else c64 - 1,
            order=list(range(nchunk - 1, -1, -1) if rev else range(nchunk))))

    gall, gc_col, gc_row = {}, {}, {}
    for d, dr in enumerate(dirs):
        gb_ref, gbt_ref = dr["refs"][3], dr["refs"][4]
        for c in range(nchunk):
            gall[d, c] = gb_ref[0, rows[c], :]
            g1, g2, g3 = _split3(gall[d, c])
            tc, tr = dr["tri_c"], dr["tri_r"]
            gc_col[d, c] = _dot(tc, g1) + _dot(tc, g2) + _dot(tc, g3)
            t1, t2, t3 = _split3(gbt_ref[0, :, rows[c]])
            gc_row[d, c] = _dot(t1, tr) + _dot(t2, tr) + _dot(t3, tr)

    probs = [(d, c, h) for d in range(2) for c in range(nchunk) for h in heads]
    ln = lambda d, h: d * GDN_HEADS + h
    k = {(d, c, h): dirs[d]["refs"][1][0, rows[c], cols[h]] for d, c, h in probs}
    q = {(d, c, h): dirs[d]["refs"][0][0, rows[c], cols[h]] for d, c, h in probs}
    kbf = {p: k[p].astype(BF16) for p in probs}
    kk = {p: _dot_nt(kbf[p], kbf[p]) for p in probs}
    qk = {p: _dot_nt(q[p].astype(BF16), kbf[p]) for p in probs}
    a_list, rhs_list = [], []
    kd, qd, aqk, glast = {}, {}, {}, {}
    for p in probs:
        d, c, h = p
        dr = dirs[d]
        lane = ln(d, h)
        gcol = gc_col[d, c][:, lane:lane + 1]
        bcol = gall[d, c][:, GDN_AB // 2 + lane:GDN_AB // 2 + lane + 1]
        diff = gcol - gc_row[d, c][lane:lane + 1, :]
        e_strict = jnp.where(dr["strict"], jnp.exp(jnp.where(dr["strict"], diff, 0.0)), 0.0)
        e_incl = jnp.where(dr["incl"], jnp.exp(jnp.where(dr["incl"], diff, 0.0)), 0.0)
        eg = jnp.exp(gcol)
        glast[p] = gcol[dr["last"]:dr["last"] + 1, :]
        a_list.append(bcol * kk[p] * e_strict)
        v = dr["refs"][2][0, rows[c], cols[h]]
        rhs_list.append(jnp.concatenate([(bcol * eg) * k[p], bcol * v], axis=1))
        kd[p] = (k[p] * jnp.exp(glast[p] - gcol)).astype(BF16)
        qd[p] = (q[p] * eg).astype(BF16)
        aqk[p] = (qk[p] * e_incl).astype(BF16)
    sol = dict(zip(probs, _solve_unit_triangular(a_list, rhs_list, bd_mask)))

    chains = [(d, h) for d in range(2) for h in heads]
    for t in range(nchunk):
        cur = {(d, h): (d, dirs[d]["order"][t], h) for d, h in chains}
        s = {dh: s_ref[dh[0], dh[1]] for dh in chains}
        sb = {dh: s[dh].astype(BF16) for dh in chains}
        ws = {dh: _dot(sol[cur[dh]][:, :GDN_DK].astype(BF16), sb[dh]) for dh in chains}
        qs = {dh: _dot(qd[cur[dh]], sb[dh]) for dh in chains}
        u = {dh: (sol[cur[dh]][:, GDN_DK:] - ws[dh]).astype(BF16) for dh in chains}
        au = {dh: _dot(aqk[cur[dh]], u[dh]) for dh in chains}
        ku = {dh: _dot_tn(kd[cur[dh]], u[dh]) for dh in chains}
        for dh in chains:
            d, c, h = cur[dh]
            s_ref[d, h] = jnp.exp(glast[cur[dh]]) * s[dh] + ku[dh]
            dirs[d]["refs"][5][0, rows[c], cols[h]] = qs[dh] + au[dh]

    @pl.when(i == pl.num_programs(1) - 1)
    def _():
        sfin_ref[0] = s_ref[...]


def _gdn_scan_call(q, k, v, gb, gbt, s0, sc):
    b, n, _ = q.shape
    nsup = n // sc
    fwd = lambda bi, i: (bi, i, 0)
    bwd = lambda bi, i: (bi, nsup - 1 - i, 0)
    st = lambda bi, i: (bi, 0, 0, 0, 0)
    state = (1, 2, GDN_HEADS, GDN_DK, GDN_DV)

    def in_specs(row, tr):
        return [pl.BlockSpec((1, sc, GDN_QK), row), pl.BlockSpec((1, sc, GDN_QK), row),
                pl.BlockSpec((1, sc, GDN_V), row), pl.BlockSpec((1, sc, LANES), row),
                pl.BlockSpec((1, GDN_AB, sc), tr)]

    return pl.pallas_call(
        _gdn_scan_kernel,
        grid=(b, nsup),
        in_specs=(in_specs(fwd, lambda bi, i: (bi, 0, i))
                  + in_specs(bwd, lambda bi, i: (bi, 0, nsup - 1 - i))
                  + [pl.BlockSpec(state, st)]),
        out_specs=[pl.BlockSpec((1, sc, GDN_V), fwd), pl.BlockSpec((1, sc, GDN_V), bwd),
                   pl.BlockSpec(state, st)],
        out_shape=[jax.ShapeDtypeStruct((b, n, GDN_V), F32), jax.ShapeDtypeStruct((b, n, GDN_V), F32),
                   jax.ShapeDtypeStruct((b,) + state[1:], F32)],
        scratch_shapes=[pltpu.VMEM(state[1:], F32)],
        compiler_params=_cparams(("parallel", "arbitrary")),
        name="gdn_scan",
    )(q, k, v, gb, gbt, q, k, v, gb, gbt, s0)


def _outproj_kernel(oda_ref, of_ref, ob_ref, gate_ref, x_ref, gt1_ref, gng_ref, wout_ref,
                    n2g_ref, sh2_ref, sc2_ref, wr_ref, xnew_ref, ht_ref, aff_ref, afft_ref):
    tb = ht_ref.shape[3]
    w1, w2, _ = _split3(wr_ref[...])

    def mix_and_project(c):
        rows = slice(c * tb, (c + 1) * tb)
        og = of_ref[0, rows, :] + ob_ref[0, rows, :]
        gate = gate_ref[0, rows, :]
        ys = []
        for h in range(GDN_HEADS):
            cols = slice(h * GDN_DV, (h + 1) * GDN_DV)
            t = og[:, cols]
            y = t * lax.rsqrt(jnp.mean(t * t, axis=-1, keepdims=True) + EPS) * gng_ref[...]
            ys.append((y * _silu(gate[:, cols])).astype(BF16))
        y_gdn = jnp.concatenate(ys, axis=1)
        proj = _dot(oda_ref[0, rows, :], wout_ref[0:DA_V, :]) + _dot(y_gdn, wout_ref[DA_V:, :])
        return x_ref[0, rows, :] + gt1_ref[0] * proj

    def norm_and_route(c, xn):
        rows = slice(c * tb, (c + 1) * tb)
        xnew_ref[0, rows, :] = xn
        ms = jnp.mean(xn * xn, axis=-1, keepdims=True)
        hm = (xn * lax.rsqrt(ms + EPS) * n2g_ref[...]) * (1.0 + sc2_ref[0]) + sh2_ref[0]
        ht_ref[0, c] = hm.T.astype(BF16)
        h1, h2, _ = _split3(hm)
        logits = _dot(h1, w1) + (_dot(h1, w2) + _dot(h2, w1))
        lane = lax.broadcasted_iota(jnp.int32, logits.shape, 1)
        logits = jnp.where(lane < N_EXPERTS, logits, NEG_BIG)
        e = jnp.exp(logits - jnp.max(logits, axis=-1, keepdims=True))
        aff = e / jnp.sum(e, axis=-1, keepdims=True)
        aff_ref[0, rows, :] = aff
        afft_ref[0, :, rows] = aff.T[:N_EXPERTS, :]

    nblk = ht_ref.shape[1]
    xn = mix_and_project(0)
    for c in range(nblk):
        xn_next = mix_and_project(c + 1) if c + 1 < nblk else None
        norm_and_route(c, xn)
        xn = xn_next


def _outproj_call(o_da, o_f, o_b, gate, x, gt1, gdn_norm_g, w_out_bf, norm2_g, sh2, sc2, wr_pad, tm, tb):
    b, n, d = x.shape
    assert tm % tb == 0
    row = lambda bi, i: (bi, i, 0)
    vec = lambda bi, i: (bi, 0, 0)
    const2 = lambda bi, i: (0, 0)
    return pl.pallas_call(
        _outproj_kernel,
        grid=(b, n // tm),
        in_specs=[pl.BlockSpec((1, tm, DA_V), row), pl.BlockSpec((1, tm, GDN_V), row),
                  pl.BlockSpec((1, tm, GDN_V), row), pl.BlockSpec((1, tm, GDN_V), row),
                  pl.BlockSpec((1, tm, d), row), pl.BlockSpec((1, 1, d), vec),
                  pl.BlockSpec((1, GDN_DV), const2), pl.BlockSpec((DA_V + GDN_V, d), const2),
                  pl.BlockSpec((1, d), const2), pl.BlockSpec((1, 1, d), vec), pl.BlockSpec((1, 1, d), vec),
                  pl.BlockSpec((d, LANES), const2)],
        out_specs=[pl.BlockSpec((1, tm, d), row),
                   pl.BlockSpec((1, tm // tb, d, tb), lambda bi, i: (bi, i, 0, 0)),
                   pl.BlockSpec((1, tm, LANES), row),
                   pl.BlockSpec((1, N_EXPERTS, tm), lambda bi, i: (bi, 0, i))],
        out_shape=[jax.ShapeDtypeStruct((b, n, d), F32), jax.ShapeDtypeStruct((b, n // tb, d, tb), BF16),
                   jax.ShapeDtypeStruct((b, n, LANES), F32),
                   jax.ShapeDtypeStruct((b, N_EXPERTS, n), F32)],
        compiler_params=_cparams(("parallel", "parallel")),
        name="outproj",
    )(o_da, o_f, o_b, gate, x, gt1, gdn_norm_g, w_out_bf, norm2_g, sh2, sc2, wr_pad)


GATHER_GROUP = 8
COMBINE_ALIGN = 16
PREFIX_BLK = 256
BISECT_STEPS = 40


def _select_kernel(afft_ref, aff_ref, post_ref, pos_ref, st_ref, rng_ref, *, cap, tb, tj):
    at = afft_ref[0]
    n = at.shape[1]

    def count_ge(t):
        return jnp.sum(jnp.where(at >= t, 1.0, 0.0), axis=1, keepdims=True)

    def bisect(_, bounds):
        lo, hi = bounds
        mid = 0.5 * (lo + hi)
        ok = count_ge(mid) >= cap
        return jnp.where(ok, mid, lo), jnp.where(ok, hi, mid)

    _, hi = lax.fori_loop(0, BISECT_STEPS, bisect,
                          (jnp.zeros((N_EXPERTS, 1), F32), jnp.full((N_EXPERTS, 1), 2.0, F32)))

    def below(h):
        return jnp.max(jnp.where(at < h, at, -1.0), axis=1, keepdims=True)

    def short(state):
        return jnp.sum(jnp.where(count_ge(state[0]) < cap, 1.0, 0.0)) > 0.0

    def step_down(state):
        t, h = state
        lacking = count_ge(t) < cap
        h = jnp.where(lacking, t, h)
        return jnp.where(lacking, below(h), t), h

    thr, _ = lax.while_loop(short, step_down, (below(hi), hi))
    need = cap - jnp.sum(jnp.where(at > thr, 1.0, 0.0), axis=1, keepdims=True)

    pi = lax.broadcasted_iota(jnp.int32, (PREFIX_BLK, PREFIX_BLK), 0)
    pj = lax.broadcasted_iota(jnp.int32, (PREFIX_BLK, PREFIX_BLK), 1)
    upper = jnp.where(pi <= pj, 1.0, 0.0).astype(BF16)
    lower = jnp.where(pi >= pj, 1.0, 0.0).astype(BF16)

    def prefix_lanes(m):
        carry = jnp.zeros((m.shape[0], 1), F32)
        outs = []
        for c in range(m.shape[1] // PREFIX_BLK):
            mc = m[:, c * PREFIX_BLK:(c + 1) * PREFIX_BLK]
            loc = _dot(mc.astype(BF16), upper)
            outs.append(loc - mc + carry)
            carry = carry + loc[:, PREFIX_BLK - 1:PREFIX_BLK]
        return jnp.concatenate(outs, axis=1)

    def prefix_rows(m):
        carry = jnp.zeros((1, m.shape[1]), F32)
        outs = []
        for c in range(m.shape[0] // PREFIX_BLK):
            mc = m[c * PREFIX_BLK:(c + 1) * PREFIX_BLK, :]
            loc = _dot(lower, mc.astype(BF16))
            outs.append(loc - mc + carry)
            carry = carry + loc[PREFIX_BLK - 1:PREFIX_BLK, :]
        return jnp.concatenate(outs, axis=0)

    eq_t = jnp.where(at == thr, 1.0, 0.0)
    sel_t = jnp.where(at > thr, 1.0, jnp.where(prefix_lanes(eq_t) < need, eq_t, 0.0))
    post_ref[0] = jnp.where(sel_t > 0.0, prefix_lanes(sel_t), -1.0)

    ti = lax.broadcasted_iota(jnp.int32, (n, LANES), 0)
    bi = lax.broadcasted_iota(jnp.int32, (n, LANES), 1)
    before = jnp.where(ti < bi * tb, 1.0, 0.0).astype(BF16)
    st = _dot(sel_t.astype(BF16), before)
    st_ref[0] = st.astype(jnp.int32)

    through = jnp.where(ti < (bi + 1) * tb, 1.0, 0.0).astype(BF16)
    st_end = _dot(sel_t.astype(BF16), through)
    blk_lane = lax.broadcasted_iota(jnp.int32, (N_EXPERTS, LANES), 1)
    is_blk = blk_lane < n // tb
    rng = jnp.zeros((N_EXPERTS, LANES), F32)
    ntile = cap // tj
    for j in range(ntile):
        first = jnp.sum(jnp.where(is_blk & (st_end <= j * tj), 1.0, 0.0), axis=1, keepdims=True)
        last = jnp.sum(jnp.where(is_blk & (st < (j + 1) * tj), 1.0, 0.0), axis=1, keepdims=True) - 1.0
        rng = rng + jnp.where(blk_lane == j, first, 0.0) + jnp.where(blk_lane == ntile + j, last, 0.0)
    rng_ref[0] = rng.astype(jnp.int32)

    a = aff_ref[0]
    er = lax.broadcasted_iota(jnp.int32, (N_EXPERTS, LANES), 0)
    ec = lax.broadcasted_iota(jnp.int32, (N_EXPERTS, LANES), 1)
    diag = er == ec
    thr_row = jnp.sum(jnp.where(diag, thr, 0.0), axis=0, keepdims=True)
    need_row = jnp.sum(jnp.where(diag, need, 0.0), axis=0, keepdims=True)
    valid = lax.broadcasted_iota(jnp.int32, a.shape, 1) < N_EXPERTS
    eq = jnp.where(valid & (a == thr_row), 1.0, 0.0)
    sel = jnp.where(valid & (a > thr_row), 1.0, jnp.where(prefix_rows(eq) < need_row, eq, 0.0))
    pos_ref[0] = jnp.where(sel > 0.0, prefix_rows(sel), -1.0)


def _select_call(afft, aff, cap, tb, tj):
    b, e, n = afft.shape
    assert n // tb + 1 <= LANES and 2 * (cap // tj) <= LANES
    return pl.pallas_call(
        functools.partial(_select_kernel, cap=cap, tb=tb, tj=tj),
        grid=(b,),
        in_specs=[pl.BlockSpec((1, e, n), lambda bi: (bi, 0, 0)),
                  pl.BlockSpec((1, n, LANES), lambda bi: (bi, 0, 0))],
        out_specs=[pl.BlockSpec((1, e, n), lambda bi: (bi, 0, 0)),
                   pl.BlockSpec((1, n, LANES), lambda bi: (bi, 0, 0)),
                   pl.BlockSpec((1, e, LANES), lambda bi: (bi, 0, 0)),
                   pl.BlockSpec((1, e, LANES), lambda bi: (bi, 0, 0))],
        out_shape=[jax.ShapeDtypeStruct((b, e, n), F32), jax.ShapeDtypeStruct((b, n, LANES), F32),
                   jax.ShapeDtypeStruct((b, e, LANES), jnp.int32),
                   jax.ShapeDtypeStruct((b, e, LANES), jnp.int32)],
        compiler_params=_cparams(("parallel",)),
        name="select",
    )(afft, aff)


def _moe_ffn_kernel(rng_ref, ht_ref, post_ref, afft_ref, wg_ref, wu_ref, wd_ref, ye_ref,
                    acc_ref, gacc_ref, wgb_ref, wub_ref, wdb_ref, *, tj, tb):
    b = pl.program_id(0)
    e = pl.program_id(1)
    cap = ye_ref.shape[2]
    ntile = cap // tj
    slot = lax.broadcasted_iota(jnp.int32, (tj, tb), 0).astype(F32)

    for j in range(cap // tj):
        lo = j * tj
        acc_ref[...] = jnp.zeros(acc_ref.shape, F32)
        gacc_ref[...] = jnp.zeros(gacc_ref.shape, F32)

        first = rng_ref[b, e, j]
        last = rng_ref[b, e, ntile + j]

        def gather_blocks(start, count):
            part = jnp.zeros(acc_ref.shape, F32)
            gate = jnp.zeros(gacc_ref.shape, F32)
            for u in range(count):
                bi = start + u
                prow = post_ref[0, 0, pl.ds(bi, 1), :]
                hit = prow == (slot + float(lo))
                onehot = jnp.where(hit, 1.0, 0.0).astype(BF16)
                part = part + _dot_nt(ht_ref[0, bi], onehot)
                arow = afft_ref[0, 0, pl.ds(bi, 1), :]
                gate = gate + jnp.sum(jnp.where(hit, arow, 0.0), axis=1, keepdims=True)
            acc_ref[...] += part
            gacc_ref[...] += gate

        def group_body(g, carry):
            gather_blocks(first + g * GATHER_GROUP, GATHER_GROUP)
            return carry

        nblocks = last - first + 1
        full = nblocks // GATHER_GROUP
        lax.fori_loop(0, full, group_body, 0)
        done = first + full * GATHER_GROUP
        size = GATHER_GROUP // 2
        while size >= 1:
            @pl.when((nblocks & size) != 0)
            def _(done=done, size=size):
                gather_blocks(done, size)
            done = done + (nblocks & size)
            size //= 2
        xe = acc_ref[...].T.astype(BF16)
        if j == 0:
            wgb_ref[...] = wg_ref[0].astype(BF16)
            wub_ref[...] = wu_ref[0].astype(BF16)
            wdb_ref[...] = wd_ref[0].astype(BF16)
        hid = _silu(_dot(xe, wgb_ref[...])) * _dot(xe, wub_ref[...])
        ye = _dot(hid.astype(BF16), wdb_ref[...]) * gacc_ref[...]
        ye_ref[0, 0, lo:lo + tj, :] = ye.astype(ye_ref.dtype)


def _moe_ffn_call(tile_rng, ht4, post4, afft4, wg, wu, wd, cap, tj):
    b, nblk, d, tb = ht4.shape
    e, _, f = wg.shape
    wspec = lambda shp: pl.BlockSpec((1,) + shp, lambda bi, ei: (ei, 0, 0))
    return pl.pallas_call(
        functools.partial(_moe_ffn_kernel, tj=tj, tb=tb),
        grid=(b, e),
        in_specs=[pl.BlockSpec(memory_space=pltpu.SMEM),
                  pl.BlockSpec((1, nblk, d, tb), lambda bi, ei: (bi, 0, 0, 0), pipeline_mode=pl.Buffered(1)),
                  pl.BlockSpec((1, 1, nblk, tb), lambda bi, ei: (bi, ei, 0, 0)),
                  pl.BlockSpec((1, 1, nblk, tb), lambda bi, ei: (bi, ei, 0, 0)),
                  wspec((d, f)), wspec((d, f)), wspec((f, d))],
        out_specs=pl.BlockSpec((1, 1, cap, d), lambda bi, ei: (bi, ei, 0, 0)),
        out_shape=jax.ShapeDtypeStruct((b, e, cap, d), BF16),
        scratch_shapes=[pltpu.VMEM((d, tj), F32), pltpu.VMEM((tj, 1), F32),
                        pltpu.VMEM((d, f), BF16), pltpu.VMEM((d, f), BF16), pltpu.VMEM((f, d), BF16)],
        compiler_params=_cparams(("parallel", "arbitrary")),
        name="moe_ffn",
    )(tile_rng, ht4, post4, afft4, wg, wu, wd)


def _combine_kernel(st_ref, pos_ref, ye_ref, x_ref, gt2_ref, g_ref, o_ref, rest_ref, *, tb):
    b = pl.program_id(0)
    blk = pl.program_id(1)
    cap = ye_ref.shape[2]
    win = tb
    slot = lax.broadcasted_iota(jnp.int32, (tb, win), 1).astype(F32)
    pos = pos_ref[0]

    def window_base(e):
        return jnp.minimum(st_ref[b, e, blk] // COMBINE_ALIGN, (cap - win) // COMBINE_ALIGN) * COMBINE_ALIGN

    def window(e, start):
        return ye_ref[0, e, pl.ds(pl.multiple_of(start, COMBINE_ALIGN), win), :]

    moe = jnp.zeros((tb, ye_ref.shape[3]), F32)
    for e in range(N_EXPERTS):
        base = window_base(e)
        onehot = jnp.where(pos[:, e:e + 1] == slot + base.astype(F32), 1.0, 0.0).astype(BF16)
        moe = moe + _dot(onehot, window(e, base))

    rest_ref[...] = jnp.zeros(rest_ref.shape, F32)
    for e in range(N_EXPERTS):
        done = window_base(e) + win

        @pl.when(st_ref[b, e, blk + 1] > done)
        def _():
            start = jnp.minimum(done, cap - win)
            pcol = pos[:, e:e + 1]
            hit = (pcol == slot + start.astype(F32)) & (pcol >= done.astype(F32))
            rest_ref[...] += _dot(jnp.where(hit, 1.0, 0.0).astype(BF16), window(e, start))

    y = x_ref[0] + gt2_ref[0] * (moe + rest_ref[...])
    ms = jnp.mean(y * y, axis=-1, keepdims=True)
    o_ref[0] = y * lax.rsqrt(ms + EPS) * g_ref[...]


def _combine_call(starts, pos, ye, x_new, gt2, final_g, tb):
    b, e, cap, d = ye.shape
    n = pos.shape[1]
    assert cap % COMBINE_ALIGN == 0 and tb % COMBINE_ALIGN == 0 and cap >= tb
    row = lambda bi, i: (bi, i, 0)
    return pl.pallas_call(
        functools.partial(_combine_kernel, tb=tb),
        grid=(b, n // tb),
        in_specs=[pl.BlockSpec(memory_space=pltpu.SMEM),
                  pl.BlockSpec((1, tb, LANES), row),
                  pl.BlockSpec((1, e, cap, d), lambda bi, i: (bi, 0, 0, 0), pipeline_mode=pl.Buffered(1)),
                  pl.BlockSpec((1, tb, d), row),
                  pl.BlockSpec((1, 1, d), lambda bi, i: (bi, 0, 0)),
                  pl.BlockSpec((1, d), lambda bi, i: (0, 0))],
        out_specs=pl.BlockSpec((1, tb, d), row),
        out_shape=jax.ShapeDtypeStruct((b, n, d), F32),
        scratch_shapes=[pltpu.VMEM((tb, d), F32)],
        compiler_params=_cparams(("parallel", "arbitrary")),
        name="combine",
    )(starts, pos, ye, x_new, gt2, final_g)


def _rope_tables(n):
    t = np.arange(n)
    rows = (t // GRID_W).astype(np.float64)
    cols = (t % GRID_W).astype(np.float64)
    inv_freq = np.power(ROPE_THETA, -np.arange(0, ROPE_AXIS_DIM, 2, dtype=np.float64) / ROPE_AXIS_DIM)
    ang_row = rows[:, None] * inv_freq[None, :]
    ang_col = cols[:, None] * inv_freq[None, :]

    def axis_tables(ang):
        c = np.cos(ang).astype(np.float32)
        s = np.sin(ang).astype(np.float32)
        return np.concatenate([c, c], axis=1), np.concatenate([-s, s], axis=1)

    cr, sr = axis_tables(ang_row)
    cc, sc = axis_tables(ang_col)
    cos64 = np.concatenate([cr, cc], axis=1)
    sin64 = np.concatenate([sr, sc], axis=1)
    return (jnp.asarray(np.concatenate([cos64, cos64], axis=1)),
            jnp.asarray(np.concatenate([sin64, sin64], axis=1)))


def _attn_chunk(nk):
    return max(t for t in range(VT_BLK, 1024 + 1, VT_BLK) if nk % t == 0)


def _pad_lanes(v):
    return jnp.pad(v.reshape(1, -1), ((0, 0), (0, LANES - v.size)))


def kernel(x, c, ctx, c_ctx, w_mod, b_mod, norm1_g, w_in, conv_w, a_log, dt_bias, gdn_norm_g,
           lam_q1, lam_k1, lam_q2, lam_k2, da_subln_g, w_out, norm2_g,
           w_router, w_gate, w_up, w_down, final_g):
    b, n, d = x.shape
    nc = ctx.shape[1]
    layer = 0

    cvec = jnp.concatenate([c, c_ctx[None, :], jnp.zeros((8 - b - 1, d), F32)], axis=0)
    mod = _mod_call(cvec, w_mod[layer], b_mod[layer])
    sh1, sc1, gt1, sh2, sc2, gt2 = [mod[:b, i * d:(i + 1) * d].reshape(b, 1, d) for i in range(6)]
    sh1c = jnp.broadcast_to(mod[b:b + 1, 0:d].reshape(1, 1, d), (b, 1, d))
    sc1c = jnp.broadcast_to(mod[b:b + 1, d:2 * d].reshape(1, 1, d), (b, 1, d))

    w_in_bf = jnp.pad(w_in[layer].astype(BF16), ((0, 0), (0, IN_COLS_PAD - w_in.shape[2])))
    g1 = norm1_g[layer].reshape(1, d)
    cos_l, sin_l = _rope_tables(n)
    cos_c, sin_c = jnp.ones((nc, LANES), F32), jnp.zeros((nc, LANES), F32)
    conv_w8 = jnp.pad(conv_w[layer], ((0, 8 - conv_w.shape[1]), (0, 0)))
    alog_row = _pad_lanes(a_log[layer])
    dtb_row = _pad_lanes(dt_bias[layer])
    gdn_args = (conv_w8, alog_row, dtb_row)
    q, k, vt, ql, kl, vl, gate, gbl, gbtl = _inproj_call(
        x, g1, sh1, sc1, cos_l, sin_l, w_in_bf, *gdn_args, tm=1024)
    _, kc, vct, qc, kcg, vcg, _, gbc, gbtc = _inproj_call(
        ctx, g1, sh1c, sc1c, cos_c, sin_c, w_in_bf, *gdn_args, tm=nc)

    lam_vecs = jnp.stack([lam_q1[layer], lam_k1[layer], lam_q2[layer], lam_k2[layer]], axis=0)
    k_all = jnp.concatenate([kc, k], axis=1)
    vt_all = jnp.concatenate([vct, vt], axis=2)
    o_da = _attn_call(q, k_all, vt_all, lam_vecs, da_subln_g[layer].reshape(1, DA_DV),
                      tq=n, tk=_attn_chunk(nc + n))

    zeros_state = jnp.zeros((b, 2, GDN_HEADS, GDN_DK, GDN_DV), F32)
    _, _, s_ctx = _gdn_scan_call(qc, kcg, vcg, gbc, gbtc, zeros_state, sc=nc)
    o_fwd, o_bwd, _ = _gdn_scan_call(ql, kl, vl, gbl, gbtl, s_ctx, sc=512)
    o_dirs = [o_fwd, o_bwd]

    wr_pad = jnp.pad(w_router[layer], ((0, 0), (0, LANES - N_EXPERTS)))
    cap = CAP_FACTOR * n // N_EXPERTS
    tb = 256
    tj = 256
    x_new, ht4, aff, afft = _outproj_call(
        o_da, o_dirs[0], o_dirs[1], gate, x, gt1, gdn_norm_g[layer].reshape(1, GDN_DV),
        w_out[layer].astype(BF16), norm2_g[layer].reshape(1, d), sh2, sc2, wr_pad, tm=1024, tb=tb)

    post, pos, starts, tile_rng = _select_call(afft, aff, cap, tb, tj)
    nblk = n // tb
    ye = _moe_ffn_call(tile_rng, ht4, post.reshape(b, N_EXPERTS, nblk, tb),
                       afft.reshape(b, N_EXPERTS, nblk, tb),
                       w_gate[layer], w_up[layer], w_down[layer], cap, tj)
    return _combine_call(starts, pos, ye, x_new, gt2, final_g.reshape(1, d), tb)
```
